```python
import math
import jax, jax.numpy as jnp
from jax import lax
import numpy as np

D_MODEL = 2048
BATCH = 16
SEQ = 256
DEPTH = 2
DEC_BATCH = 8
DEC_SEQ = 2048
PAST_LEN = 256

GRID_W = 64
MIX_WIDTH = D_MODEL
GROUP_W = MIX_WIDTH // 4
N_DIR = 2
GLA_HEADS = 4
GLA_DK = GROUP_W // (2 * GLA_HEADS)
GLA_DV = GROUP_W // GLA_HEADS
GLA_RANK = 16
GLA_TAU = 16.0
RET_HEADS = 4
RET_DK = GROUP_W // (2 * RET_HEADS)
RET_DV = GROUP_W // RET_HEADS
S5_CH = GROUP_W
S5_GROUP = 16
S5_GROUPS = S5_CH // S5_GROUP
S5_STATE = 64
HY_CH = GROUP_W
HY_ORDER = 2
HY_BANDS = 16
HY_EMB = 1 + 2 * HY_BANDS
HY_HIDDEN = 64
HY_SHORT = 3
CHUNK = 64
D_FF = 4 * D_MODEL
ALPHA = (2 * DEPTH) ** 0.25
BETA = (8 * DEPTH) ** -0.25
LN_EPS = 1e-5
NORM_EPS = 1e-6
IN_SPLITS = (GLA_HEADS * GLA_DK, GLA_HEADS * GLA_DK, GLA_HEADS * GLA_DV, GROUP_W, N_DIR * GLA_RANK,
             RET_HEADS * RET_DK, RET_HEADS * RET_DK, RET_HEADS * RET_DV, GROUP_W,
             S5_CH, 3 * HY_CH)
D_IN = sum(IN_SPLITS)

kernel_name = 'hybrid_prefix_diffusion_step'


def _layer_norm(x, g=None, b=None):
    xf = x.astype(jnp.float32)
    mu = jnp.mean(xf, -1, keepdims=True)
    var = jnp.mean(jnp.square(xf - mu), -1, keepdims=True)
    y = (xf - mu) * lax.rsqrt(var + LN_EPS)
    if g is not None:
        y = y * g.astype(jnp.float32) + b.astype(jnp.float32)
    return y.astype(x.dtype)


def _split_heads(x, h):
    bsz, l, _ = x.shape
    return x.reshape(bsz, l, h, -1).transpose(0, 2, 1, 3)


def _merge_heads(x):
    bsz, h, l, d = x.shape
    return x.transpose(0, 2, 1, 3).reshape(bsz, l, h * d)


def _chunk(x):
    bsz, h, l, d = x.shape
    return x.reshape(bsz, h, l // CHUNK, CHUNK, d)


def _flip_t(x):
    return jnp.flip(x, axis=2)


def _scan_chunks(decay, update, s0):
    def step(s, inp):
        d, u = inp
        return d[..., None] * s + u, s
    s_fin, s_prev = lax.scan(step, s0, (jnp.moveaxis(decay, 2, 0), jnp.moveaxis(update, 2, 0)))
    return s_fin, jnp.moveaxis(s_prev, 0, 2)


def _gla_chunked(q, k, v, log_a, s0):
    q, k, v, la = _chunk(q), _chunk(k), _chunk(v), _chunk(log_a)
    b = jnp.cumsum(la, axis=3)
    b_last = b[:, :, :, -1:, :]
    q_d = q * jnp.exp(b)
    k_d = k * jnp.exp(-b)
    k_s = k * jnp.exp(b_last - b)
    lower = jnp.tril(jnp.ones((CHUNK, CHUNK), jnp.float32))
    att = jnp.einsum('bhncd,bhnsd->bhncs', q_d, k_d) * lower
    upd = jnp.einsum('bhnsd,bhnsv->bhndv', k_s, v)
    s_fin, s_prev = _scan_chunks(jnp.exp(b_last[:, :, :, 0]), upd, s0)
    o = jnp.einsum('bhncs,bhnsv->bhncv', att, v) + jnp.einsum('bhncd,bhndv->bhncv', q_d, s_prev)
    bsz, h, n, c, dv = o.shape
    return o.reshape(bsz, h, n * c, dv), s_fin


def _retention_chunked(q, k, v, log_g, s0):
    q, k, v = _chunk(q), _chunk(k), _chunk(v)
    bsz, h, n, c, dk = q.shape
    pos = jnp.arange(CHUNK, dtype=jnp.float32)
    diff = pos[:, None] - pos[None, :]
    lg = log_g[:, None, None]
    decay_mat = jnp.exp(jnp.where(diff >= 0, lg * diff, -jnp.inf))
    w_q = jnp.exp(log_g[:, None] * (pos + 1.0))
    w_k = jnp.exp(log_g[:, None] * (CHUNK - 1.0 - pos))
    att = jnp.einsum('bhncd,bhnsd->bhncs', q, k) * decay_mat[None, :, None]
    upd = jnp.einsum('bhnsd,hs,bhnsv->bhndv', k, w_k, v)
    chunk_decay = jnp.broadcast_to(jnp.exp(log_g * CHUNK)[None, :, None, None], (bsz, h, n, dk))
    s_fin, s_prev = _scan_chunks(chunk_decay, upd, s0)
    o = jnp.einsum('bhncs,bhnsv->bhncv', att, v) + jnp.einsum('bhncd,hc,bhndv->bhncv', q, w_q, s_prev)
    return o.reshape(bsz, h, n * c, -1), s_fin


def _gla_mixer(q, k, v, g, lr, p, s0):
    f32 = jnp.float32
    qh = _split_heads(q.astype(f32), GLA_HEADS) * GLA_DK ** -0.5
    kh = _split_heads(k.astype(f32), GLA_HEADS)
    vh = _split_heads(v.astype(f32), GLA_HEADS)
    lr = lr.astype(f32).reshape(lr.shape[0], lr.shape[1], N_DIR, GLA_RANK)
    logits = jnp.einsum('bldr,drk->dblk', lr, p['gla_w_gate'].astype(f32)) + p['gla_b_gate'].astype(f32)[:, None, None, :]
    log_a = jax.nn.log_sigmoid(logits) / GLA_TAU
    s0 = s0.astype(f32)
    o_f, s_f = _gla_chunked(qh, kh, vh, _split_heads(log_a[0], GLA_HEADS), s0[:, 0])
    o_b, s_b = _gla_chunked(_flip_t(qh), _flip_t(kh), _flip_t(vh), _flip_t(_split_heads(log_a[1], GLA_HEADS)), s0[:, 1])
    o = o_f + _flip_t(o_b)
    o = o * lax.rsqrt(jnp.mean(o * o, -1, keepdims=True) + NORM_EPS) * p['gla_norm_w'].astype(f32)
    out = _merge_heads(o) * jax.nn.silu(g.astype(f32))
    return out, jnp.stack([s_f, s_b], axis=1)


def _ret_mixer(q, k, v, g, p, s0):
    f32 = jnp.float32
    qh = _split_heads(q.astype(f32), RET_HEADS)
    kh = _split_heads(k.astype(f32), RET_HEADS) * RET_DK ** -0.5
    vh = _split_heads(v.astype(f32), RET_HEADS)
    log_g = jnp.log1p(-jnp.exp2(-p['ret_decay_exp'].astype(f32)))
    s0 = s0.astype(f32)
    o_f, s_f = _retention_chunked(qh, kh, vh, log_g[0], s0[:, 0])
    o_b, s_b = _retention_chunked(_flip_t(qh), _flip_t(kh), _flip_t(vh), log_g[1], s0[:, 1])
    o = o_f + _flip_t(o_b)
    mu = jnp.mean(o, -1, keepdims=True)
    var = jnp.mean(jnp.square(o - mu), -1, keepdims=True)
    o = (o - mu) * lax.rsqrt(var + LN_EPS)
    out = _merge_heads(o) * jax.nn.silu(g.astype(f32))
    return out, jnp.stack([s_f, s_b], axis=1)


def _linear_combine(e1, e2):
    a1, b1 = e1
    a2, b2 = e2
    return a1 * a2, a2 * b1 + b2


def _s5_direction(u, lam, step, bmat, cmat, h0):
    lam_bar = jnp.exp(lam * step[:, None])
    b_bar = ((lam_bar - 1.0) / lam)[..., None] * bmat
    bu = jnp.einsum('gph,blgh->blgp', b_bar, u)
    a = jnp.broadcast_to(lam_bar, bu.shape)
    a_cum, h = lax.associative_scan(_linear_combine, (a, bu), axis=1)
    h = h + a_cum * h0[:, None]
    y = jnp.real(jnp.einsum('ghp,blgp->blgh', cmat, h))
    return y, h[:, -1]


def _s5_mixer(u, p, h0_re, h0_im):
    f32 = jnp.float32
    f = lambda n: p[n].astype(f32)
    bsz, l, _ = u.shape
    uf = u.astype(f32)
    ug = uf.reshape(bsz, l, S5_GROUPS, S5_GROUP).astype(jnp.complex64)
    h0 = lax.complex(h0_re.astype(f32), h0_im.astype(f32))
    lam = lax.complex(f('s5_a_re'), f('s5_a_im'))
    bmat = lax.complex(f('s5_b_re'), f('s5_b_im'))
    cmat = lax.complex(f('s5_c_re'), f('s5_c_im'))
    step = jnp.exp(f('s5_log_step'))
    y_f, h_f = _s5_direction(ug, lam[0], step[0], bmat[0], cmat[0], h0[:, 0])
    y_b, h_b = _s5_direction(jnp.flip(ug, 1), lam[1], step[1], bmat[1], cmat[1], h0[:, 1])
    y = (y_f + jnp.flip(y_b, 1)).reshape(bsz, l, S5_CH) + uf * f('s5_d')
    z = jax.nn.gelu(y)
    out = z * jax.nn.sigmoid(z @ f('s5_glu_w') + f('s5_glu_b'))
    h_new = jnp.stack([h_f, h_b], axis=1)
    return out, jnp.real(h_new), jnp.imag(h_new)


def _short_conv(x, w, b, rows):
    bsz, l, ch = x.shape
    n = l // rows
    pad = HY_SHORT // 2
    xp = jnp.pad(x.reshape(bsz, rows, n, ch), ((0, 0), (0, 0), (pad, pad), (0, 0)))
    y = xp[:, :, 0:n] * w[0]
    for i in range(1, HY_SHORT):
        y = y + xp[:, :, i:i + n] * w[i]
    return (y + b).reshape(bsz, l, ch)


def _hyena_filters(l, p):
    f32 = jnp.float32
    g = lambda n: p[n].astype(f32)
    t = jnp.linspace(0.0, 1.0, l, dtype=f32)[:, None]
    w = 2.0 * math.pi * jnp.arange(l, dtype=f32)[:, None] / l
    fr = jnp.linspace(1e-4, HY_BANDS - 1.0, HY_BANDS, dtype=f32)[None, :]
    feats = jnp.concatenate([t, jnp.cos(fr * w), -jnp.sin(fr * w)], axis=-1)
    freq = g('hy_f_freq')
    h = jnp.sin(freq * (feats @ g('hy_f_w1') + g('hy_f_b1')))
    h = jnp.sin(freq * (h @ g('hy_f_w2') + g('hy_f_b2')))
    filt = (h @ g('hy_f_w3')) * jnp.exp(-t * jnp.abs(g('hy_decay')))
    filt = filt.reshape(l, N_DIR, HY_ORDER, HY_CH)
    filt = filt / jnp.sum(jnp.abs(filt), axis=0, keepdims=True)
    two_sided = jnp.concatenate([filt[:, 0], jnp.zeros((1, HY_ORDER, HY_CH), f32), jnp.flip(filt[1:, 1], 0)], axis=0)
    return jnp.fft.rfft(two_sided, axis=0)


def _hyena_mixer(z, p, rows):
    f32 = jnp.float32
    l = z.shape[1]
    zc = _short_conv(z.astype(f32), p['hy_conv_w'].astype(f32), p['hy_conv_b'].astype(f32), rows)
    v, x1, x2 = jnp.split(zc, 3, axis=-1)
    filt = _hyena_filters(l, p)
    d = p['hy_d'].astype(f32)
    y = v
    for o, gate in enumerate((x1, x2)):
        conv = jnp.fft.irfft(jnp.fft.rfft(y, n=2 * l, axis=1) * filt[None, :, o], n=2 * l, axis=1)[:, :l]
        y = gate * (conv + d[o] * y)
    return y


def _mixer(h, p, state, rows):
    s_gla, s_ret, s5_re, s5_im = state
    proj = h @ p['w_in']
    idx = np.cumsum(IN_SPLITS)[:-1].tolist()
    gq, gk, gv, gg, glr, rq, rk, rv, rg, su, hy = jnp.split(proj, idx, axis=-1)
    o_gla, ns_gla = _gla_mixer(gq, gk, gv, gg, glr, p, s_gla)
    o_ret, ns_ret = _ret_mixer(rq, rk, rv, rg, p, s_ret)
    o_s5, ns_re, ns_im = _s5_mixer(su, p, s5_re, s5_im)
    o_hy = _hyena_mixer(hy, p, rows)
    mix = jnp.concatenate([o_gla, o_ret, o_s5, o_hy], axis=-1).astype(h.dtype)
    return mix @ p['w_out'], (ns_gla, ns_ret, ns_re, ns_im)


def _layer(x, mod, p, state, rows):
    sh1, sc1, g1, sh2, sc2, g2 = jnp.split(mod[:, None, :], 6, axis=-1)
    h = _layer_norm(x) * (1.0 + sc1) + sh1
    mix, new_state = _mixer(h, p, state, rows)
    x = _layer_norm(ALPHA * x + g1 * mix, p['ln1_g'], p['ln1_b'])
    h = _layer_norm(x) * (1.0 + sc2) + sh2
    ff = jnp.square(jax.nn.relu(h @ p['w_up'])) @ p['w_down']
    x = _layer_norm(ALPHA * x + g2 * ff, p['ln2_g'], p['ln2_b'])
    return x, new_state


def setup_inputs(seed: int = 0) -> dict:
    key = jax.random.key(seed)
    ks = iter(jax.random.split(key, 64))
    f32 = jnp.float32
    nrm = lambda shape, s: s * jax.random.normal(next(ks), shape, f32)
    inp = {}
    inp['x_prompt'] = nrm((BATCH, SEQ, D_MODEL), 1.0)
    inp['x_sample'] = nrm((DEC_BATCH, DEC_SEQ, D_MODEL), 1.0)
    inp['state_gla'] = nrm((DEC_BATCH, DEPTH, N_DIR, GLA_HEADS, GLA_DK, GLA_DV), 0.1)
    inp['state_ret'] = nrm((DEC_BATCH, DEPTH, N_DIR, RET_HEADS, RET_DK, RET_DV), 0.1)
    inp['state_s5_re'] = nrm((DEC_BATCH, DEPTH, N_DIR, S5_GROUPS, S5_STATE), 0.1)
    inp['state_s5_im'] = nrm((DEC_BATCH, DEPTH, N_DIR, S5_GROUPS, S5_STATE), 0.1)
    inp['c'] = nrm((DEC_BATCH, D_MODEL), 1.0)
    inp['c_ctx'] = nrm((D_MODEL,), 1.0)
    inp['ada_w'] = nrm((DEPTH, D_MODEL, 6 * D_MODEL), 0.5 * D_MODEL ** -0.5)
    inp['ada_b'] = nrm((DEPTH, 6 * D_MODEL), 0.1)
    inp['w_in'] = nrm((DEPTH, D_MODEL, D_IN), D_MODEL ** -0.5)
    inp['gla_w_gate'] = nrm((DEPTH, N_DIR, GLA_RANK, GLA_HEADS * GLA_DK), GLA_RANK ** -0.5)
    inp['gla_b_gate'] = 1.0 + nrm((DEPTH, N_DIR, GLA_HEADS * GLA_DK), 0.5)
    inp['gla_norm_w'] = 1.0 + nrm((DEPTH, GLA_DV), 0.05)
    inp['ret_decay_exp'] = (5.0 + jnp.arange(RET_HEADS, dtype=f32)[None, None, :]
                            + 0.5 * jnp.arange(N_DIR, dtype=f32)[None, :, None]
                            + nrm((DEPTH, N_DIR, RET_HEADS), 0.05))
    inp['s5_a_re'] = -0.5 + nrm((DEPTH, N_DIR, S5_GROUPS, S5_STATE), 0.01)
    inp['s5_a_im'] = math.pi * jnp.arange(S5_STATE, dtype=f32) + nrm((DEPTH, N_DIR, S5_GROUPS, S5_STATE), 0.01)
    inp['s5_log_step'] = jax.random.uniform(next(ks), (DEPTH, N_DIR, S5_GROUPS), f32, math.log(1e-3), math.log(1e-1))
    inp['s5_b_re'] = nrm((DEPTH, N_DIR, S5_GROUPS, S5_STATE, S5_GROUP), (2 * S5_GROUP) ** -0.5)
    inp['s5_b_im'] = nrm((DEPTH, N_DIR, S5_GROUPS, S5_STATE, S5_GROUP), (2 * S5_GROUP) ** -0.5)
    inp['s5_c_re'] = nrm((DEPTH, N_DIR, S5_GROUPS, S5_GROUP, S5_STATE), S5_STATE ** -0.5)
    inp['s5_c_im'] = nrm((DEPTH, N_DIR, S5_GROUPS, S5_GROUP, S5_STATE), S5_STATE ** -0.5)
    inp['s5_d'] = nrm((DEPTH, S5_CH), 1.0)
    inp['s5_glu_w'] = nrm((DEPTH, S5_CH, S5_CH), S5_CH ** -0.5)
    inp['s5_glu_b'] = nrm((DEPTH, S5_CH), 0.1)
    inp['hy_conv_w'] = nrm((DEPTH, HY_SHORT, 3 * HY_CH), HY_SHORT ** -0.5)
    inp['hy_conv_b'] = nrm((DEPTH, 3 * HY_CH), 0.1)
    inp['hy_f_w1'] = nrm((DEPTH, HY_EMB, HY_HIDDEN), HY_EMB ** -0.5)
    inp['hy_f_b1'] = nrm((DEPTH, HY_HIDDEN), 0.5)
    inp['hy_f_w2'] = nrm((DEPTH, HY_HIDDEN, HY_HIDDEN), HY_HIDDEN ** -0.5)
    inp['hy_f_b2'] = nrm((DEPTH, HY_HIDDEN), 0.5)
    inp['hy_f_freq'] = 1.0 + nrm((DEPTH, HY_HIDDEN), 0.1)
    inp['hy_f_w3'] = nrm((DEPTH, HY_HIDDEN, N_DIR * HY_ORDER * HY_CH), HY_HIDDEN ** -0.5)
    base_decay = jnp.abs(jnp.linspace(math.log(1e-2) / 1.5, math.log(1e-2) / 0.3, HY_CH, dtype=f32))
    inp['hy_decay'] = jnp.tile(base_decay, N_DIR * HY_ORDER)[None, :] + nrm((DEPTH, N_DIR * HY_ORDER * HY_CH), 0.1)
    inp['hy_d'] = nrm((DEPTH, HY_ORDER, HY_CH), 0.5)
    inp['w_out'] = nrm((DEPTH, MIX_WIDTH, D_MODEL), BETA * MIX_WIDTH ** -0.5)
    inp['ln1_g'] = 1.0 + nrm((DEPTH, D_MODEL), 0.05)
    inp['ln1_b'] = nrm((DEPTH, D_MODEL), 0.05)
    inp['w_up'] = nrm((DEPTH, D_MODEL, D_FF), D_MODEL ** -0.5)
    inp['w_down'] = nrm((DEPTH, D_FF, D_MODEL), BETA * D_FF ** -0.5)
    inp['ln2_g'] = 1.0 + nrm((DEPTH, D_MODEL), 0.05)
    inp['ln2_b'] = nrm((DEPTH, D_MODEL), 0.05)
    return inp


def reference(x_prompt, x_sample, state_gla, state_ret, state_s5_re, state_s5_im, c, c_ctx,
              ada_w, ada_b, w_in, gla_w_gate, gla_b_gate, gla_norm_w, ret_decay_exp,
              s5_a_re, s5_a_im, s5_log_step, s5_b_re, s5_b_im, s5_c_re, s5_c_im, s5_d, s5_glu_w, s5_glu_b,
              hy_conv_w, hy_conv_b, hy_f_w1, hy_f_b1, hy_f_w2, hy_f_b2, hy_f_freq, hy_f_w3, hy_decay, hy_d,
              w_out, ln1_g, ln1_b, w_up, w_down, ln2_g, ln2_b):
    f32 = jnp.float32
    stacked = dict(w_in=w_in, gla_w_gate=gla_w_gate, gla_b_gate=gla_b_gate, gla_norm_w=gla_norm_w,
                   ret_decay_exp=ret_decay_exp, s5_a_re=s5_a_re, s5_a_im=s5_a_im, s5_log_step=s5_log_step,
                   s5_b_re=s5_b_re, s5_b_im=s5_b_im, s5_c_re=s5_c_re, s5_c_im=s5_c_im, s5_d=s5_d,
                   s5_glu_w=s5_glu_w, s5_glu_b=s5_glu_b, hy_conv_w=hy_conv_w, hy_conv_b=hy_conv_b,
                   hy_f_w1=hy_f_w1, hy_f_b1=hy_f_b1, hy_f_w2=hy_f_w2, hy_f_b2=hy_f_b2, hy_f_freq=hy_f_freq,
                   hy_f_w3=hy_f_w3, hy_decay=hy_decay, hy_d=hy_d, w_out=w_out, ln1_g=ln1_g, ln1_b=ln1_b,
                   w_up=w_up, w_down=w_down, ln2_g=ln2_g, ln2_b=ln2_b)
    bp = x_prompt.shape[0]
    zero_state = (jnp.zeros((bp, N_DIR, GLA_HEADS, GLA_DK, GLA_DV), f32),
                  jnp.zeros((bp, N_DIR, RET_HEADS, RET_DK, RET_DV), f32),
                  jnp.zeros((bp, N_DIR, S5_GROUPS, S5_STATE), f32),
                  jnp.zeros((bp, N_DIR, S5_GROUPS, S5_STATE), f32))
    rows = x_sample.shape[1] // GRID_W
    yp, ys = x_prompt, x_sample
    ctx_gla, ctx_ret, ctx_re, ctx_im = [], [], [], []
    for l in range(DEPTH):
        p = {name: arr[l] for name, arr in stacked.items()}
        mod_ctx = jax.nn.silu(c_ctx)[None, :] @ ada_w[l] + ada_b[l]
        mod_lat = jax.nn.silu(c) @ ada_w[l] + ada_b[l]
        yp, (sg, sr, s5r, s5i) = _layer(yp, mod_ctx, p, zero_state, 1)
        ctx_gla.append(sg)
        ctx_ret.append(sr)
        ctx_re.append(s5r)
        ctx_im.append(s5i)
        cached = (state_gla[:, l], state_ret[:, l], state_s5_re[:, l], state_s5_im[:, l])
        ys, _ = _layer(ys, mod_lat, p, cached, rows)
    new_state_gla = jnp.stack(ctx_gla, axis=1)
    new_state_ret = jnp.stack(ctx_ret, axis=1)
    new_state_s5_re = jnp.stack(ctx_re, axis=1)
    new_state_s5_im = jnp.stack(ctx_im, axis=1)
    return (yp, ys, new_state_gla, new_state_ret, new_state_s5_re, new_state_s5_im)
```

```python
import functools
import math

import numpy as np
import jax
import jax.numpy as jnp
from jax import lax
from jax.experimental import pallas as pl
from jax.experimental.pallas import tpu as pltpu

f32 = jnp.float32
bf16 = jnp.bfloat16

GRID_W = 64
HEADS = 4
DK = 64
DV = 128
GLA_RANK = 16
GLA_TAU = 16.0
S5_GROUP = 16
S5_STATE = 64
HY_BANDS = 16
HY_SHORT = 3
CHUNK = 64
LN_EPS = 1e-5
NORM_EPS = 1e-6

LANES = 128
SUBLANES = 8
VMEM_LIMIT = 52 * 1024 * 1024

GRP = HEADS * DV
OFF_GLA = 0
OFF_RET = 3 * GRP
OFF_S5 = 6 * GRP
OFF_HY = 7 * GRP
OFF_LR = 10 * GRP
NP_IN = 5376
TN_IN = 1792


def _cparams(sem, vmem=VMEM_LIMIT):
    return pltpu.CompilerParams(dimension_semantics=sem, vmem_limit_bytes=vmem)


def _ln(x):
    mu = jnp.mean(x, axis=-1, keepdims=True)
    xc = x - mu
    var = jnp.mean(xc * xc, axis=-1, keepdims=True)
    return xc * lax.rsqrt(var + LN_EPS)


def _silu(x):
    return x * jax.nn.sigmoid(x)


def _bdot(a, b):
    return jnp.dot(a.astype(bf16), b.astype(bf16), preferred_element_type=f32)


def _bdot_nt(a, b):
    return lax.dot_general(a.astype(bf16), b.astype(bf16), (((1,), (1,)), ((), ())), preferred_element_type=f32)


def _bdot_tn(a, b):
    return lax.dot_general(a.astype(bf16), b.astype(bf16), (((0,), (0,)), ((), ())), preferred_element_type=f32)


def _split3(x):
    hi = x.astype(bf16)
    r = x - hi.astype(f32)
    mid = r.astype(bf16)
    lo = (r - mid.astype(f32)).astype(bf16)
    return hi, mid, lo


def _dot_exact_lhs(a_bf, x):
    hi, mid, lo = _split3(x)
    d = lambda b: jnp.dot(a_bf, b, preferred_element_type=f32)
    return d(hi) + (d(mid) + d(lo))


def _dot3(a, b):
    ah = a.astype(bf16)
    al = (a - ah.astype(f32)).astype(bf16)
    bh = b.astype(bf16)
    bl = (b - bh.astype(f32)).astype(bf16)
    d = lambda p, q: jnp.dot(p, q, preferred_element_type=f32)
    return d(ah, bh) + (d(al, bh) + d(ah, bl))


def _ada_kernel(c_ref, w_ref, b_ref, o_ref):
    s = _silu(c_ref[...])
    o_ref[...] = _bdot(s, w_ref[...]) + b_ref[...]


def _ada(c_rows, ada_w, ada_b):
    depth, d, n = ada_w.shape
    r = c_rows.shape[0]
    tn = 1024
    return pl.pallas_call(
        _ada_kernel,
        out_shape=jax.ShapeDtypeStruct((depth, r, n), f32),
        grid=(depth, n // tn),
        in_specs=[pl.BlockSpec((r, d), lambda l, j: (0, 0)),
                  pl.BlockSpec((None, d, tn), lambda l, j: (l, 0, j)),
                  pl.BlockSpec((None, 1, tn), lambda l, j: (l, 0, j))],
        out_specs=pl.BlockSpec((None, r, tn), lambda l, j: (l, 0, j)),
        compiler_params=_cparams(("parallel", "parallel")),
        name="ada",
    )(c_rows, ada_w, ada_b.reshape(depth, 1, n))


def _in_kernel(x_ref, mod_ref, w_ref, o_ref, h_ref):
    @pl.when(pl.program_id(1) == 0)
    def _():
        y = _ln(x_ref[...])
        h_ref[...] = (y * (1.0 + mod_ref[1:2, :]) + mod_ref[0:1, :]).astype(bf16)

    o_ref[...] = jnp.dot(h_ref[...], w_ref[...], preferred_element_type=f32)


def _in_proj(x, mod, layer, row_fn, w_perm, tm):
    t, d = x.shape
    return pl.pallas_call(
        _in_kernel,
        out_shape=jax.ShapeDtypeStruct((t, NP_IN), f32),
        grid=(t // tm, NP_IN // TN_IN),
        in_specs=[pl.BlockSpec((tm, d), lambda i, j: (i, 0)),
                  pl.BlockSpec((None, None, 6, d), lambda i, j: (layer, row_fn(i), 0, 0)),
                  pl.BlockSpec((d, TN_IN), lambda i, j: (0, j))],
        out_specs=pl.BlockSpec((tm, TN_IN), lambda i, j: (i, j)),
        scratch_shapes=[pltpu.VMEM((tm, d), bf16)],
        compiler_params=_cparams(("parallel", "arbitrary")),
        name="in_proj",
    )(x, mod, w_perm)


def _linattn_kernel(*refs, mode, seq, has_s0, want_state):
    it = iter(refs)
    qk_ref, v_ref, g_ref = next(it), next(it), next(it)
    if mode == "gla":
        lr_ref, wg_ref, bg_ref, nw_ref = next(it), next(it), next(it), next(it)
    else:
        dexp_ref = next(it)
    s0_ref = next(it) if has_s0 else None
    o_ref = next(it)
    sf_ref = next(it) if want_state else None
    of_ref = next(it)

    c = CHUNK
    nchunk = seq // c
    row = lax.broadcasted_iota(jnp.int32, (c, c), 0)
    col = lax.broadcasted_iota(jnp.int32, (c, c), 1)
    scale = DK ** -0.5

    for d in (0, 1):
        causal = (row >= col) if d == 0 else (row <= col)
        if mode == "gla":
            tri = jnp.where(causal, 1.0, 0.0).astype(bf16)
            wg = wg_ref[d]
            bg = bg_ref[d]
            last = c - 1 if d == 0 else 0
        else:
            lgv = jnp.log1p(-jnp.exp2(-dexp_ref[d]))
            lg = jnp.broadcast_to(lgv[0:1, 0:c], (c, c))
            dist = (row - col) if d == 0 else (col - row)
            pos = row if d == 0 else (c - 1 - row)
            mask = jnp.where(causal, jnp.exp(lg * dist.astype(f32)), 0.0)
            w_q = jnp.exp(lg * (pos + 1).astype(f32))
            w_k = jnp.exp(lg * (c - 1 - pos).astype(f32)) * scale
            dec = jnp.exp(lgv[0:1, 0:DK] * float(c))

        def body(i, st, d=d):
            n = i if d == 0 else nchunk - 1 - i
            r0 = pl.multiple_of(n * c, c)
            qk = qk_ref[pl.ds(r0, c), :]
            q = qk[:, 0:DK]
            k = qk[:, DK:2 * DK]
            v = v_ref[pl.ds(r0, c), :]
            if mode == "gla":
                logits = _bdot(lr_ref[pl.ds(r0, c), :], wg) + bg
                la = (jnp.minimum(logits, 0.0) - jnp.log1p(jnp.exp(-jnp.abs(logits)))) * (1.0 / GLA_TAU)
                b = _dot_exact_lhs(tri, la)
                btot = b[last:last + 1, :]
                q_o = q * (jnp.exp(b) * scale)
                att = jnp.where(causal, _bdot_nt(q_o, k * jnp.exp(-b)), 0.0)
                k_u = k * jnp.exp(btot - b)
                decay = jnp.exp(btot)
            else:
                att = _bdot_nt(q, k * scale) * mask
                q_o = q * w_q
                k_u = k * w_k
                decay = dec
            o = _bdot(att, v) + _bdot_nt(q_o, st)
            st = decay * st + _bdot_tn(v, k_u)
            if d == 0:
                of_ref[pl.ds(r0, c), :] = o
            else:
                o = o + of_ref[pl.ds(r0, c), :]
                if mode == "gla":
                    o = o * lax.rsqrt(jnp.mean(o * o, axis=-1, keepdims=True) + NORM_EPS) * nw_ref[...]
                else:
                    o = _ln(o)
                o_ref[pl.ds(r0, c), :] = o * _silu(g_ref[pl.ds(r0, c), :])
            return st

        st0 = s0_ref[d] if has_s0 else jnp.zeros((DV, DK), f32)
        st = lax.fori_loop(0, nchunk, body, st0)
        if want_state:
            sf_ref[d] = st


def _linattn(proj, col0, extra, s0t, layer, *, mode, nbatch, seq, want_state):
    cb = col0 // LANES
    rowblk = lambda b, h: b
    in_specs = [pl.BlockSpec((seq, LANES), lambda b, h: (b, cb + h)),
                pl.BlockSpec((seq, LANES), lambda b, h: (b, cb + HEADS + h)),
                pl.BlockSpec((seq, LANES), lambda b, h: (b, cb + 2 * HEADS + h))]
    args = [proj, proj, proj]
    if mode == "gla":
        wg, bg, nw = extra
        in_specs += [pl.BlockSpec((seq, LANES), lambda b, h: (b, OFF_LR // LANES)),
                     pl.BlockSpec((2, None, LANES, DK), lambda b, h: (0, h, 0, 0)),
                     pl.BlockSpec((2, None, 1, DK), lambda b, h: (0, h, 0, 0)),
                     pl.BlockSpec((1, DV), lambda b, h: (0, 0))]
        args += [proj, wg, bg, nw]
    else:
        (dexp,) = extra
        in_specs += [pl.BlockSpec((2, None, SUBLANES, LANES), lambda b, h: (0, h, 0, 0))]
        args += [dexp]
    has_s0 = s0t is not None
    if has_s0:
        in_specs += [pl.BlockSpec((None, None, 2, None, DV, DK), lambda b, h: (b, layer, 0, h, 0, 0))]
        args += [s0t]
    out_shape = [jax.ShapeDtypeStruct((nbatch * seq, GRP), f32)]
    out_specs = [pl.BlockSpec((seq, LANES), lambda b, h: (b, h))]
    if want_state:
        out_shape += [jax.ShapeDtypeStruct((nbatch, 2, HEADS, DV, DK), f32)]
        out_specs += [pl.BlockSpec((None, 2, None, DV, DK), lambda b, h: (b, 0, h, 0, 0))]
    res = pl.pallas_call(
        functools.partial(_linattn_kernel, mode=mode, seq=seq, has_s0=has_s0, want_state=want_state),
        out_shape=out_shape,
        grid=(nbatch, HEADS),
        in_specs=in_specs,
        out_specs=out_specs,
        scratch_shapes=[pltpu.VMEM((seq, DV), f32)],
        compiler_params=_cparams(("parallel", "parallel")),
        name="linattn_" + mode,
    )(*args)
    return (res[0], res[1]) if want_state else (res[0], None)


def _s5_prep_kernel(are_ref, aim_ref, lstep_ref, bre_ref, bim_ref, lre_ref, lim_ref, bbre_ref, bbim_ref):
    ar, ai = are_ref[...], aim_ref[...]
    st = jnp.exp(lstep_ref[...])
    mag = jnp.exp(ar * st)
    lr = mag * jnp.cos(ai * st)
    li = mag * jnp.sin(ai * st)
    nr, ni = lr - 1.0, li
    den = ar * ar + ai * ai
    kr = (nr * ar + ni * ai) / den
    ki = (ni * ar - nr * ai) / den
    br, bi = bre_ref[...], bim_ref[...]
    lre_ref[...] = lr
    lim_ref[...] = li
    bbre_ref[...] = kr * br - ki * bi
    bbim_ref[...] = kr * bi + ki * br


def _s5_prep(a_re, a_im, log_step, b_re, b_im):
    _, g, p = a_re.shape
    hch = b_re.shape[-1]
    rows = 2 * g * hch
    bc = lambda a: jnp.broadcast_to(a[:, :, None, :], (2, g, hch, p)).reshape(rows, p)
    tr = lambda b: jnp.transpose(b, (0, 1, 3, 2)).reshape(rows, p)
    ls = jnp.broadcast_to(log_step[:, :, None, None], (2, g, hch, p)).reshape(rows, p)
    spec = pl.BlockSpec((rows, p), lambda: (0, 0))
    lre, lim, bbre, bbim = pl.pallas_call(
        _s5_prep_kernel,
        out_shape=[jax.ShapeDtypeStruct((rows, p), f32)] * 4,
        in_specs=[spec] * 5,
        out_specs=[spec] * 4,
        name="s5_prep",
    )(bc(a_re), bc(a_im), ls, tr(b_re), tr(b_im))
    r4 = lambda a: a.reshape(2, g, hch, p)
    return r4(lre)[:, :, 0, :], r4(lim)[:, :, 0, :], r4(bbre), r4(bbim)


S5_TC = 64
S5_GPT = LANES // S5_GROUP
S5_SW = S5_GPT * S5_STATE


def _s5_scan_kernel(*refs, seq, nb, has_h0, want_state):
    it = iter(refs)
    u_ref, wb_ref, lam_ref, cre_ref, cim_ref, dsk_ref = (next(it) for _ in range(6))
    h0re_ref = next(it) if has_h0 else None
    h0im_ref = next(it) if has_h0 else None
    y_ref = next(it)
    hfre_ref = next(it) if want_state else None
    hfim_ref = next(it) if want_state else None
    bu_ref, hs_ref = next(it), next(it)

    tc = S5_TC
    nchunk = seq // tc
    sw = S5_SW
    for d in (0, 1):
        lam_r = jnp.broadcast_to(lam_ref[d, 0:1, :], (nb, sw))
        lam_i = jnp.broadcast_to(lam_ref[d, 1:2, :], (nb, sw))

        def chunk(i, carry, d=d, lam_r=lam_r, lam_i=lam_i):
            hre, him = carry
            n = i if d == 0 else nchunk - 1 - i
            t0 = pl.multiple_of(n * tc, tc)
            ub = u_ref[pl.ds(t0, tc)]
            bu_ref[...] = _bdot(ub.reshape(tc * nb, LANES), wb_ref[d])
            order = range(tc) if d == 0 else range(tc - 1, -1, -1)
            for t in order:
                rs = slice(t * nb, (t + 1) * nb)
                nre = lam_r * hre - lam_i * him + bu_ref[rs, 0:sw]
                nim = lam_r * him + lam_i * hre + bu_ref[rs, sw:2 * sw]
                hre, him = nre, nim
                hs_ref[rs, 0:sw] = hre
                hs_ref[rs, sw:2 * sw] = him
            y = _bdot(hs_ref[:, 0:sw], cre_ref[d]) - _bdot(hs_ref[:, sw:2 * sw], cim_ref[d])
            y = y.reshape(tc, nb, LANES)
            if d == 0:
                y_ref[pl.ds(t0, tc)] = y + ub * dsk_ref[...]
            else:
                y_ref[pl.ds(t0, tc)] = y_ref[pl.ds(t0, tc)] + y
            return hre, him

        if has_h0:
            h0 = (h0re_ref[d], h0im_ref[d])
        else:
            h0 = (jnp.zeros((nb, sw), f32), jnp.zeros((nb, sw), f32))
        hre, him = lax.fori_loop(0, nchunk, chunk, h0)
        if want_state:
            hfre_ref[d] = hre
            hfim_ref[d] = him


def _s5_scan(u_t, wb, lam, cre, cim, dsk, h0, *, want_state):
    seq, nb, ch = u_t.shape
    ntile = ch // LANES
    sw = S5_SW
    in_specs = [pl.BlockSpec((seq, nb, LANES), lambda j: (0, 0, j)),
                pl.BlockSpec((2, None, LANES, 2 * sw), lambda j: (0, j, 0, 0)),
                pl.BlockSpec((2, None, 2, sw), lambda j: (0, j, 0, 0)),
                pl.BlockSpec((2, None, sw, LANES), lambda j: (0, j, 0, 0)),
                pl.BlockSpec((2, None, sw, LANES), lambda j: (0, j, 0, 0)),
                pl.BlockSpec((1, LANES), lambda j: (0, j))]
    args = [u_t, wb, lam, cre, cim, dsk]
    has_h0 = h0 is not None
    st_spec = pl.BlockSpec((2, None, nb, sw), lambda j: (0, j, 0, 0))
    if has_h0:
        in_specs += [st_spec, st_spec]
        args += list(h0)
    out_shape = [jax.ShapeDtypeStruct((seq, nb, ch), f32)]
    out_specs = [pl.BlockSpec((seq, nb, LANES), lambda j: (0, 0, j))]
    if want_state:
        out_shape += [jax.ShapeDtypeStruct((2, ntile, nb, sw), f32)] * 2
        out_specs += [st_spec, st_spec]
    res = pl.pallas_call(
        functools.partial(_s5_scan_kernel, seq=seq, nb=nb, has_h0=has_h0, want_state=want_state),
        out_shape=out_shape,
        grid=(ntile,),
        in_specs=in_specs,
        out_specs=out_specs,
        scratch_shapes=[pltpu.VMEM((S5_TC * nb, 2 * sw), f32), pltpu.VMEM((S5_TC * nb, 2 * sw), f32)],
        compiler_params=_cparams(("parallel",)),
        name="s5_scan",
    )(*args)
    return (res[0], res[1], res[2]) if want_state else (res[0], None, None)


def _s5_glu_kernel(y_ref, w_ref, b_ref, o_ref):
    y = y_ref[...]
    z = y * (0.5 * (1.0 + jnp.tanh(math.sqrt(2.0 / math.pi) * (y + 0.044715 * (y * y * y)))))
    o_ref[...] = z * jax.nn.sigmoid(_bdot(z, w_ref[...]) + b_ref[...])


def _s5_glu(y, w, b, tm):
    t, ch = y.shape
    return pl.pallas_call(
        _s5_glu_kernel,
        out_shape=jax.ShapeDtypeStruct((t, ch), f32),
        grid=(t // tm,),
        in_specs=[pl.BlockSpec((tm, ch), lambda i: (i, 0)),
                  pl.BlockSpec((ch, ch), lambda i: (0, 0)),
                  pl.BlockSpec((1, ch), lambda i: (0, 0))],
        out_specs=pl.BlockSpec((tm, ch), lambda i: (i, 0)),
        compiler_params=_cparams(("parallel",)),
        name="s5_glu",
    )(y, w, b)


HY_TK = 256
HY_CT = 256


def _hy_filter_kernel(feat_ref, w1_ref, b1_ref, w2_ref, b2_ref, fq_ref, w3f_ref, w3b_ref, dcf_ref, dcb_ref,
                      fre_ref, fim_ref, fa_ref, fb_ref, fd_ref, fs_ref, fdif_ref, nyq_ref, *, seq):
    j = pl.program_id(1)

    @pl.when(j == 0)
    def _():
        feat = feat_ref[...]
        fq = fq_ref[...]
        h = jnp.sin(fq * (_dot3(feat, w1_ref[...]) + b1_ref[...]))
        h = jnp.sin(fq * (_dot3(h, w2_ref[...]) + b2_ref[...]))
        t = feat[:, 0:1]
        rowi = lax.broadcasted_iota(jnp.int32, (seq, HY_CT), 0)

        def filt(w3_ref, dc_ref):
            f = _dot3(h, w3_ref[...]) * jnp.exp(-t * jnp.abs(dc_ref[...]))
            return f / jnp.sum(jnp.abs(f), axis=0, keepdims=True)

        ff = filt(w3f_ref, dcf_ref)
        fb = jnp.where(rowi == 0, 0.0, filt(w3b_ref, dcb_ref))
        fsum = ff + fb
        fs_ref[...] = fsum.astype(bf16)
        fdif_ref[...] = (ff - fb).astype(bf16)
        sign = jnp.where((rowi & 1) == 0, 1.0, -1.0)
        nyq_ref[...] = jnp.sum(fsum * sign, axis=0, keepdims=True)

    scl = 1.0 / seq
    fa = jnp.dot(fre_ref[...], fs_ref[...], preferred_element_type=f32) * scl
    fb = jnp.dot(fim_ref[...], fdif_ref[...], preferred_element_type=f32) * scl
    first = jnp.logical_and(lax.broadcasted_iota(jnp.int32, (HY_TK, HY_CT), 0) == 0, j == 0)
    fa_ref[...] = jnp.where(first, 0.5 * fa, fa)
    fb_ref[...] = jnp.where(first, 0.0, fb)
    fd_ref[...] = jnp.where(first, (0.5 * scl) * nyq_ref[...], fa)


def _hy_filter(feat, w1p, b1, w2, b2, fq, w3, dc, fre, fim):
    seq = feat.shape[0]
    ncol = w3.shape[1] // 2
    nct = ncol // HY_CT
    tk = min(HY_TK, seq)
    assert tk == HY_TK
    full = lambda a: pl.BlockSpec(a.shape, lambda c, j: (0,) * a.ndim)
    out_spec = pl.BlockSpec((tk, HY_CT), lambda c, j: (j, c))
    return pl.pallas_call(
        functools.partial(_hy_filter_kernel, seq=seq),
        out_shape=[jax.ShapeDtypeStruct((seq, ncol), f32)] * 3,
        grid=(nct, seq // tk),
        in_specs=[full(feat), full(w1p), full(b1), full(w2), full(b2), full(fq),
                  pl.BlockSpec((w3.shape[0], HY_CT), lambda c, j: (0, c)),
                  pl.BlockSpec((w3.shape[0], HY_CT), lambda c, j: (0, nct + c)),
                  pl.BlockSpec((1, HY_CT), lambda c, j: (0, c)),
                  pl.BlockSpec((1, HY_CT), lambda c, j: (0, nct + c)),
                  pl.BlockSpec((tk, seq), lambda c, j: (j, 0)),
                  pl.BlockSpec((tk, seq), lambda c, j: (j, 0))],
        out_specs=[out_spec] * 3,
        scratch_shapes=[pltpu.VMEM((seq, HY_CT), bf16), pltpu.VMEM((seq, HY_CT), bf16), pltpu.VMEM((1, HY_CT), f32)],
        compiler_params=_cparams(("parallel", "arbitrary")),
        name="hy_filter",
    )(feat, w1p, b1, w2, b2, fq, w3, w3, dc, dc, fre, fim)


def _short_conv(x, w_ref, b_ref, seg):
    rows = x.shape[0]
    r = lax.broadcasted_iota(jnp.int32, x.shape, 0) & (seg - 1)
    xp = jnp.where(r == 0, 0.0, pltpu.roll(x, 1, 0))
    xn = jnp.where(r == seg - 1, 0.0, pltpu.roll(x, rows - 1, 0))
    return xp * w_ref[0:1, :] + x * w_ref[1:2, :] + xn * w_ref[2:3, :] + b_ref[...]


def _longconv_kernel(yin_ref, gate_ref, cwy_ref, cby_ref, cwg_ref, cbg_ref, fre_ref, fim_ref, ire_ref, iim_ref,
                     fa_ref, fb_ref, fd_ref, dsk_ref, o_ref, y32_ref, ybf_ref, acc_ref, *, seg, conv_yin):
    j = pl.program_id(1)

    @pl.when(j == 0)
    def _():
        y = yin_ref[...]
        if conv_yin:
            y = _short_conv(y, cwy_ref, cby_ref, seg)
        y32_ref[...] = y
        ybf_ref[...] = y.astype(bf16)
        acc_ref[...] = jnp.zeros_like(acc_ref)

    yb = ybf_ref[...]
    yre = jnp.dot(fre_ref[...], yb, preferred_element_type=f32)
    yim = jnp.dot(fim_ref[...], yb, preferred_element_type=f32)
    fa, fb, fd = fa_ref[...], fb_ref[...], fd_ref[...]
    zre = (yre * fa - yim * fb).astype(bf16)
    zim = (yre * fb + yim * fd).astype(bf16)
    acc_ref[...] += (jnp.dot(ire_ref[...], zre, preferred_element_type=f32)
                     + jnp.dot(iim_ref[...], zim, preferred_element_type=f32))

    @pl.when(j == pl.num_programs(1) - 1)
    def _():
        gate = _short_conv(gate_ref[...], cwg_ref, cbg_ref, seg)
        o_ref[...] = gate * (acc_ref[...] + dsk_ref[...] * y32_ref[...])


def _longconv(yin, yin_cb, gate_arr, gate_cb, cw, cb, part_y, part_g, fre, fim, iim, fa, fb, fd, dsk, order,
              *, nbatch, seq, seg, conv_yin):
    tk = HY_TK
    once = pl.Buffered(1)
    cw_spec = lambda p: pl.BlockSpec((HY_SHORT, GRP), lambda b, j: (0, p))
    cb_spec = lambda p: pl.BlockSpec((1, GRP), lambda b, j: (0, p))
    filt_spec = pl.BlockSpec((tk, GRP), lambda b, j: (j, order))
    return pl.pallas_call(
        functools.partial(_longconv_kernel, seg=seg, conv_yin=conv_yin),
        out_shape=jax.ShapeDtypeStruct((nbatch * seq, GRP), f32),
        grid=(nbatch, seq // tk),
        in_specs=[pl.BlockSpec((seq, GRP), lambda b, j: (b, yin_cb), pipeline_mode=once),
                  pl.BlockSpec((seq, GRP), lambda b, j: (b, gate_cb), pipeline_mode=once),
                  cw_spec(part_y), cb_spec(part_y), cw_spec(part_g), cb_spec(part_g),
                  pl.BlockSpec((tk, seq), lambda b, j: (j, 0)),
                  pl.BlockSpec((tk, seq), lambda b, j: (j, 0)),
                  pl.BlockSpec((seq, tk), lambda b, j: (0, j)),
                  pl.BlockSpec((seq, tk), lambda b, j: (0, j)),
                  filt_spec, filt_spec, filt_spec,
                  pl.BlockSpec((None, 1, GRP), lambda b, j: (order, 0, 0))],
        out_specs=pl.BlockSpec((seq, GRP), lambda b, j: (b, 0), pipeline_mode=once),
        scratch_shapes=[pltpu.VMEM((seq, GRP), f32), pltpu.VMEM((seq, GRP), bf16), pltpu.VMEM((seq, GRP), f32)],
        compiler_params=_cparams(("parallel", "arbitrary")),
        name="hy_longconv",
    )(yin, gate_arr, cw, cb, cw, cb, fre, fim, fre, iim, fa, fb, fd, dsk)


def _out_kernel(m0_ref, m1_ref, m2_ref, m3_ref, x_ref, mod_ref, w_ref, g_ref, b_ref, o_ref, *, alpha):
    acc = _bdot(m0_ref[...], w_ref[0 * GRP:1 * GRP, :])
    acc += _bdot(m1_ref[...], w_ref[1 * GRP:2 * GRP, :])
    acc += _bdot(m2_ref[...], w_ref[2 * GRP:3 * GRP, :])
    acc += _bdot(m3_ref[...], w_ref[3 * GRP:4 * GRP, :])
    r = alpha * x_ref[...] + mod_ref[2:3, :] * acc
    o_ref[...] = _ln(r) * g_ref[...] + b_ref[...]


def _out_proj(mixes, s5_spec, x, mod, layer, row_fn, w_out, g, b, tm, alpha):
    t, d = x.shape
    mspec = pl.BlockSpec((tm, GRP), lambda i: (i, 0))
    return pl.pallas_call(
        functools.partial(_out_kernel, alpha=alpha),
        out_shape=jax.ShapeDtypeStruct((t, d), f32),
        grid=(t // tm,),
        in_specs=[mspec, mspec, s5_spec, mspec,
                  pl.BlockSpec((tm, d), lambda i: (i, 0)),
                  pl.BlockSpec((None, None, 6, d), lambda i: (layer, row_fn(i), 0, 0)),
                  pl.BlockSpec((d, d), lambda i: (0, 0), pipeline_mode=pl.Buffered(1)),
                  pl.BlockSpec((1, d), lambda i: (0, 0)),
                  pl.BlockSpec((1, d), lambda i: (0, 0))],
        out_specs=pl.BlockSpec((tm, d), lambda i: (i, 0)),
        compiler_params=_cparams(("parallel",)),
        name="out_proj",
    )(*mixes, x, mod, w_out, g, b)


def _mlp_kernel(x_ref, mod_ref, wu_ref, wd_ref, g_ref, b_ref, o_ref, h_ref, acc_ref, *, alpha):
    j = pl.program_id(1)

    @pl.when(j == 0)
    def _():
        y = _ln(x_ref[...])
        h_ref[...] = (y * (1.0 + mod_ref[4:5, :]) + mod_ref[3:4, :]).astype(bf16)
        acc_ref[...] = jnp.zeros_like(acc_ref)

    u = jnp.maximum(jnp.dot(h_ref[...], wu_ref[...], preferred_element_type=f32), 0.0)
    acc_ref[...] += jnp.dot((u * u).astype(bf16), wd_ref[...], preferred_element_type=f32)

    @pl.when(j == pl.num_programs(1) - 1)
    def _():
        r = alpha * x_ref[...] + mod_ref[5:6, :] * acc_ref[...]
        o_ref[...] = _ln(r) * g_ref[...] + b_ref[...]


def _mlp(x, mod, layer, row_fn, w_up, w_down, g, b, tm, tf, alpha):
    t, d = x.shape
    dff = w_up.shape[1]
    return pl.pallas_call(
        functools.partial(_mlp_kernel, alpha=alpha),
        out_shape=jax.ShapeDtypeStruct((t, d), f32),
        grid=(t // tm, dff // tf),
        in_specs=[pl.BlockSpec((tm, d), lambda i, j: (i, 0)),
                  pl.BlockSpec((None, None, 6, d), lambda i, j: (layer, row_fn(i), 0, 0)),
                  pl.BlockSpec((d, tf), lambda i, j: (0, j)),
                  pl.BlockSpec((tf, d), lambda i, j: (j, 0)),
                  pl.BlockSpec((1, d), lambda i, j: (0, 0)),
                  pl.BlockSpec((1, d), lambda i, j: (0, 0))],
        out_specs=pl.BlockSpec((tm, d), lambda i, j: (i, 0)),
        scratch_shapes=[pltpu.VMEM((tm, d), bf16), pltpu.VMEM((tm, d), f32)],
        compiler_params=_cparams(("parallel", "arbitrary")),
        name="mlp",
    )(x, mod, w_up, w_down, g, b)


def _perm_w_in(w):
    d = w.shape[0]
    hk, hv = HEADS * DK, HEADS * DV
    o = 0
    parts = {}
    for name, width in (("gq", hk), ("gk", hk), ("gv", hv), ("gg", GRP), ("glr", 2 * GLA_RANK),
                        ("rq", hk), ("rk", hk), ("rv", hv), ("rg", GRP), ("su", GRP), ("hy", 3 * GRP)):
        parts[name] = w[:, o:o + width]
        o += width
    qk = lambda q, k: jnp.concatenate([q.reshape(d, HEADS, DK), k.reshape(d, HEADS, DK)], axis=-1).reshape(d, 2 * hk)
    cols = [qk(parts["gq"], parts["gk"]), parts["gv"], parts["gg"],
            qk(parts["rq"], parts["rk"]), parts["rv"], parts["rg"],
            parts["su"], parts["hy"], parts["glr"]]
    wp = jnp.concatenate(cols, axis=1)
    return jnp.pad(wp, ((0, 0), (0, NP_IN - wp.shape[1]))).astype(bf16)


def _gla_gate_params(w_gate, b_gate):
    r = w_gate.shape[1]
    wg = jnp.transpose(w_gate.reshape(2, r, HEADS, DK), (0, 2, 1, 3))
    full = jnp.zeros((2, HEADS, LANES, DK), f32)
    for d in range(2):
        full = full.at[d, :, d * r:(d + 1) * r, :].set(wg[d])
    return full.astype(bf16), b_gate.reshape(2, HEADS, 1, DK)


def _s5_block_weights(bbre, bbim, c_re, c_im):
    _, g, hch, p = bbre.shape
    ntile = g // S5_GPT
    eye = jnp.eye(S5_GPT, dtype=f32)

    def in_map(b):
        b = b.reshape(2, ntile, S5_GPT, hch, p)
        return jnp.einsum("dtghp,gk->dtghkp", b, eye).reshape(2, ntile, S5_GPT * hch, S5_GPT * p)

    def out_map(c):
        c = c.reshape(2, ntile, S5_GPT, hch, p)
        return jnp.einsum("dtghp,gk->dtgpkh", c, eye).reshape(2, ntile, S5_GPT * p, S5_GPT * hch)

    wb = jnp.concatenate([in_map(bbre), in_map(bbim)], axis=-1).astype(bf16)
    return wb, out_map(c_re).astype(bf16), out_map(c_im).astype(bf16)


@functools.lru_cache(maxsize=None)
def _dft_tables(seq):
    n = 2 * seq
    k = np.arange(seq, dtype=np.int64)[:, None]
    s1 = np.arange(seq // 64, dtype=np.int64)[None, :]
    s2 = np.arange(64, dtype=np.int64)[None, :]
    a = 2.0 * np.pi * ((k * s1 * 64) % n) / n
    b = 2.0 * np.pi * ((k * s2) % n) / n
    return tuple(np.asarray(t, np.float32) for t in (np.cos(a), np.sin(a), np.cos(b), np.sin(b)))


def _dft_mats(seq):
    ca, sa, cb, sb = (jnp.asarray(t) for t in _dft_tables(seq))
    cosm = (ca[:, :, None] * cb[:, None, :] - sa[:, :, None] * sb[:, None, :]).reshape(seq, seq)
    sinm = (sa[:, :, None] * cb[:, None, :] + ca[:, :, None] * sb[:, None, :]).reshape(seq, seq)
    sign = jnp.where((jnp.arange(seq) & 1) == 0, 1.0, -1.0).astype(f32)
    ri = jnp.arange(seq)[:, None]
    ci = jnp.arange(seq)[None, :]
    fim = jnp.where(ri == 0, sign[None, :], -sinm)
    iim = jnp.where(ci == 0, sign[:, None], -sinm)
    return cosm.astype(bf16), fim.astype(bf16), iim.astype(bf16)


@functools.lru_cache(maxsize=None)
def _hy_feats(seq):
    t = np.linspace(0.0, 1.0, seq, dtype=np.float32)[:, None]
    w = (2.0 * math.pi * np.arange(seq, dtype=np.float32)[:, None] / seq).astype(np.float32)
    fr = np.linspace(1e-4, HY_BANDS - 1.0, HY_BANDS, dtype=np.float32)[None, :]
    feats = np.concatenate([t, np.cos(fr * w), -np.sin(fr * w)], axis=-1).astype(np.float32)
    return np.pad(feats, ((0, 0), (0, LANES - feats.shape[1])))


def _layer_group(x, mod, layer, p, state, *, nbatch, seq, rows, tm, row_fn, want_state, alpha):
    proj = _in_proj(x, mod, layer, row_fn, p["w_in"], tm)

    o_gla, s_gla = _linattn(proj, OFF_GLA, (p["gla_wg"], p["gla_bg"], p["gla_nw"]),
                            None if state is None else state["gla"], layer,
                            mode="gla", nbatch=nbatch, seq=seq, want_state=want_state)
    o_ret, s_ret = _linattn(proj, OFF_RET, (p["ret_dexp"],),
                            None if state is None else state["ret"], layer,
                            mode="ret", nbatch=nbatch, seq=seq, want_state=want_state)

    u_t = jnp.transpose(proj[:, OFF_S5:OFF_S5 + GRP].reshape(nbatch, seq, GRP), (1, 0, 2))
    h0 = None if state is None else (state["s5_re"][layer], state["s5_im"][layer])
    y_t, hf_re, hf_im = _s5_scan(u_t, p["s5_wb"], p["s5_lam"], p["s5_cre"], p["s5_cim"], p["s5_d"], h0,
                                 want_state=want_state)
    o_s5 = _s5_glu(y_t.reshape(seq * nbatch, GRP), p["s5_glu_w"], p["s5_glu_b"], 512)
    o_s5 = o_s5.reshape(seq, nbatch * GRP)
    tpb = seq // tm
    s5_spec = pl.BlockSpec((tm, GRP), lambda i: (i % tpb, i // tpb))

    fre, fim, iim = _dft_mats(seq)
    fa, fb, fd = _hy_filter(jnp.asarray(_hy_feats(seq)), p["hy_w1"], p["hy_b1"], p["hy_w2"], p["hy_b2"],
                            p["hy_freq"], p["hy_w3"], p["hy_decay"], fre, fim)
    seg = seq // rows
    hy_cb = OFF_HY // GRP
    y1 = _longconv(proj, hy_cb, proj, hy_cb + 1, p["hy_cw"], p["hy_cb"], 0, 1, fre, fim, iim, fa, fb, fd,
                   p["hy_d"], 0, nbatch=nbatch, seq=seq, seg=seg, conv_yin=True)
    o_hy = _longconv(y1, 0, proj, hy_cb + 2, p["hy_cw"], p["hy_cb"], 0, 2, fre, fim, iim, fa, fb, fd,
                     p["hy_d"], 1, nbatch=nbatch, seq=seq, seg=seg, conv_yin=False)

    x = _out_proj((o_gla, o_ret, o_s5, o_hy), s5_spec, x, mod, layer, row_fn, p["w_out"],
                  p["ln1_g"], p["ln1_b"], tm, alpha)
    x = _mlp(x, mod, layer, row_fn, p["w_up"], p["w_down"], p["ln2_g"], p["ln2_b"], tm, 1024, alpha)
    return x, (s_gla, s_ret, hf_re, hf_im)


def kernel(x_prompt, x_sample, state_gla, state_ret, state_s5_re, state_s5_im, c, c_ctx, ada_w, ada_b, w_in, gla_w_gate, gla_b_gate, gla_norm_w, ret_decay_exp, s5_a_re, s5_a_im, s5_log_step, s5_b_re, s5_b_im, s5_c_re, s5_c_im, s5_d, s5_glu_w, s5_glu_b, hy_conv_w, hy_conv_b, hy_f_w1, hy_f_b1, hy_f_w2, hy_f_b2, hy_f_freq, hy_f_w3, hy_decay, hy_d, w_out, ln1_g, ln1_b, w_up, w_down, ln2_g, ln2_b):
    bp, lp, d = x_prompt.shape
    bs, ls, _ = x_sample.shape
    depth = w_in.shape[0]
    alpha = (2 * depth) ** 0.25
    ngroup = s5_a_re.shape[2]
    ntile = ngroup // S5_GPT

    nrow = -(-(1 + bs) // SUBLANES) * SUBLANES
    c_rows = jnp.concatenate([c_ctx[None, :], c, jnp.zeros((nrow - 1 - bs, d), f32)], axis=0)
    mod = _ada(c_rows, ada_w, ada_b).reshape(depth, nrow, 6, d)

    tr_state = lambda s: jnp.swapaxes(s, -1, -2)
    s5_state = lambda s: jnp.transpose(s.reshape(bs, depth, 2, ntile, S5_SW), (1, 2, 3, 0, 4))
    lat_state = dict(gla=tr_state(state_gla), ret=tr_state(state_ret),
                     s5_re=s5_state(state_s5_re), s5_im=s5_state(state_s5_im))

    tm_ctx = min(lp, 512)
    tm_lat = min(ls, 512)
    tpb_lat = ls // tm_lat
    tpb_ctx = lp // tm_ctx
    yp = x_prompt.reshape(bp * lp, d)
    ys = x_sample.reshape(bs * ls, d)
    outs = []
    for l in range(depth):
        lre, lim, bbre, bbim = _s5_prep(s5_a_re[l], s5_a_im[l], s5_log_step[l], s5_b_re[l], s5_b_im[l])
        wb, cre, cim = _s5_block_weights(bbre, bbim, s5_c_re[l], s5_c_im[l])
        lam = jnp.stack([lre.reshape(2, ntile, S5_SW), lim.reshape(2, ntile, S5_SW)], axis=2)
        wg, bg = _gla_gate_params(gla_w_gate[l], gla_b_gate[l])
        w1p = jnp.pad(hy_f_w1[l], ((0, LANES - hy_f_w1.shape[1]), (0, 0)))
        p = dict(
            w_in=_perm_w_in(w_in[l]),
            gla_wg=wg, gla_bg=bg, gla_nw=gla_norm_w[l][None, :],
            ret_dexp=jnp.broadcast_to(ret_decay_exp[l][:, :, None, None], (2, HEADS, SUBLANES, LANES)),
            s5_wb=wb, s5_lam=lam, s5_cre=cre, s5_cim=cim, s5_d=s5_d[l][None, :],
            s5_glu_w=s5_glu_w[l].astype(bf16), s5_glu_b=s5_glu_b[l][None, :],
            hy_cw=hy_conv_w[l], hy_cb=hy_conv_b[l][None, :],
            hy_w1=w1p, hy_b1=hy_f_b1[l][None, :], hy_w2=hy_f_w2[l], hy_b2=hy_f_b2[l][None, :],
            hy_freq=hy_f_freq[l][None, :], hy_w3=hy_f_w3[l], hy_decay=hy_decay[l][None, :],
            hy_d=hy_d[l][:, None, :],
            w_out=w_out[l].astype(bf16), ln1_g=ln1_g[l][None, :], ln1_b=ln1_b[l][None, :],
            w_up=w_up[l].astype(bf16), w_down=w_down[l].astype(bf16),
            ln2_g=ln2_g[l][None, :], ln2_b=ln2_b[l][None, :],
        )
        yp, st = _layer_group(yp, mod, l, p, None, nbatch=bp, seq=lp, rows=1, tm=tm_ctx,
                              row_fn=lambda i: 0, want_state=True, alpha=alpha)
        outs.append(st)
        ys, _ = _layer_group(ys, mod, l, p, lat_state, nbatch=bs, seq=ls, rows=ls // GRID_W, tm=tm_lat,
                             row_fn=lambda i: 1 + i // tpb_lat, want_state=False, alpha=alpha)

    new_gla = jnp.stack([jnp.swapaxes(o[0], -1, -2) for o in outs], axis=1)
    new_ret = jnp.stack([jnp.swapaxes(o[1], -1, -2) for o in outs], axis=1)
    unpack = lambda h: jnp.transpose(h, (2, 0, 1, 3)).reshape(bp, 2, ngroup, S5_STATE)
    new_re = jnp.stack([unpack(o[2]) for o in outs], axis=1)
    new_im = jnp.stack([unpack(o[3]) for o in outs], axis=1)
    return (yp.reshape(bp, lp, d), ys.reshape(bs, ls, d), new_gla, new_ret, new_re, new_im)
```

```python
import functools
import math

import numpy as np
import jax
import jax.numpy as jnp
from jax import lax
from jax.experimental import pallas as pl
from jax.experimental.pallas import tpu as pltpu

f32 = jnp.float32
bf16 = jnp.bfloat16

GRID_W = 64
HEADS = 4
DK = 64
DV = 128
GLA_RANK = 16
GLA_TAU = 16.0
S5_GROUP = 16
S5_STATE = 64
HY_BANDS = 16
HY_SHORT = 3
CHUNK = 64
LN_EPS = 1e-5
NORM_EPS = 1e-6

LANES = 128
SUBLANES = 8
VMEM_LIMIT = 52 * 1024 * 1024

GRP = HEADS * DV
OFF_GLA = 0
OFF_RET = 3 * GRP
OFF_S5 = 6 * GRP
OFF_HY = 7 * GRP
OFF_LR = 10 * GRP
NP_IN = 5376
TN_IN = 1792
DENSE_TM = 512
MLP_TF = 1024


def _cparams(sem, vmem=VMEM_LIMIT):
    return pltpu.CompilerParams(dimension_semantics=sem, vmem_limit_bytes=vmem)


def _ln(x):
    mu = jnp.mean(x, axis=-1, keepdims=True)
    xc = x - mu
    var = jnp.mean(xc * xc, axis=-1, keepdims=True)
    return xc * lax.rsqrt(var + LN_EPS)


def _silu(x):
    return x * jax.nn.sigmoid(x)


def _bdot(a, b):
    return jnp.dot(a.astype(bf16), b.astype(bf16), preferred_element_type=f32)


def _bdot_nt(a, b):
    return lax.dot_general(a.astype(bf16), b.astype(bf16), (((1,), (1,)), ((), ())), preferred_element_type=f32)


def _bdot_tn(a, b):
    return lax.dot_general(a.astype(bf16), b.astype(bf16), (((0,), (0,)), ((), ())), preferred_element_type=f32)


def _split3(x):
    hi = x.astype(bf16)
    r = x - hi.astype(f32)
    mid = r.astype(bf16)
    lo = (r - mid.astype(f32)).astype(bf16)
    return hi, mid, lo


def _dot_exact_lhs(a_bf, x):
    hi, mid, lo = _split3(x)
    d = lambda b: jnp.dot(a_bf, b, preferred_element_type=f32)
    return d(hi) + (d(mid) + d(lo))


def _dot3(a, b):
    ah = a.astype(bf16)
    al = (a - ah.astype(f32)).astype(bf16)
    bh = b.astype(bf16)
    bl = (b - bh.astype(f32)).astype(bf16)
    d = lambda p, q: jnp.dot(p, q, preferred_element_type=f32)
    return d(ah, bh) + (d(al, bh) + d(ah, bl))


def _ada_kernel(c_ref, w_ref, b_ref, o_ref):
    s = _silu(c_ref[...])
    o_ref[...] = _bdot(s, w_ref[...]) + b_ref[...]


def _ada(c_rows, ada_w, ada_b):
    depth, d, n = ada_w.shape
    r = c_rows.shape[0]
    tn = 1024
    return pl.pallas_call(
        _ada_kernel,
        out_shape=jax.ShapeDtypeStruct((depth, r, n), f32),
        grid=(depth, n // tn),
        in_specs=[pl.BlockSpec((r, d), lambda l, j: (0, 0)),
                  pl.BlockSpec((None, d, tn), lambda l, j: (l, 0, j)),
                  pl.BlockSpec((None, 1, tn), lambda l, j: (l, 0, j))],
        out_specs=pl.BlockSpec((None, r, tn), lambda l, j: (l, 0, j)),
        compiler_params=_cparams(("parallel", "parallel")),
        name="ada",
    )(c_rows, ada_w, ada_b.reshape(depth, 1, n))


def _in_kernel(x_ref, mod_ref, w_ref, o_ref, h_ref):
    @pl.when(pl.program_id(1) == 0)
    def _():
        y = _ln(x_ref[...])
        h_ref[...] = (y * (1.0 + mod_ref[1:2, :]) + mod_ref[0:1, :]).astype(bf16)

    o_ref[...] = jnp.dot(h_ref[...], w_ref[...], preferred_element_type=f32)


def _in_proj(x, mod, layer, row_fn, w_perm, tm):
    t, d = x.shape
    return pl.pallas_call(
        _in_kernel,
        out_shape=jax.ShapeDtypeStruct((t, NP_IN), f32),
        grid=(t // tm, NP_IN // TN_IN),
        in_specs=[pl.BlockSpec((tm, d), lambda i, j: (i, 0)),
                  pl.BlockSpec((None, None, 6, d), lambda i, j: (layer, row_fn(i), 0, 0)),
                  pl.BlockSpec((d, TN_IN), lambda i, j: (0, j))],
        out_specs=pl.BlockSpec((tm, TN_IN), lambda i, j: (i, j)),
        scratch_shapes=[pltpu.VMEM((tm, d), bf16)],
        compiler_params=_cparams(("parallel", "arbitrary")),
        name="in_proj",
    )(x, mod, w_perm)


LA_TAIL = 256


def _linattn_kernel(*refs, mode, seq, has_s0, want_state):
    it = iter(refs)
    qk_ref, v_ref, g_ref = next(it), next(it), next(it)
    if mode == "gla":
        lr_ref, wg_ref, bg_ref, nw_ref = next(it), next(it), next(it), next(it)
    else:
        dexp_ref = next(it)
    s0_ref = next(it) if has_s0 else None
    o_ref = next(it)
    sf_ref = next(it) if want_state else None
    ob_ref, st_ref = next(it), next(it)

    c = CHUNK
    nchunk = seq // c
    scale = DK ** -0.5
    row = lax.broadcasted_iota(jnp.int32, (c, c), 0)
    col = lax.broadcasted_iota(jnp.int32, (c, c), 1)
    is_k = (lax.broadcasted_iota(jnp.int32, (1, GRP), 1) & DK) != 0
    hq = lambda a, h: a[:, h * LANES:h * LANES + DK]
    hk = lambda a, h: a[:, h * LANES + DK:(h + 1) * LANES]
    hv = lambda h: slice(h * DV, (h + 1) * DV)

    if has_s0:
        st_ref[...] = s0_ref[...]
    else:
        st_ref[...] = jnp.zeros_like(st_ref)

    per_dir = []
    for d in (0, 1):
        causal = (row >= col) if d == 0 else (row <= col)
        if mode == "gla":
            tri = jnp.where(causal, 1.0, 0.0).astype(bf16)
            sgn = jnp.where(is_k, -1.0, 1.0)
            qsc = jnp.where(is_k, 1.0, scale)
            per_dir.append((causal, tri, sgn, qsc))
        else:
            lgr = jnp.log1p(-jnp.exp2(-dexp_ref[d]))[0:1, :]
            rowg = lax.broadcasted_iota(jnp.int32, (c, GRP), 0)
            pos = rowg if d == 0 else (c - 1 - rowg)
            pw = jnp.where(is_k, c - 1 - pos, pos + 1).astype(f32)
            wqk = jnp.exp(lgr * pw) * jnp.where(is_k, scale, 1.0)
            dist = ((row - col) if d == 0 else (col - row)).astype(f32)
            masks = [jnp.where(causal, jnp.exp(jnp.broadcast_to(hq(lgr, h), (c, c)) * dist), 0.0) * scale
                     for h in range(HEADS)]
            dec = jnp.exp(lgr * float(c))
            per_dir.append((wqk, masks, dec))

    def body(i, carry):
        rows, qk, qka, qko, ku, dec = [], [], [], [], [], []
        for d in (0, 1):
            n = i if d == 0 else nchunk - 1 - i
            rows.append(pl.ds(pl.multiple_of(n * c, c), c))
            qk.append(qk_ref[rows[d], :])
        if mode == "gla":
            logits = [_bdot(lr_ref[rows[d], :], wg_ref[d]) + bg_ref[d] for d in (0, 1)]
            la = [(jnp.minimum(x, 0.0) - jnp.log1p(jnp.exp(-jnp.abs(x)))) * (1.0 / GLA_TAU) for x in logits]
            b = [_dot_exact_lhs(per_dir[d][1], la[d]) for d in (0, 1)]
            for d in (0, 1):
                _, _, sgn, qsc = per_dir[d]
                last = c - 1 if d == 0 else 0
                btot = b[d][last:last + 1, :]
                qkd = qk[d] * (jnp.exp(b[d] * sgn) * qsc)
                qka.append(qkd)
                qko.append(qkd)
                ku.append(qk[d] * jnp.exp(btot - b[d]))
                dec.append(jnp.exp(btot))
        else:
            for d in (0, 1):
                wqk, _, dcy = per_dir[d]
                qka.append(qk[d])
                qko.append(qk[d] * wqk)
                ku.append(qko[d])
                dec.append(dcy)
        chains = [(d, h) for d in (0, 1) for h in range(HEADS)]
        v = {(d, h): v_ref[rows[d], hv(h)] for d, h in chains}
        st = {(d, h): st_ref[d, h] for d, h in chains}
        att = {(d, h): _bdot_nt(hq(qka[d], h), hk(qka[d], h)) for d, h in chains}
        oi = {(d, h): _bdot_nt(hq(qko[d], h), st[d, h]) for d, h in chains}
        up = {(d, h): _bdot_tn(v[d, h], hk(ku[d], h)) for d, h in chains}
        for d, h in chains:
            if mode == "gla":
                a = jnp.where(per_dir[d][0], att[d, h], 0.0)
            else:
                a = att[d, h] * per_dir[d][1][h]
            o = _bdot(a, v[d, h]) + oi[d, h]
            st_ref[d, h] = hq(dec[d], h) * st[d, h] + up[d, h]
            if d == 0:
                o_ref[rows[d], hv(h)] = o
            else:
                ob_ref[rows[d], hv(h)] = o
        return carry

    lax.fori_loop(0, nchunk, body, 0, unroll=2)
    if want_state:
        sf_ref[...] = st_ref[...]

    tl = min(LA_TAIL, seq)

    def tail(i, carry):
        rows = pl.ds(pl.multiple_of(i * tl, tl), tl)
        for h in range(HEADS):
            o = o_ref[rows, hv(h)] + ob_ref[rows, hv(h)]
            if mode == "gla":
                o = o * lax.rsqrt(jnp.mean(o * o, axis=-1, keepdims=True) + NORM_EPS) * nw_ref[:, hv(h)]
            else:
                o = _ln(o)
            o_ref[rows, hv(h)] = o * _silu(g_ref[rows, hv(h)])
        return carry

    lax.fori_loop(0, seq // tl, tail, 0)


def _linattn(proj, col0, extra, s0t, layer, *, mode, nbatch, seq, want_state):
    cb = col0 // GRP
    full = lambda a: pl.BlockSpec(a.shape, lambda b: (0,) * a.ndim)
    in_specs = [pl.BlockSpec((seq, GRP), lambda b: (b, cb)),
                pl.BlockSpec((seq, GRP), lambda b: (b, cb + 1)),
                pl.BlockSpec((seq, GRP), lambda b: (b, cb + 2))]
    args = [proj, proj, proj]
    if mode == "gla":
        in_specs += [pl.BlockSpec((seq, LANES), lambda b: (b, OFF_LR // LANES))] + [full(a) for a in extra]
        args += [proj, *extra]
    else:
        in_specs += [full(a) for a in extra]
        args += list(extra)
    has_s0 = s0t is not None
    if has_s0:
        in_specs += [pl.BlockSpec((None, None, 2, HEADS, DV, DK), lambda b: (b, layer, 0, 0, 0, 0))]
        args += [s0t]
    out_shape = [jax.ShapeDtypeStruct((nbatch * seq, GRP), f32)]
    out_specs = [pl.BlockSpec((seq, GRP), lambda b: (b, 0))]
    if want_state:
        out_shape += [jax.ShapeDtypeStruct((nbatch, 2, HEADS, DV, DK), f32)]
        out_specs += [pl.BlockSpec((None, 2, HEADS, DV, DK), lambda b: (b, 0, 0, 0, 0))]
    res = pl.pallas_call(
        functools.partial(_linattn_kernel, mode=mode, seq=seq, has_s0=has_s0, want_state=want_state),
        out_shape=out_shape,
        grid=(nbatch,),
        in_specs=in_specs,
        out_specs=out_specs,
        scratch_shapes=[pltpu.VMEM((seq, GRP), f32), pltpu.VMEM((2, HEADS, DV, DK), f32)],
        compiler_params=_cparams(("parallel",)),
        name="linattn_" + mode,
    )(*args)
    return (res[0], res[1]) if want_state else (res[0], None)


def _s5_prep_kernel(are_ref, aim_ref, lstep_ref, bre_ref, bim_ref, lre_ref, lim_ref, bbre_ref, bbim_ref):
    ar, ai = are_ref[...], aim_ref[...]
    st = jnp.exp(lstep_ref[...])
    mag = jnp.exp(ar * st)
    lr = mag * jnp.cos(ai * st)
    li = mag * jnp.sin(ai * st)
    nr, ni = lr - 1.0, li
    den = ar * ar + ai * ai
    kr = (nr * ar + ni * ai) / den
    ki = (ni * ar - nr * ai) / den
    br, bi = bre_ref[...], bim_ref[...]
    lre_ref[...] = lr
    lim_ref[...] = li
    bbre_ref[...] = kr * br - ki * bi
    bbim_ref[...] = kr * bi + ki * br


def _s5_prep(a_re, a_im, log_step, b_re, b_im):
    _, g, p = a_re.shape
    hch = b_re.shape[-1]
    rows = 2 * g * hch
    bc = lambda a: jnp.broadcast_to(a[:, :, None, :], (2, g, hch, p)).reshape(rows, p)
    tr = lambda b: jnp.transpose(b, (0, 1, 3, 2)).reshape(rows, p)
    ls = jnp.broadcast_to(log_step[:, :, None, None], (2, g, hch, p)).reshape(rows, p)
    spec = pl.BlockSpec((rows, p), lambda: (0, 0))
    lre, lim, bbre, bbim = pl.pallas_call(
        _s5_prep_kernel,
        out_shape=[jax.ShapeDtypeStruct((rows, p), f32)] * 4,
        in_specs=[spec] * 5,
        out_specs=[spec] * 4,
        name="s5_prep",
    )(bc(a_re), bc(a_im), ls, tr(b_re), tr(b_im))
    r4 = lambda a: a.reshape(2, g, hch, p)
    return r4(lre)[:, :, 0, :], r4(lim)[:, :, 0, :], r4(bbre), r4(bbim)


S5_TC = 64
S5_GPT = LANES // S5_GROUP
S5_SW = S5_GPT * S5_STATE


def _s5_scan_kernel(*refs, seq, nb, has_h0, want_state):
    it = iter(refs)
    u_ref, wb_ref, lam_ref, cre_ref, cim_ref, dsk_ref = (next(it) for _ in range(6))
    h0re_ref = next(it) if has_h0 else None
    h0im_ref = next(it) if has_h0 else None
    y_ref = next(it)
    hfre_ref = next(it) if want_state else None
    hfim_ref = next(it) if want_state else None
    bu_ref, hs_ref = next(it), next(it)

    tc = S5_TC
    nchunk = seq // tc
    sw = S5_SW
    for d in (0, 1):
        lam_r = jnp.broadcast_to(lam_ref[d, 0:1, :], (nb, sw))
        lam_i = jnp.broadcast_to(lam_ref[d, 1:2, :], (nb, sw))

        def chunk(i, carry, d=d, lam_r=lam_r, lam_i=lam_i):
            hre, him = carry
            n = i if d == 0 else nchunk - 1 - i
            t0 = pl.multiple_of(n * tc, tc)
            ub = u_ref[pl.ds(t0, tc)]
            bu_ref[...] = _bdot(ub.reshape(tc * nb, LANES), wb_ref[d])
            order = range(tc) if d == 0 else range(tc - 1, -1, -1)
            for t in order:
                rs = slice(t * nb, (t + 1) * nb)
                nre = lam_r * hre - lam_i * him + bu_ref[rs, 0:sw]
                nim = lam_r * him + lam_i * hre + bu_ref[rs, sw:2 * sw]
                hre, him = nre, nim
                hs_ref[rs, 0:sw] = hre
                hs_ref[rs, sw:2 * sw] = him
            y = _bdot(hs_ref[:, 0:sw], cre_ref[d]) - _bdot(hs_ref[:, sw:2 * sw], cim_ref[d])
            y = y.reshape(tc, nb, LANES)
            if d == 0:
                y_ref[pl.ds(t0, tc)] = y + ub * dsk_ref[...]
            else:
                y_ref[pl.ds(t0, tc)] = y_ref[pl.ds(t0, tc)] + y
            return hre, him

        if has_h0:
            h0 = (h0re_ref[d], h0im_ref[d])
        else:
            h0 = (jnp.zeros((nb, sw), f32), jnp.zeros((nb, sw), f32))
        hre, him = lax.fori_loop(0, nchunk, chunk, h0)
        if want_state:
            hfre_ref[d] = hre
            hfim_ref[d] = him


def _s5_scan(u_t, wb, lam, cre, cim, dsk, h0, *, want_state):
    seq, nb, ch = u_t.shape
    ntile = ch // LANES
    sw = S5_SW
    in_specs = [pl.BlockSpec((seq, nb, LANES), lambda j: (0, 0, j)),
                pl.BlockSpec((2, None, LANES, 2 * sw), lambda j: (0, j, 0, 0)),
                pl.BlockSpec((2, None, 2, sw), lambda j: (0, j, 0, 0)),
                pl.BlockSpec((2, None, sw, LANES), lambda j: (0, j, 0, 0)),
                pl.BlockSpec((2, None, sw, LANES), lambda j: (0, j, 0, 0)),
                pl.BlockSpec((1, LANES), lambda j: (0, j))]
    args = [u_t, wb, lam, cre, cim, dsk]
    has_h0 = h0 is not None
    st_spec = pl.BlockSpec((2, None, nb, sw), lambda j: (0, j, 0, 0))
    if has_h0:
        in_specs += [st_spec, st_spec]
        args += list(h0)
    out_shape = [jax.ShapeDtypeStruct((seq, nb, ch), f32)]
    out_specs = [pl.BlockSpec((seq, nb, LANES), lambda j: (0, 0, j))]
    if want_state:
        out_shape += [jax.ShapeDtypeStruct((2, ntile, nb, sw), f32)] * 2
        out_specs += [st_spec, st_spec]
    res = pl.pallas_call(
        functools.partial(_s5_scan_kernel, seq=seq, nb=nb, has_h0=has_h0, want_state=want_state),
        out_shape=out_shape,
        grid=(ntile,),
        in_specs=in_specs,
        out_specs=out_specs,
        scratch_shapes=[pltpu.VMEM((S5_TC * nb, 2 * sw), f32), pltpu.VMEM((S5_TC * nb, 2 * sw), f32)],
        compiler_params=_cparams(("parallel",)),
        name="s5_scan",
    )(*args)
    return (res[0], res[1], res[2]) if want_state else (res[0], None, None)


def _s5_glu_kernel(y_ref, w_ref, b_ref, o_ref):
    y = y_ref[...]
    z = y * (0.5 * (1.0 + jnp.tanh(math.sqrt(2.0 / math.pi) * (y + 0.044715 * (y * y * y)))))
    o_ref[...] = z * jax.nn.sigmoid(_bdot(z, w_ref[...]) + b_ref[...])


def _s5_glu(y, w, b, tm):
    t, ch = y.shape
    return pl.pallas_call(
        _s5_glu_kernel,
        out_shape=jax.ShapeDtypeStruct((t, ch), f32),
        grid=(t // tm,),
        in_specs=[pl.BlockSpec((tm, ch), lambda i: (i, 0)),
                  pl.BlockSpec((ch, ch), lambda i: (0, 0)),
                  pl.BlockSpec((1, ch), lambda i: (0, 0))],
        out_specs=pl.BlockSpec((tm, ch), lambda i: (i, 0)),
        compiler_params=_cparams(("parallel",)),
        name="s5_glu",
    )(y, w, b)


HY_TK = 256
HY_CT = 256


def _hy_filter_kernel(feat_ref, w1_ref, b1_ref, w2_ref, b2_ref, fq_ref, w3f_ref, w3b_ref, dcf_ref, dcb_ref,
                      fre_ref, fim_ref, fa_ref, fb_ref, fd_ref, fs_ref, fdif_ref, nyq_ref, *, seq):
    j = pl.program_id(1)

    @pl.when(j == 0)
    def _():
        feat = feat_ref[...]
        fq = fq_ref[...]
        h = jnp.sin(fq * (_dot3(feat, w1_ref[...]) + b1_ref[...]))
        h = jnp.sin(fq * (_dot3(h, w2_ref[...]) + b2_ref[...]))
        t = feat[:, 0:1]
        rowi = lax.broadcasted_iota(jnp.int32, (seq, HY_CT), 0)

        def filt(w3_ref, dc_ref):
            f = _dot3(h, w3_ref[...]) * jnp.exp(-t * jnp.abs(dc_ref[...]))
            return f / jnp.sum(jnp.abs(f), axis=0, keepdims=True)

        ff = filt(w3f_ref, dcf_ref)
        fb = jnp.where(rowi == 0, 0.0, filt(w3b_ref, dcb_ref))
        fsum = ff + fb
        fs_ref[...] = fsum.astype(bf16)
        fdif_ref[...] = (ff - fb).astype(bf16)
        sign = jnp.where((rowi & 1) == 0, 1.0, -1.0)
        nyq_ref[...] = jnp.sum(fsum * sign, axis=0, keepdims=True)

    scl = 1.0 / seq
    fa = jnp.dot(fre_ref[...], fs_ref[...], preferred_element_type=f32) * scl
    fb = jnp.dot(fim_ref[...], fdif_ref[...], preferred_element_type=f32) * scl
    first = jnp.logical_and(lax.broadcasted_iota(jnp.int32, (HY_TK, HY_CT), 0) == 0, j == 0)
    fa_ref[...] = jnp.where(first, 0.5 * fa, fa)
    fb_ref[...] = jnp.where(first, 0.0, fb)
    fd_ref[...] = jnp.where(first, (0.5 * scl) * nyq_ref[...], fa)


def _hy_filter(feat, w1p, b1, w2, b2, fq, w3, dc, fre, fim):
    seq = feat.shape[0]
    ncol = w3.shape[1] // 2
    nct = ncol // HY_CT
    tk = min(HY_TK, seq)
    assert tk == HY_TK
    full = lambda a: pl.BlockSpec(a.shape, lambda c, j: (0,) * a.ndim)
    out_spec = pl.BlockSpec((tk, HY_CT), lambda c, j: (j, c))
    return pl.pallas_call(
        functools.partial(_hy_filter_kernel, seq=seq),
        out_shape=[jax.ShapeDtypeStruct((seq, ncol), f32)] * 3,
        grid=(nct, seq // tk),
        in_specs=[full(feat), full(w1p), full(b1), full(w2), full(b2), full(fq),
                  pl.BlockSpec((w3.shape[0], HY_CT), lambda c, j: (0, c)),
                  pl.BlockSpec((w3.shape[0], HY_CT), lambda c, j: (0, nct + c)),
                  pl.BlockSpec((1, HY_CT), lambda c, j: (0, c)),
                  pl.BlockSpec((1, HY_CT), lambda c, j: (0, nct + c)),
                  pl.BlockSpec((tk, seq), lambda c, j: (j, 0)),
                  pl.BlockSpec((tk, seq), lambda c, j: (j, 0))],
        out_specs=[out_spec] * 3,
        scratch_shapes=[pltpu.VMEM((seq, HY_CT), bf16), pltpu.VMEM((seq, HY_CT), bf16), pltpu.VMEM((1, HY_CT), f32)],
        compiler_params=_cparams(("parallel", "arbitrary")),
        name="hy_filter",
    )(feat, w1p, b1, w2, b2, fq, w3, w3, dc, dc, fre, fim)


def _short_conv(x, w_ref, b_ref, seg):
    rows = x.shape[0]
    r = lax.broadcasted_iota(jnp.int32, x.shape, 0) & (seg - 1)
    xp = jnp.where(r == 0, 0.0, pltpu.roll(x, 1, 0))
    xn = jnp.where(r == seg - 1, 0.0, pltpu.roll(x, rows - 1, 0))
    return xp * w_ref[0:1, :] + x * w_ref[1:2, :] + xn * w_ref[2:3, :] + b_ref[...]


def _longconv_kernel(yin_ref, gate_ref, cwy_ref, cby_ref, cwg_ref, cbg_ref, fre_ref, fim_ref, ire_ref, iim_ref,
                     fa_ref, fb_ref, fd_ref, dsk_ref, o_ref, y32_ref, ybf_ref, acc_ref, *, seg, conv_yin):
    j = pl.program_id(1)

    @pl.when(j == 0)
    def _():
        y = yin_ref[...]
        if conv_yin:
            y = _short_conv(y, cwy_ref, cby_ref, seg)
        y32_ref[...] = y
        ybf_ref[...] = y.astype(bf16)
        acc_ref[...] = jnp.zeros_like(acc_ref)

    yb = ybf_ref[...]
    yre = jnp.dot(fre_ref[...], yb, preferred_element_type=f32)
    yim = jnp.dot(fim_ref[...], yb, preferred_element_type=f32)
    fa, fb, fd = fa_ref[...], fb_ref[...], fd_ref[...]
    zre = (yre * fa - yim * fb).astype(bf16)
    zim = (yre * fb + yim * fd).astype(bf16)
    acc_ref[...] += (jnp.dot(ire_ref[...], zre, preferred_element_type=f32)
                     + jnp.dot(iim_ref[...], zim, preferred_element_type=f32))

    @pl.when(j == pl.num_programs(1) - 1)
    def _():
        gate = _short_conv(gate_ref[...], cwg_ref, cbg_ref, seg)
        o_ref[...] = gate * (acc_ref[...] + dsk_ref[...] * y32_ref[...])


def _longconv(yin, yin_cb, gate_arr, gate_cb, cw, cb, part_y, part_g, fre, fim, iim, fa, fb, fd, dsk, order,
              *, nbatch, seq, seg, conv_yin):
    tk = HY_TK
    once = pl.Buffered(1)
    cw_spec = lambda p: pl.BlockSpec((HY_SHORT, GRP), lambda b, j: (0, p))
    cb_spec = lambda p: pl.BlockSpec((1, GRP), lambda b, j: (0, p))
    filt_spec = pl.BlockSpec((tk, GRP), lambda b, j: (j, order))
    return pl.pallas_call(
        functools.partial(_longconv_kernel, seg=seg, conv_yin=conv_yin),
        out_shape=jax.ShapeDtypeStruct((nbatch * seq, GRP), f32),
        grid=(nbatch, seq // tk),
        in_specs=[pl.BlockSpec((seq, GRP), lambda b, j: (b, yin_cb), pipeline_mode=once),
                  pl.BlockSpec((seq, GRP), lambda b, j: (b, gate_cb), pipeline_mode=once),
                  cw_spec(part_y), cb_spec(part_y), cw_spec(part_g), cb_spec(part_g),
                  pl.BlockSpec((tk, seq), lambda b, j: (j, 0)),
                  pl.BlockSpec((tk, seq), lambda b, j: (j, 0)),
                  pl.BlockSpec((seq, tk), lambda b, j: (0, j)),
                  pl.BlockSpec((seq, tk), lambda b, j: (0, j)),
                  filt_spec, filt_spec, filt_spec,
                  pl.BlockSpec((None, 1, GRP), lambda b, j: (order, 0, 0))],
        out_specs=pl.BlockSpec((seq, GRP), lambda b, j: (b, 0), pipeline_mode=once),
        scratch_shapes=[pltpu.VMEM((seq, GRP), f32), pltpu.VMEM((seq, GRP), bf16), pltpu.VMEM((seq, GRP), f32)],
        compiler_params=_cparams(("parallel", "arbitrary")),
        name="hy_longconv",
    )(yin, gate_arr, cw, cb, cw, cb, fre, fim, fre, iim, fa, fb, fd, dsk)


def _out_kernel(m0_ref, m1_ref, m2_ref, m3_ref, x_ref, mod_ref, w_ref, g_ref, b_ref, o_ref, *, alpha):
    acc = _bdot(m0_ref[...], w_ref[0 * GRP:1 * GRP, :])
    acc += _bdot(m1_ref[...], w_ref[1 * GRP:2 * GRP, :])
    acc += _bdot(m2_ref[...], w_ref[2 * GRP:3 * GRP, :])
    acc += _bdot(m3_ref[...], w_ref[3 * GRP:4 * GRP, :])
    r = alpha * x_ref[...] + mod_ref[2:3, :] * acc
    o_ref[...] = _ln(r) * g_ref[...] + b_ref[...]


def _out_proj(mixes, s5_spec, x, mod, layer, row_fn, w_out, g, b, tm, alpha):
    t, d = x.shape
    mspec = pl.BlockSpec((tm, GRP), lambda i: (i, 0))
    return pl.pallas_call(
        functools.partial(_out_kernel, alpha=alpha),
        out_shape=jax.ShapeDtypeStruct((t, d), f32),
        grid=(t // tm,),
        in_specs=[mspec, mspec, s5_spec, mspec,
                  pl.BlockSpec((tm, d), lambda i: (i, 0)),
                  pl.BlockSpec((None, None, 6, d), lambda i: (layer, row_fn(i), 0, 0)),
                  pl.BlockSpec((d, d), lambda i: (0, 0), pipeline_mode=pl.Buffered(1)),
                  pl.BlockSpec((1, d), lambda i: (0, 0)),
                  pl.BlockSpec((1, d), lambda i: (0, 0))],
        out_specs=pl.BlockSpec((tm, d), lambda i: (i, 0)),
        compiler_params=_cparams(("parallel",)),
        name="out_proj",
    )(*mixes, x, mod, w_out, g, b)


def _mlp_kernel(x_ref, mod_ref, wu_ref, wd_ref, g_ref, b_ref, o_ref, h_ref, acc_ref, *, alpha):
    j = pl.program_id(1)

    @pl.when(j == 0)
    def _():
        y = _ln(x_ref[...])
        h_ref[...] = (y * (1.0 + mod_ref[4:5, :]) + mod_ref[3:4, :]).astype(bf16)
        acc_ref[...] = jnp.zeros_like(acc_ref)

    u = jnp.maximum(jnp.dot(h_ref[...], wu_ref[...], preferred_element_type=f32), 0.0)
    acc_ref[...] += jnp.dot((u * u).astype(bf16), wd_ref[...], preferred_element_type=f32)

    @pl.when(j == pl.num_programs(1) - 1)
    def _():
        r = alpha * x_ref[...] + mod_ref[5:6, :] * acc_ref[...]
        o_ref[...] = _ln(r) * g_ref[...] + b_ref[...]


def _mlp(x, mod, layer, row_fn, w_up, w_down, g, b, tm, tf, alpha):
    t, d = x.shape
    dff = w_up.shape[1]
    return pl.pallas_call(
        functools.partial(_mlp_kernel, alpha=alpha),
        out_shape=jax.ShapeDtypeStruct((t, d), f32),
        grid=(t // tm, dff // tf),
        in_specs=[pl.BlockSpec((tm, d), lambda i, j: (i, 0)),
                  pl.BlockSpec((None, None, 6, d), lambda i, j: (layer, row_fn(i), 0, 0)),
                  pl.BlockSpec((d, tf), lambda i, j: (0, j)),
                  pl.BlockSpec((tf, d), lambda i, j: (j, 0)),
                  pl.BlockSpec((1, d), lambda i, j: (0, 0)),
                  pl.BlockSpec((1, d), lambda i, j: (0, 0))],
        out_specs=pl.BlockSpec((tm, d), lambda i, j: (i, 0)),
        scratch_shapes=[pltpu.VMEM((tm, d), bf16), pltpu.VMEM((tm, d), f32)],
        compiler_params=_cparams(("parallel", "arbitrary")),
        name="mlp",
    )(x, mod, w_up, w_down, g, b)


def _perm_w_in(w):
    d = w.shape[0]
    hk, hv = HEADS * DK, HEADS * DV
    o = 0
    parts = {}
    for name, width in (("gq", hk), ("gk", hk), ("gv", hv), ("gg", GRP), ("glr", 2 * GLA_RANK),
                        ("rq", hk), ("rk", hk), ("rv", hv), ("rg", GRP), ("su", GRP), ("hy", 3 * GRP)):
        parts[name] = w[:, o:o + width]
        o += width
    qk = lambda q, k: jnp.concatenate([q.reshape(d, HEADS, DK), k.reshape(d, HEADS, DK)], axis=-1).reshape(d, 2 * hk)
    cols = [qk(parts["gq"], parts["gk"]), parts["gv"], parts["gg"],
            qk(parts["rq"], parts["rk"]), parts["rv"], parts["rg"],
            parts["su"], parts["hy"], parts["glr"]]
    wp = jnp.concatenate(cols, axis=1)
    return jnp.pad(wp, ((0, 0), (0, NP_IN - wp.shape[1]))).astype(bf16)


def _gla_gate_params(w_gate, b_gate, norm_w):
    r = w_gate.shape[1]
    dup = lambda a: jnp.concatenate([a, a], axis=-1).reshape(*a.shape[:-2], 2 * HEADS * DK)
    wg = dup(w_gate.reshape(2, r, HEADS, DK))
    full = jnp.zeros((2, LANES, 2 * HEADS * DK), f32)
    for d in range(2):
        full = full.at[d, d * r:(d + 1) * r, :].set(wg[d])
    return full.astype(bf16), dup(b_gate.reshape(2, 1, HEADS, DK)), jnp.tile(norm_w[None, :], (1, HEADS))


def _s5_block_weights(bbre, bbim, c_re, c_im):
    _, g, hch, p = bbre.shape
    ntile = g // S5_GPT
    eye = jnp.eye(S5_GPT, dtype=f32)

    def in_map(b):
        b = b.reshape(2, ntile, S5_GPT, hch, p)
        return jnp.einsum("dtghp,gk->dtghkp", b, eye).reshape(2, ntile, S5_GPT * hch, S5_GPT * p)

    def out_map(c):
        c = c.reshape(2, ntile, S5_GPT, hch, p)
        return jnp.einsum("dtghp,gk->dtgpkh", c, eye).reshape(2, ntile, S5_GPT * p, S5_GPT * hch)

    wb = jnp.concatenate([in_map(bbre), in_map(bbim)], axis=-1).astype(bf16)
    return wb, out_map(c_re).astype(bf16), out_map(c_im).astype(bf16)


@functools.lru_cache(maxsize=None)
def _dft_tables(seq):
    n = 2 * seq
    k = np.arange(seq, dtype=np.int64)[:, None]
    s1 = np.arange(seq // 64, dtype=np.int64)[None, :]
    s2 = np.arange(64, dtype=np.int64)[None, :]
    a = 2.0 * np.pi * ((k * s1 * 64) % n) / n
    b = 2.0 * np.pi * ((k * s2) % n) / n
    return tuple(np.asarray(t, np.float32) for t in (np.cos(a), np.sin(a), np.cos(b), np.sin(b)))


def _dft_mats(seq):
    ca, sa, cb, sb = (jnp.asarray(t) for t in _dft_tables(seq))
    cosm = (ca[:, :, None] * cb[:, None, :] - sa[:, :, None] * sb[:, None, :]).reshape(seq, seq)
    sinm = (sa[:, :, None] * cb[:, None, :] + ca[:, :, None] * sb[:, None, :]).reshape(seq, seq)
    sign = jnp.where((jnp.arange(seq) & 1) == 0, 1.0, -1.0).astype(f32)
    ri = jnp.arange(seq)[:, None]
    ci = jnp.arange(seq)[None, :]
    fim = jnp.where(ri == 0, sign[None, :], -sinm)
    iim = jnp.where(ci == 0, sign[:, None], -sinm)
    return cosm.astype(bf16), fim.astype(bf16), iim.astype(bf16)


@functools.lru_cache(maxsize=None)
def _hy_feats(seq):
    t = np.linspace(0.0, 1.0, seq, dtype=np.float32)[:, None]
    w = (2.0 * math.pi * np.arange(seq, dtype=np.float32)[:, None] / seq).astype(np.float32)
    fr = np.linspace(1e-4, HY_BANDS - 1.0, HY_BANDS, dtype=np.float32)[None, :]
    feats = np.concatenate([t, np.cos(fr * w), -np.sin(fr * w)], axis=-1).astype(np.float32)
    return np.pad(feats, ((0, 0), (0, LANES - feats.shape[1])))


def _layer_group(x, mod, layer, p, state, *, nbatch, seq, rows, tm, row_of, want_state, alpha):
    tm_dense = min(DENSE_TM, nbatch * seq)
    proj = _in_proj(x, mod, layer, row_of(tm_dense), p["w_in"], tm_dense)

    o_gla, s_gla = _linattn(proj, OFF_GLA, (p["gla_wg"], p["gla_bg"], p["gla_nw"]),
                            None if state is None else state["gla"], layer,
                            mode="gla", nbatch=nbatch, seq=seq, want_state=want_state)
    o_ret, s_ret = _linattn(proj, OFF_RET, (p["ret_dexp"],),
                            None if state is None else state["ret"], layer,
                            mode="ret", nbatch=nbatch, seq=seq, want_state=want_state)

    u_t = jnp.transpose(proj[:, OFF_S5:OFF_S5 + GRP].reshape(nbatch, seq, GRP), (1, 0, 2))
    h0 = None if state is None else (state["s5_re"][layer], state["s5_im"][layer])
    y_t, hf_re, hf_im = _s5_scan(u_t, p["s5_wb"], p["s5_lam"], p["s5_cre"], p["s5_cim"], p["s5_d"], h0,
                                 want_state=want_state)
    o_s5 = _s5_glu(y_t.reshape(seq * nbatch, GRP), p["s5_glu_w"], p["s5_glu_b"], 512)
    o_s5 = o_s5.reshape(seq, nbatch * GRP)
    tpb = seq // tm
    s5_spec = pl.BlockSpec((tm, GRP), lambda i: (i % tpb, i // tpb))

    fre, fim, iim = _dft_mats(seq)
    fa, fb, fd = _hy_filter(jnp.asarray(_hy_feats(seq)), p["hy_w1"], p["hy_b1"], p["hy_w2"], p["hy_b2"],
                            p["hy_freq"], p["hy_w3"], p["hy_decay"], fre, fim)
    seg = seq // rows
    hy_cb = OFF_HY // GRP
    y1 = _longconv(proj, hy_cb, proj, hy_cb + 1, p["hy_cw"], p["hy_cb"], 0, 1, fre, fim, iim, fa, fb, fd,
                   p["hy_d"], 0, nbatch=nbatch, seq=seq, seg=seg, conv_yin=True)
    o_hy = _longconv(y1, 0, proj, hy_cb + 2, p["hy_cw"], p["hy_cb"], 0, 2, fre, fim, iim, fa, fb, fd,
                     p["hy_d"], 1, nbatch=nbatch, seq=seq, seg=seg, conv_yin=False)

    x = _out_proj((o_gla, o_ret, o_s5, o_hy), s5_spec, x, mod, layer, row_of(tm), p["w_out"],
                  p["ln1_g"], p["ln1_b"], tm, alpha)
    x = _mlp(x, mod, layer, row_of(tm_dense), p["w_up"], p["w_down"], p["ln2_g"], p["ln2_b"], tm_dense, MLP_TF, alpha)
    return x, (s_gla, s_ret, hf_re, hf_im)


def kernel(x_prompt, x_sample, state_gla, state_ret, state_s5_re, state_s5_im, c, c_ctx, ada_w, ada_b, w_in, gla_w_gate, gla_b_gate, gla_norm_w, ret_decay_exp, s5_a_re, s5_a_im, s5_log_step, s5_b_re, s5_b_im, s5_c_re, s5_c_im, s5_d, s5_glu_w, s5_glu_b, hy_conv_w, hy_conv_b, hy_f_w1, hy_f_b1, hy_f_w2, hy_f_b2, hy_f_freq, hy_f_w3, hy_decay, hy_d, w_out, ln1_g, ln1_b, w_up, w_down, ln2_g, ln2_b):
    bp, lp, d = x_prompt.shape
    bs, ls, _ = x_sample.shape
    depth = w_in.shape[0]
    alpha = (2 * depth) ** 0.25
    ngroup = s5_a_re.shape[2]
    ntile = ngroup // S5_GPT

    nrow = -(-(1 + bs) // SUBLANES) * SUBLANES
    c_rows = jnp.concatenate([c_ctx[None, :], c, jnp.zeros((nrow - 1 - bs, d), f32)], axis=0)
    mod = _ada(c_rows, ada_w, ada_b).reshape(depth, nrow, 6, d)

    tr_state = lambda s: jnp.swapaxes(s, -1, -2)
    s5_state = lambda s: jnp.transpose(s.reshape(bs, depth, 2, ntile, S5_SW), (1, 2, 3, 0, 4))
    lat_state = dict(gla=tr_state(state_gla), ret=tr_state(state_ret),
                     s5_re=s5_state(state_s5_re), s5_im=s5_state(state_s5_im))

    tm_ctx = min(lp, DENSE_TM)
    tm_lat = min(ls, DENSE_TM)
    yp = x_prompt.reshape(bp * lp, d)
    ys = x_sample.reshape(bs * ls, d)
    outs = []
    for l in range(depth):
        lre, lim, bbre, bbim = _s5_prep(s5_a_re[l], s5_a_im[l], s5_log_step[l], s5_b_re[l], s5_b_im[l])
        wb, cre, cim = _s5_block_weights(bbre, bbim, s5_c_re[l], s5_c_im[l])
        lam = jnp.stack([lre.reshape(2, ntile, S5_SW), lim.reshape(2, ntile, S5_SW)], axis=2)
        wg, bg, nw = _gla_gate_params(gla_w_gate[l], gla_b_gate[l], gla_norm_w[l])
        w1p = jnp.pad(hy_f_w1[l], ((0, LANES - hy_f_w1.shape[1]), (0, 0)))
        p = dict(
            w_in=_perm_w_in(w_in[l]),
            gla_wg=wg, gla_bg=bg, gla_nw=nw,
            ret_dexp=jnp.broadcast_to(ret_decay_exp[l][:, None, :, None], (2, SUBLANES, HEADS, LANES)).reshape(2, SUBLANES, GRP),
            s5_wb=wb, s5_lam=lam, s5_cre=cre, s5_cim=cim, s5_d=s5_d[l][None, :],
            s5_glu_w=s5_glu_w[l].astype(bf16), s5_glu_b=s5_glu_b[l][None, :],
            hy_cw=hy_conv_w[l], hy_cb=hy_conv_b[l][None, :],
            hy_w1=w1p, hy_b1=hy_f_b1[l][None, :], hy_w2=hy_f_w2[l], hy_b2=hy_f_b2[l][None, :],
            hy_freq=hy_f_freq[l][None, :], hy_w3=hy_f_w3[l], hy_decay=hy_decay[l][None, :],
            hy_d=hy_d[l][:, None, :],
            w_out=w_out[l].astype(bf16), ln1_g=ln1_g[l][None, :], ln1_b=ln1_b[l][None, :],
            w_up=w_up[l].astype(bf16), w_down=w_down[l].astype(bf16),
            ln2_g=ln2_g[l][None, :], ln2_b=ln2_b[l][None, :],
        )
        yp, st = _layer_group(yp, mod, l, p, None, nbatch=bp, seq=lp, rows=1, tm=tm_ctx,
                              row_of=lambda tile: (lambda i: 0), want_state=True, alpha=alpha)
        outs.append(st)
        ys, _ = _layer_group(ys, mod, l, p, lat_state, nbatch=bs, seq=ls, rows=ls // GRID_W, tm=tm_lat,
                             row_of=lambda tile: (lambda i: 1 + i // (ls // tile)), want_state=False, alpha=alpha)

    new_gla = jnp.stack([jnp.swapaxes(o[0], -1, -2) for o in outs], axis=1)
    new_ret = jnp.stack([jnp.swapaxes(o[1], -1, -2) for o in outs], axis=1)
    unpack = lambda h: jnp.transpose(h, (2, 0, 1, 3)).reshape(bp, 2, ngroup, S5_STATE)
    new_re = jnp.stack([unpack(o[2]) for o in outs], axis=1)
    new_im = jnp.stack([unpack(o[3]) for o in outs], axis=1)
    return (yp.reshape(bp, lp, d), ys.reshape(bs, ls, d), new_gla, new_ret, new_re, new_im)
```

```python
import functools
import math

import numpy as np
import jax
import jax.numpy as jnp
from jax import lax
from jax.experimental import pallas as pl
from jax.experimental.pallas import tpu as pltpu

f32 = jnp.float32
bf16 = jnp.bfloat16

GRID_W = 64
HEADS = 4
DK = 64
DV = 128
GLA_RANK = 16
GLA_TAU = 16.0
S5_GROUP = 16
S5_STATE = 64
HY_BANDS = 16
HY_SHORT = 3
CHUNK = 64
LN_EPS = 1e-5
NORM_EPS = 1e-6

LANES = 128
SUBLANES = 8
VMEM_LIMIT = 52 * 1024 * 1024

GRP = HEADS * DV
OFF_GLA = 0
OFF_RET = 3 * GRP
OFF_S5 = 6 * GRP
OFF_HY = 7 * GRP
OFF_LR = 10 * GRP
NP_IN = 5376
TN_IN = 1792
DENSE_TM = 512
MLP_TF = 1024


def _cparams(sem, vmem=VMEM_LIMIT):
    return pltpu.CompilerParams(dimension_semantics=sem, vmem_limit_bytes=vmem)


def _ln(x):
    mu = jnp.mean(x, axis=-1, keepdims=True)
    xc = x - mu
    var = jnp.mean(xc * xc, axis=-1, keepdims=True)
    return xc * lax.rsqrt(var + LN_EPS)


def _silu(x):
    return x * jax.nn.sigmoid(x)


def _bdot(a, b):
    return jnp.dot(a.astype(bf16), b.astype(bf16), preferred_element_type=f32)


def _bdot_nt(a, b):
    return lax.dot_general(a.astype(bf16), b.astype(bf16), (((1,), (1,)), ((), ())), preferred_element_type=f32)


def _bdot_tn(a, b):
    return lax.dot_general(a.astype(bf16), b.astype(bf16), (((0,), (0,)), ((), ())), preferred_element_type=f32)


def _dot_exact_lhs(a_bf, x):
    hi = x.astype(bf16)
    lo = (x - hi.astype(f32)).astype(bf16)
    d = lambda b: jnp.dot(a_bf, b, preferred_element_type=f32)
    return d(hi) + d(lo)


def _dot3(a, b):
    ah = a.astype(bf16)
    al = (a - ah.astype(f32)).astype(bf16)
    bh = b.astype(bf16)
    bl = (b - bh.astype(f32)).astype(bf16)
    d = lambda p, q: jnp.dot(p, q, preferred_element_type=f32)
    return d(ah, bh) + (d(al, bh) + d(ah, bl))


def _ada_kernel(c_ref, w_ref, b_ref, o_ref):
    s = _silu(c_ref[...])
    o_ref[...] = _bdot(s, w_ref[...]) + b_ref[...]


def _ada(c_rows, ada_w, ada_b):
    depth, d, n = ada_w.shape
    r = c_rows.shape[0]
    tn = 1024
    return pl.pallas_call(
        _ada_kernel,
        out_shape=jax.ShapeDtypeStruct((depth, r, n), f32),
        grid=(depth, n // tn),
        in_specs=[pl.BlockSpec((r, d), lambda l, j: (0, 0)),
                  pl.BlockSpec((None, d, tn), lambda l, j: (l, 0, j)),
                  pl.BlockSpec((None, 1, tn), lambda l, j: (l, 0, j))],
        out_specs=pl.BlockSpec((None, r, tn), lambda l, j: (l, 0, j)),
        compiler_params=_cparams(("parallel", "parallel")),
        name="ada",
    )(c_rows, ada_w, ada_b.reshape(depth, 1, n))


def _in_kernel(x_ref, mod_ref, w_ref, o_ref, u_ref, h_ref, *, nsub, sub):
    j = pl.program_id(1)

    @pl.when(j == 0)
    def _():
        y = _ln(x_ref[...])
        h_ref[...] = (y * (1.0 + mod_ref[1:2, :]) + mod_ref[0:1, :]).astype(bf16)

    o_ref[...] = jnp.dot(h_ref[...], w_ref[...], preferred_element_type=f32)

    @pl.when(j == OFF_S5 // TN_IN)
    def _():
        c0 = OFF_S5 % TN_IN
        for k in range(nsub):
            u_ref[:, k * GRP:(k + 1) * GRP] = o_ref[k * sub:(k + 1) * sub, c0:c0 + GRP]


def _in_proj(x, mod, layer, row_fn, w_perm, tm, seq):
    t, d = x.shape
    assert OFF_S5 // TN_IN == (OFF_S5 + GRP - 1) // TN_IN
    sub = min(tm, seq)
    nsub = tm // sub
    tps = seq // sub
    return pl.pallas_call(
        functools.partial(_in_kernel, nsub=nsub, sub=sub),
        out_shape=[jax.ShapeDtypeStruct((t, NP_IN), f32), jax.ShapeDtypeStruct((seq, (t // seq) * GRP), f32)],
        grid=(t // tm, NP_IN // TN_IN),
        in_specs=[pl.BlockSpec((tm, d), lambda i, j: (i, 0)),
                  pl.BlockSpec((None, None, 6, d), lambda i, j: (layer, row_fn(i), 0, 0)),
                  pl.BlockSpec((d, TN_IN), lambda i, j: (0, j))],
        out_specs=[pl.BlockSpec((tm, TN_IN), lambda i, j: (i, j)),
                   pl.BlockSpec((sub, nsub * GRP), lambda i, j: (i % tps, i // tps))],
        scratch_shapes=[pltpu.VMEM((tm, d), bf16)],
        compiler_params=_cparams(("parallel", "arbitrary")),
        name="in_proj",
    )(x, mod, w_perm)


LA_TAIL = 256


def _linattn_kernel(*refs, mode, seq, has_s0, want_state):
    it = iter(refs)
    qk_ref, v_ref, g_ref = next(it), next(it), next(it)
    if mode == "gla":
        lr_ref, wg_ref, bg_ref, nw_ref = next(it), next(it), next(it), next(it)
    else:
        dexp_ref = next(it)
    s0_ref = next(it) if has_s0 else None
    o_ref = next(it)
    sf_ref = next(it) if want_state else None
    ob_ref, st_ref = next(it), next(it)
    if mode == "gla":
        qkd_ref, ku_ref, dec_ref = next(it), next(it), next(it)

    c = CHUNK
    nchunk = seq // c
    scale = DK ** -0.5
    row = lax.broadcasted_iota(jnp.int32, (c, c), 0)
    col = lax.broadcasted_iota(jnp.int32, (c, c), 1)
    is_k = (lax.broadcasted_iota(jnp.int32, (1, GRP), 1) & DK) != 0
    hq = lambda a, h: a[:, h * LANES:h * LANES + DK]
    hk = lambda a, h: a[:, h * LANES + DK:(h + 1) * LANES]
    hv = lambda h: slice(h * DV, (h + 1) * DV)

    if has_s0:
        st_ref[...] = s0_ref[...]
    else:
        st_ref[...] = jnp.zeros_like(st_ref)

    per_dir = []
    for d in (0, 1):
        causal = (row >= col) if d == 0 else (row <= col)
        if mode == "gla":
            tri = jnp.where(causal, 1.0, 0.0).astype(bf16)
            sgn = jnp.where(is_k, -1.0, 1.0)
            qsc = jnp.where(is_k, 1.0, scale)
            per_dir.append((causal, tri, sgn, qsc))
        else:
            lgr = jnp.log1p(-jnp.exp2(-dexp_ref[d]))[0:1, :]
            rowg = lax.broadcasted_iota(jnp.int32, (c, GRP), 0)
            pos = rowg if d == 0 else (c - 1 - rowg)
            pw = jnp.where(is_k, c - 1 - pos, pos + 1).astype(f32)
            wqk = jnp.exp(lgr * pw) * jnp.where(is_k, scale, 1.0)
            dist = ((row - col) if d == 0 else (col - row)).astype(f32)
            masks = [jnp.where(causal, jnp.exp(jnp.broadcast_to(hq(lgr, h), (c, c)) * dist), 0.0) * scale
                     for h in range(HEADS)]
            dec = jnp.exp(lgr * float(c))
            per_dir.append((wqk, masks, dec))

    if mode == "gla":
        def pre(n, carry):
            rows = pl.ds(pl.multiple_of(n * c, c), c)
            qk = qk_ref[rows, :]
            lrc = lr_ref[rows, :]
            logits = [_bdot(lrc, wg_ref[d]) + bg_ref[d] for d in (0, 1)]
            la = [(jnp.minimum(x, 0.0) - jnp.log(1.0 + jnp.exp(-jnp.abs(x)))) * (1.0 / GLA_TAU) for x in logits]
            b = [_dot_exact_lhs(per_dir[d][1], la[d]) for d in (0, 1)]
            for d in (0, 1):
                _, _, sgn, qsc = per_dir[d]
                last = c - 1 if d == 0 else 0
                btot = b[d][last:last + 1, :]
                qkd_ref[d, rows, :] = (qk * (jnp.exp(b[d] * sgn) * qsc)).astype(bf16)
                ku_ref[d, rows, :] = (qk * jnp.exp(btot - b[d])).astype(bf16)
                dec_ref[d, n] = jnp.exp(btot)
            return carry

        lax.fori_loop(0, nchunk, pre, 0, unroll=2)

    def body(i, carry):
        rows, qka, qko, ku, dec = [], [], [], [], []
        for d in (0, 1):
            n = i if d == 0 else nchunk - 1 - i
            rows.append(pl.ds(pl.multiple_of(n * c, c), c))
            if mode == "gla":
                qka.append(qkd_ref[d, rows[d], :])
                qko.append(qka[d])
                ku.append(ku_ref[d, rows[d], :])
                dec.append(dec_ref[d, n])
            else:
                wqk, _, dcy = per_dir[d]
                qka.append(qk_ref[rows[d], :])
                qko.append(qka[d] * wqk)
                ku.append(qko[d])
                dec.append(dcy)
        chains = [(d, h) for d in (0, 1) for h in range(HEADS)]
        v = {(d, h): v_ref[rows[d], hv(h)] for d, h in chains}
        st = {(d, h): st_ref[d, h] for d, h in chains}
        att = {(d, h): _bdot_nt(hq(qka[d], h), hk(qka[d], h)) for d, h in chains}
        oi = {(d, h): _bdot_nt(hq(qko[d], h), st[d, h]) for d, h in chains}
        up = {(d, h): _bdot_tn(v[d, h], hk(ku[d], h)) for d, h in chains}
        for d, h in chains:
            if mode == "gla":
                a = jnp.where(per_dir[d][0], att[d, h], 0.0)
            else:
                a = att[d, h] * per_dir[d][1][h]
            o = _bdot(a, v[d, h]) + oi[d, h]
            st_ref[d, h] = hq(dec[d], h) * st[d, h] + up[d, h]
            if d == 0:
                o_ref[rows[d], hv(h)] = o
            else:
                ob_ref[rows[d], hv(h)] = o
        return carry

    lax.fori_loop(0, nchunk, body, 0, unroll=4)
    if want_state:
        sf_ref[...] = st_ref[...]

    tl = min(LA_TAIL, seq)

    def tail(i, carry):
        rows = pl.ds(pl.multiple_of(i * tl, tl), tl)
        for h in range(HEADS):
            o = o_ref[rows, hv(h)] + ob_ref[rows, hv(h)]
            if mode == "gla":
                o = o * lax.rsqrt(jnp.mean(o * o, axis=-1, keepdims=True) + NORM_EPS) * nw_ref[:, hv(h)]
            else:
                o = _ln(o)
            o_ref[rows, hv(h)] = o * _silu(g_ref[rows, hv(h)])
        return carry

    lax.fori_loop(0, seq // tl, tail, 0)


def _linattn(proj, col0, extra, s0t, layer, *, mode, nbatch, seq, want_state):
    cb = col0 // GRP
    full = lambda a: pl.BlockSpec(a.shape, lambda b: (0,) * a.ndim)
    in_specs = [pl.BlockSpec((seq, GRP), lambda b: (b, cb)),
                pl.BlockSpec((seq, GRP), lambda b: (b, cb + 1)),
                pl.BlockSpec((seq, GRP), lambda b: (b, cb + 2), pipeline_mode=pl.Buffered(1))]
    args = [proj, proj, proj]
    scratch = [pltpu.VMEM((seq, GRP), f32), pltpu.VMEM((2, HEADS, DV, DK), f32)]
    if mode == "gla":
        scratch += [pltpu.VMEM((2, seq, GRP), bf16), pltpu.VMEM((2, seq, GRP), bf16),
                    pltpu.VMEM((2, seq // CHUNK, 1, GRP), f32)]
    if mode == "gla":
        in_specs += [pl.BlockSpec((seq, LANES), lambda b: (b, OFF_LR // LANES))] + [full(a) for a in extra]
        args += [proj, *extra]
    else:
        in_specs += [full(a) for a in extra]
        args += list(extra)
    has_s0 = s0t is not None
    if has_s0:
        in_specs += [pl.BlockSpec((None, None, 2, HEADS, DV, DK), lambda b: (b, layer, 0, 0, 0, 0))]
        args += [s0t]
    out_shape = [jax.ShapeDtypeStruct((nbatch * seq, GRP), f32)]
    out_specs = [pl.BlockSpec((seq, GRP), lambda b: (b, 0))]
    if want_state:
        out_shape += [jax.ShapeDtypeStruct((nbatch, 2, HEADS, DV, DK), f32)]
        out_specs += [pl.BlockSpec((None, 2, HEADS, DV, DK), lambda b: (b, 0, 0, 0, 0))]
    res = pl.pallas_call(
        functools.partial(_linattn_kernel, mode=mode, seq=seq, has_s0=has_s0, want_state=want_state),
        out_shape=out_shape,
        grid=(nbatch,),
        in_specs=in_specs,
        out_specs=out_specs,
        scratch_shapes=scratch,
        compiler_params=_cparams(("parallel",)),
        name="linattn_" + mode,
    )(*args)
    return (res[0], res[1]) if want_state else (res[0], None)


def _s5_prep_kernel(are_ref, aim_ref, lstep_ref, bre_ref, bim_ref, lre_ref, lim_ref, bbre_ref, bbim_ref):
    ar, ai = are_ref[...], aim_ref[...]
    st = jnp.exp(lstep_ref[...])
    mag = jnp.exp(ar * st)
    lr = mag * jnp.cos(ai * st)
    li = mag * jnp.sin(ai * st)
    nr, ni = lr - 1.0, li
    den = ar * ar + ai * ai
    kr = (nr * ar + ni * ai) / den
    ki = (ni * ar - nr * ai) / den
    br, bi = bre_ref[...], bim_ref[...]
    lre_ref[...] = lr
    lim_ref[...] = li
    bbre_ref[...] = kr * br - ki * bi
    bbim_ref[...] = kr * bi + ki * br


def _s5_prep(a_re, a_im, log_step, b_re, b_im):
    _, g, p = a_re.shape
    hch = b_re.shape[-1]
    rows = 2 * g * hch
    bc = lambda a: jnp.broadcast_to(a[:, :, None, :], (2, g, hch, p)).reshape(rows, p)
    tr = lambda b: jnp.transpose(b, (0, 1, 3, 2)).reshape(rows, p)
    ls = jnp.broadcast_to(log_step[:, :, None, None], (2, g, hch, p)).reshape(rows, p)
    spec = pl.BlockSpec((rows, p), lambda: (0, 0))
    lre, lim, bbre, bbim = pl.pallas_call(
        _s5_prep_kernel,
        out_shape=[jax.ShapeDtypeStruct((rows, p), f32)] * 4,
        in_specs=[spec] * 5,
        out_specs=[spec] * 4,
        name="s5_prep",
    )(bc(a_re), bc(a_im), ls, tr(b_re), tr(b_im))
    r4 = lambda a: a.reshape(2, g, hch, p)
    return r4(lre)[:, :, 0, :], r4(lim)[:, :, 0, :], r4(bbre), r4(bbim)


S5_TC = 64
S5_GPT = LANES // S5_GROUP
S5_SW = S5_GPT * S5_STATE


def _s5_scan_kernel(*refs, seq, nb, has_h0, want_state):
    it = iter(refs)
    u_ref, wb_ref, lam_ref, cre_ref, cim_ref, dsk_ref = (next(it) for _ in range(6))
    h0re_ref = next(it) if has_h0 else None
    h0im_ref = next(it) if has_h0 else None
    y_ref = next(it)
    hfre_ref = next(it) if want_state else None
    hfim_ref = next(it) if want_state else None
    bu_ref, hs_ref = next(it), next(it)

    tc = S5_TC
    nchunk = seq // tc
    sw = S5_SW

    def skip(i, carry):
        rows = pl.ds(pl.multiple_of(i * tc, tc), tc)
        y_ref[rows] = u_ref[rows] * dsk_ref[...]
        return carry

    lax.fori_loop(0, nchunk, skip, 0)

    lam = [(jnp.broadcast_to(lam_ref[d, 0:1, :], (nb, sw)), jnp.broadcast_to(lam_ref[d, 1:2, :], (nb, sw)))
           for d in (0, 1)]

    def chunk(i, carry):
        h = [list(carry[0:2]), list(carry[2:4])]
        t0 = [pl.multiple_of(i * tc, tc), pl.multiple_of((nchunk - 1 - i) * tc, tc)]
        for d in (0, 1):
            ub = u_ref[pl.ds(t0[d], tc)]
            bu_ref[d] = _bdot(ub.reshape(tc * nb, LANES), wb_ref[d])
        for s in range(tc):
            for d in (0, 1):
                t = s if d == 0 else tc - 1 - s
                rs = slice(t * nb, (t + 1) * nb)
                lam_r, lam_i = lam[d]
                hre, him = h[d]
                nre = lam_r * hre - lam_i * him + bu_ref[d, rs, 0:sw]
                nim = lam_r * him + lam_i * hre + bu_ref[d, rs, sw:2 * sw]
                h[d] = [nre, nim]
                hs_ref[d, rs, 0:sw] = nre
                hs_ref[d, rs, sw:2 * sw] = nim
        for d in (0, 1):
            y = _bdot(hs_ref[d, :, 0:sw], cre_ref[d]) - _bdot(hs_ref[d, :, sw:2 * sw], cim_ref[d])
            rows = pl.ds(t0[d], tc)
            y_ref[rows] = y_ref[rows] + y.reshape(tc, nb, LANES)
        return (*h[0], *h[1])

    if has_h0:
        h0 = (h0re_ref[0], h0im_ref[0], h0re_ref[1], h0im_ref[1])
    else:
        h0 = (jnp.zeros((nb, sw), f32),) * 4
    hfin = lax.fori_loop(0, nchunk, chunk, h0)
    if want_state:
        for d in (0, 1):
            hfre_ref[d] = hfin[2 * d]
            hfim_ref[d] = hfin[2 * d + 1]


def _s5_scan(u_t, wb, lam, cre, cim, dsk, h0, *, want_state):
    seq, nb, ch = u_t.shape
    ntile = ch // LANES
    sw = S5_SW
    in_specs = [pl.BlockSpec((seq, nb, LANES), lambda j: (0, 0, j)),
                pl.BlockSpec((2, None, LANES, 2 * sw), lambda j: (0, j, 0, 0)),
                pl.BlockSpec((2, None, 2, sw), lambda j: (0, j, 0, 0)),
                pl.BlockSpec((2, None, sw, LANES), lambda j: (0, j, 0, 0)),
                pl.BlockSpec((2, None, sw, LANES), lambda j: (0, j, 0, 0)),
                pl.BlockSpec((1, LANES), lambda j: (0, j))]
    args = [u_t, wb, lam, cre, cim, dsk]
    has_h0 = h0 is not None
    st_spec = pl.BlockSpec((2, None, nb, sw), lambda j: (0, j, 0, 0))
    if has_h0:
        in_specs += [st_spec, st_spec]
        args += list(h0)
    out_shape = [jax.ShapeDtypeStruct((seq, nb, ch), f32)]
    out_specs = [pl.BlockSpec((seq, nb, LANES), lambda j: (0, 0, j))]
    if want_state:
        out_shape += [jax.ShapeDtypeStruct((2, ntile, nb, sw), f32)] * 2
        out_specs += [st_spec, st_spec]
    res = pl.pallas_call(
        functools.partial(_s5_scan_kernel, seq=seq, nb=nb, has_h0=has_h0, want_state=want_state),
        out_shape=out_shape,
        grid=(ntile,),
        in_specs=in_specs,
        out_specs=out_specs,
        scratch_shapes=[pltpu.VMEM((2, S5_TC * nb, 2 * sw), f32), pltpu.VMEM((2, S5_TC * nb, 2 * sw), f32)],
        compiler_params=_cparams(("parallel",)),
        name="s5_scan",
    )(*args)
    return (res[0], res[1], res[2]) if want_state else (res[0], None, None)


HY_TK = 256
HY_CT = 256


def _hy_filter_kernel(feat_ref, w1_ref, b1_ref, w2_ref, b2_ref, fq_ref, w3f_ref, w3b_ref, dcf_ref, dcb_ref,
                      fre_ref, fim_ref, fa_ref, fb_ref, fd_ref, fs_ref, fdif_ref, nyq_ref, *, seq):
    j = pl.program_id(1)

    @pl.when(j == 0)
    def _():
        feat = feat_ref[...]
        fq = fq_ref[...]
        h = jnp.sin(fq * (_dot3(feat, w1_ref[...]) + b1_ref[...]))
        h = jnp.sin(fq * (_dot3(h, w2_ref[...]) + b2_ref[...]))
        t = feat[:, 0:1]
        rowi = lax.broadcasted_iota(jnp.int32, (seq, HY_CT), 0)

        def filt(w3_ref, dc_ref):
            f = _dot3(h, w3_ref[...]) * jnp.exp(-t * jnp.abs(dc_ref[...]))
            return f / jnp.sum(jnp.abs(f), axis=0, keepdims=True)

        ff = filt(w3f_ref, dcf_ref)
        fb = jnp.where(rowi == 0, 0.0, filt(w3b_ref, dcb_ref))
        fsum = ff + fb
        fs_ref[...] = fsum.astype(bf16)
        fdif_ref[...] = (ff - fb).astype(bf16)
        sign = jnp.where((rowi & 1) == 0, 1.0, -1.0)
        nyq_ref[...] = jnp.sum(fsum * sign, axis=0, keepdims=True)

    scl = 1.0 / seq
    fa = jnp.dot(fre_ref[...], fs_ref[...], preferred_element_type=f32) * scl
    fb = jnp.dot(fim_ref[...], fdif_ref[...], preferred_element_type=f32) * scl
    first = jnp.logical_and(lax.broadcasted_iota(jnp.int32, (HY_TK, HY_CT), 0) == 0, j == 0)
    fa_ref[...] = jnp.where(first, 0.5 * fa, fa)
    fb_ref[...] = jnp.where(first, 0.0, fb)
    fd_ref[...] = jnp.where(first, (0.5 * scl) * nyq_ref[...], fa)


def _hy_filter(feat, w1p, b1, w2, b2, fq, w3, dc, fre, fim):
    seq = feat.shape[0]
    ncol = w3.shape[1] // 2
    nct = ncol // HY_CT
    tk = min(HY_TK, seq)
    assert tk == HY_TK
    full = lambda a: pl.BlockSpec(a.shape, lambda c, j: (0,) * a.ndim)
    out_spec = pl.BlockSpec((tk, HY_CT), lambda c, j: (j, c))
    return pl.pallas_call(
        functools.partial(_hy_filter_kernel, seq=seq),
        out_shape=[jax.ShapeDtypeStruct((seq, ncol), f32)] * 3,
        grid=(nct, seq // tk),
        in_specs=[full(feat), full(w1p), full(b1), full(w2), full(b2), full(fq),
                  pl.BlockSpec((w3.shape[0], HY_CT), lambda c, j: (0, c)),
                  pl.BlockSpec((w3.shape[0], HY_CT), lambda c, j: (0, nct + c)),
                  pl.BlockSpec((1, HY_CT), lambda c, j: (0, c)),
                  pl.BlockSpec((1, HY_CT), lambda c, j: (0, nct + c)),
                  pl.BlockSpec((tk, seq), lambda c, j: (j, 0)),
                  pl.BlockSpec((tk, seq), lambda c, j: (j, 0))],
        out_specs=[out_spec] * 3,
        scratch_shapes=[pltpu.VMEM((seq, HY_CT), bf16), pltpu.VMEM((seq, HY_CT), bf16), pltpu.VMEM((1, HY_CT), f32)],
        compiler_params=_cparams(("parallel", "arbitrary")),
        name="hy_filter",
    )(feat, w1p, b1, w2, b2, fq, w3, w3, dc, dc, fre, fim)


def _short_conv(x, w_ref, b_ref, seg):
    rows = x.shape[0]
    r = lax.broadcasted_iota(jnp.int32, x.shape, 0) & (seg - 1)
    xp = jnp.where(r == 0, 0.0, pltpu.roll(x, 1, 0))
    xn = jnp.where(r == seg - 1, 0.0, pltpu.roll(x, rows - 1, 0))
    return xp * w_ref[0:1, :] + x * w_ref[1:2, :] + xn * w_ref[2:3, :] + b_ref[...]


def _longconv_kernel(yin_ref, gate_ref, cwy_ref, cby_ref, cwg_ref, cbg_ref, fre_ref, fim_ref, ire_ref, iim_ref,
                     fa_ref, fb_ref, fd_ref, dsk_ref, o_ref, y32_ref, ybf_ref, acc_ref, *, seg, conv_yin):
    j = pl.program_id(1)

    @pl.when(j == 0)
    def _():
        y = yin_ref[...]
        if conv_yin:
            y = _short_conv(y, cwy_ref, cby_ref, seg)
        y32_ref[...] = y
        ybf_ref[...] = y.astype(bf16)
        acc_ref[...] = jnp.zeros_like(acc_ref)

    yb = ybf_ref[...]
    yre = jnp.dot(fre_ref[...], yb, preferred_element_type=f32)
    yim = jnp.dot(fim_ref[...], yb, preferred_element_type=f32)
    fa, fb, fd = fa_ref[...], fb_ref[...], fd_ref[...]
    zre = (yre * fa - yim * fb).astype(bf16)
    zim = (yre * fb + yim * fd).astype(bf16)
    acc_ref[...] += (jnp.dot(ire_ref[...], zre, preferred_element_type=f32)
                     + jnp.dot(iim_ref[...], zim, preferred_element_type=f32))

    @pl.when(j == pl.num_programs(1) - 1)
    def _():
        gate = _short_conv(gate_ref[...], cwg_ref, cbg_ref, seg)
        o_ref[...] = gate * (acc_ref[...] + dsk_ref[...] * y32_ref[...])


def _longconv(yin, yin_cb, gate_arr, gate_cb, cw, cb, part_y, part_g, fre, fim, iim, fa, fb, fd, dsk, order,
              *, nbatch, seq, seg, conv_yin):
    tk = HY_TK
    once = pl.Buffered(1)
    cw_spec = lambda p: pl.BlockSpec((HY_SHORT, GRP), lambda b, j: (0, p))
    cb_spec = lambda p: pl.BlockSpec((1, GRP), lambda b, j: (0, p))
    filt_spec = pl.BlockSpec((tk, GRP), lambda b, j: (j, order))
    return pl.pallas_call(
        functools.partial(_longconv_kernel, seg=seg, conv_yin=conv_yin),
        out_shape=jax.ShapeDtypeStruct((nbatch * seq, GRP), f32),
        grid=(nbatch, seq // tk),
        in_specs=[pl.BlockSpec((seq, GRP), lambda b, j: (b, yin_cb), pipeline_mode=once),
                  pl.BlockSpec((seq, GRP), lambda b, j: (b, gate_cb), pipeline_mode=once),
                  cw_spec(part_y), cb_spec(part_y), cw_spec(part_g), cb_spec(part_g),
                  pl.BlockSpec((tk, seq), lambda b, j: (j, 0)),
                  pl.BlockSpec((tk, seq), lambda b, j: (j, 0)),
                  pl.BlockSpec((seq, tk), lambda b, j: (0, j)),
                  pl.BlockSpec((seq, tk), lambda b, j: (0, j)),
                  filt_spec, filt_spec, filt_spec,
                  pl.BlockSpec((None, 1, GRP), lambda b, j: (order, 0, 0))],
        out_specs=pl.BlockSpec((seq, GRP), lambda b, j: (b, 0), pipeline_mode=once),
        scratch_shapes=[pltpu.VMEM((seq, GRP), f32), pltpu.VMEM((seq, GRP), bf16), pltpu.VMEM((seq, GRP), f32)],
        compiler_params=_cparams(("parallel", "arbitrary")),
        name="hy_longconv",
    )(yin, gate_arr, cw, cb, cw, cb, fre, fim, fre, iim, fa, fb, fd, dsk)


def _out_kernel(m0_ref, m1_ref, y2_ref, m3_ref, gw_ref, gb_ref, x_ref, mod_ref, w_ref, g_ref, b_ref, o_ref, *, alpha):
    y = y2_ref[...]
    z = y * (0.5 * (1.0 + jnp.tanh(math.sqrt(2.0 / math.pi) * (y + 0.044715 * (y * y * y)))))
    m2 = z * jax.nn.sigmoid(_bdot(z, gw_ref[...]) + gb_ref[...])
    acc = _bdot(m0_ref[...], w_ref[0 * GRP:1 * GRP, :])
    acc += _bdot(m1_ref[...], w_ref[1 * GRP:2 * GRP, :])
    acc += _bdot(m2, w_ref[2 * GRP:3 * GRP, :])
    acc += _bdot(m3_ref[...], w_ref[3 * GRP:4 * GRP, :])
    r = alpha * x_ref[...] + mod_ref[2:3, :] * acc
    o_ref[...] = _ln(r) * g_ref[...] + b_ref[...]


def _out_proj(mixes, s5_spec, glu_w, glu_b, x, mod, layer, row_fn, w_out, g, b, tm, alpha):
    t, d = x.shape
    mspec = pl.BlockSpec((tm, GRP), lambda i: (i, 0))
    const = lambda a: pl.BlockSpec(a.shape, lambda i: (0,) * a.ndim)
    return pl.pallas_call(
        functools.partial(_out_kernel, alpha=alpha),
        out_shape=jax.ShapeDtypeStruct((t, d), f32),
        grid=(t // tm,),
        in_specs=[mspec, mspec, s5_spec, mspec, const(glu_w), const(glu_b),
                  pl.BlockSpec((tm, d), lambda i: (i, 0)),
                  pl.BlockSpec((None, None, 6, d), lambda i: (layer, row_fn(i), 0, 0)),
                  pl.BlockSpec((d, d), lambda i: (0, 0), pipeline_mode=pl.Buffered(1)),
                  pl.BlockSpec((1, d), lambda i: (0, 0)),
                  pl.BlockSpec((1, d), lambda i: (0, 0))],
        out_specs=pl.BlockSpec((tm, d), lambda i: (i, 0)),
        compiler_params=_cparams(("parallel",)),
        name="out_proj",
    )(*mixes, glu_w, glu_b, x, mod, w_out, g, b)


def _mlp_kernel(x_ref, mod_ref, wu_ref, wd_ref, g_ref, b_ref, o_ref, h_ref, acc_ref, *, alpha):
    j = pl.program_id(1)

    @pl.when(j == 0)
    def _():
        y = _ln(x_ref[...])
        h_ref[...] = (y * (1.0 + mod_ref[4:5, :]) + mod_ref[3:4, :]).astype(bf16)
        acc_ref[...] = jnp.zeros_like(acc_ref)

    u = jnp.maximum(jnp.dot(h_ref[...], wu_ref[...], preferred_element_type=f32), 0.0)
    acc_ref[...] += jnp.dot((u * u).astype(bf16), wd_ref[...], preferred_element_type=f32)

    @pl.when(j == pl.num_programs(1) - 1)
    def _():
        r = alpha * x_ref[...] + mod_ref[5:6, :] * acc_ref[...]
        o_ref[...] = _ln(r) * g_ref[...] + b_ref[...]


def _mlp(x, mod, layer, row_fn, w_up, w_down, g, b, tm, tf, alpha):
    t, d = x.shape
    dff = w_up.shape[1]
    return pl.pallas_call(
        functools.partial(_mlp_kernel, alpha=alpha),
        out_shape=jax.ShapeDtypeStruct((t, d), f32),
        grid=(t // tm, dff // tf),
        in_specs=[pl.BlockSpec((tm, d), lambda i, j: (i, 0)),
                  pl.BlockSpec((None, None, 6, d), lambda i, j: (layer, row_fn(i), 0, 0)),
                  pl.BlockSpec((d, tf), lambda i, j: (0, j)),
                  pl.BlockSpec((tf, d), lambda i, j: (j, 0)),
                  pl.BlockSpec((1, d), lambda i, j: (0, 0)),
                  pl.BlockSpec((1, d), lambda i, j: (0, 0))],
        out_specs=pl.BlockSpec((tm, d), lambda i, j: (i, 0)),
        scratch_shapes=[pltpu.VMEM((tm, d), bf16), pltpu.VMEM((tm, d), f32)],
        compiler_params=_cparams(("parallel", "arbitrary")),
        name="mlp",
    )(x, mod, w_up, w_down, g, b)


def _perm_w_in(w):
    d = w.shape[0]
    hk, hv = HEADS * DK, HEADS * DV
    o = 0
    parts = {}
    for name, width in (("gq", hk), ("gk", hk), ("gv", hv), ("gg", GRP), ("glr", 2 * GLA_RANK),
                        ("rq", hk), ("rk", hk), ("rv", hv), ("rg", GRP), ("su", GRP), ("hy", 3 * GRP)):
        parts[name] = w[:, o:o + width]
        o += width
    qk = lambda q, k: jnp.concatenate([q.reshape(d, HEADS, DK), k.reshape(d, HEADS, DK)], axis=-1).reshape(d, 2 * hk)
    cols = [qk(parts["gq"], parts["gk"]), parts["gv"], parts["gg"],
            qk(parts["rq"], parts["rk"]), parts["rv"], parts["rg"],
            parts["su"], parts["hy"], parts["glr"]]
    wp = jnp.concatenate(cols, axis=1)
    return jnp.pad(wp, ((0, 0), (0, NP_IN - wp.shape[1]))).astype(bf16)


def _gla_gate_params(w_gate, b_gate, norm_w):
    r = w_gate.shape[1]
    dup = lambda a: jnp.concatenate([a, a], axis=-1).reshape(*a.shape[:-2], 2 * HEADS * DK)
    wg = dup(w_gate.reshape(2, r, HEADS, DK))
    full = jnp.zeros((2, LANES, 2 * HEADS * DK), f32)
    for d in range(2):
        full = full.at[d, d * r:(d + 1) * r, :].set(wg[d])
    return full.astype(bf16), dup(b_gate.reshape(2, 1, HEADS, DK)), jnp.tile(norm_w[None, :], (1, HEADS))


def _s5_block_weights(bbre, bbim, c_re, c_im):
    _, g, hch, p = bbre.shape
    ntile = g // S5_GPT
    eye = jnp.eye(S5_GPT, dtype=f32)

    def in_map(b):
        b = b.reshape(2, ntile, S5_GPT, hch, p)
        return jnp.einsum("dtghp,gk->dtghkp", b, eye).reshape(2, ntile, S5_GPT * hch, S5_GPT * p)

    def out_map(c):
        c = c.reshape(2, ntile, S5_GPT, hch, p)
        return jnp.einsum("dtghp,gk->dtgpkh", c, eye).reshape(2, ntile, S5_GPT * p, S5_GPT * hch)

    wb = jnp.concatenate([in_map(bbre), in_map(bbim)], axis=-1).astype(bf16)
    return wb, out_map(c_re).astype(bf16), out_map(c_im).astype(bf16)


@functools.lru_cache(maxsize=None)
def _dft_tables(seq):
    n = 2 * seq
    k = np.arange(seq, dtype=np.int64)[:, None]
    s1 = np.arange(seq // 64, dtype=np.int64)[None, :]
    s2 = np.arange(64, dtype=np.int64)[None, :]
    a = 2.0 * np.pi * ((k * s1 * 64) % n) / n
    b = 2.0 * np.pi * ((k * s2) % n) / n
    return tuple(np.asarray(t, np.float32) for t in (np.cos(a), np.sin(a), np.cos(b), np.sin(b)))


def _dft_mats(seq):
    ca, sa, cb, sb = (jnp.asarray(t) for t in _dft_tables(seq))
    cosm = (ca[:, :, None] * cb[:, None, :] - sa[:, :, None] * sb[:, None, :]).reshape(seq, seq)
    sinm = (sa[:, :, None] * cb[:, None, :] + ca[:, :, None] * sb[:, None, :]).reshape(seq, seq)
    sign = jnp.where((jnp.arange(seq) & 1) == 0, 1.0, -1.0).astype(f32)
    ri = jnp.arange(seq)[:, None]
    ci = jnp.arange(seq)[None, :]
    fim = jnp.where(ri == 0, sign[None, :], -sinm)
    iim = jnp.where(ci == 0, sign[:, None], -sinm)
    return cosm.astype(bf16), fim.astype(bf16), iim.astype(bf16)


@functools.lru_cache(maxsize=None)
def _hy_feats(seq):
    t = np.linspace(0.0, 1.0, seq, dtype=np.float32)[:, None]
    w = (2.0 * math.pi * np.arange(seq, dtype=np.float32)[:, None] / seq).astype(np.float32)
    fr = np.linspace(1e-4, HY_BANDS - 1.0, HY_BANDS, dtype=np.float32)[None, :]
    feats = np.concatenate([t, np.cos(fr * w), -np.sin(fr * w)], axis=-1).astype(np.float32)
    return np.pad(feats, ((0, 0), (0, LANES - feats.shape[1])))


def _layer_group(x, mod, layer, p, state, *, nbatch, seq, rows, tm, row_of, want_state, alpha):
    tm_dense = min(DENSE_TM, nbatch * seq)
    proj, u_t = _in_proj(x, mod, layer, row_of(tm_dense), p["w_in"], tm_dense, seq)

    o_gla, s_gla = _linattn(proj, OFF_GLA, (p["gla_wg"], p["gla_bg"], p["gla_nw"]),
                            None if state is None else state["gla"], layer,
                            mode="gla", nbatch=nbatch, seq=seq, want_state=want_state)
    o_ret, s_ret = _linattn(proj, OFF_RET, (p["ret_dexp"],),
                            None if state is None else state["ret"], layer,
                            mode="ret", nbatch=nbatch, seq=seq, want_state=want_state)

    u_t = u_t.reshape(seq, nbatch, GRP)
    h0 =None if state is None else (state["s5_re"][layer], state["s5_im"][layer])
    y_t, hf_re, hf_im = _s5_scan(u_t, p["s5_wb"], p["s5_lam"], p["s5_cre"], p["s5_cim"], p["s5_d"], h0,
                                 want_state=want_state)
    y_s5 = y_t.reshape(seq, nbatch * GRP)
    tpb = seq // tm
    s5_spec = pl.BlockSpec((tm, GRP), lambda i: (i % tpb, i // tpb))

    fre, fim, iim = _dft_mats(seq)
    fa, fb, fd = _hy_filter(jnp.asarray(_hy_feats(seq)), p["hy_w1"], p["hy_b1"], p["hy_w2"], p["hy_b2"],
                            p["hy_freq"], p["hy_w3"], p["hy_decay"], fre, fim)
    seg = seq // rows
    hy_cb = OFF_HY // GRP
    y1 = _longconv(proj, hy_cb, proj, hy_cb + 1, p["hy_cw"], p["hy_cb"], 0, 1, fre, fim, iim, fa, fb, fd,
                   p["hy_d"], 0, nbatch=nbatch, seq=seq, seg=seg, conv_yin=True)
    o_hy = _longconv(y1, 0, proj, hy_cb + 2, p["hy_cw"], p["hy_cb"], 0, 2, fre, fim, iim, fa, fb, fd,
                     p["hy_d"], 1, nbatch=nbatch, seq=seq, seg=seg, conv_yin=False)

    x = _out_proj((o_gla, o_ret, y_s5, o_hy), s5_spec, p["s5_glu_w"], p["s5_glu_b"], x, mod, layer, row_of(tm), p["w_out"],
                  p["ln1_g"], p["ln1_b"], tm, alpha)
    x = _mlp(x, mod, layer, row_of(tm_dense), p["w_up"], p["w_down"], p["ln2_g"], p["ln2_b"], tm_dense, MLP_TF, alpha)
    return x, (s_gla, s_ret, hf_re, hf_im)


def kernel(x_prompt, x_sample, state_gla, state_ret, state_s5_re, state_s5_im, c, c_ctx, ada_w, ada_b, w_in, gla_w_gate, gla_b_gate, gla_norm_w, ret_decay_exp, s5_a_re, s5_a_im, s5_log_step, s5_b_re, s5_b_im, s5_c_re, s5_c_im, s5_d, s5_glu_w, s5_glu_b, hy_conv_w, hy_conv_b, hy_f_w1, hy_f_b1, hy_f_w2, hy_f_b2, hy_f_freq, hy_f_w3, hy_decay, hy_d, w_out, ln1_g, ln1_b, w_up, w_down, ln2_g, ln2_b):
    bp, lp, d = x_prompt.shape
    bs, ls, _ = x_sample.shape
    depth = w_in.shape[0]
    alpha = (2 * depth) ** 0.25
    ngroup = s5_a_re.shape[2]
    ntile = ngroup // S5_GPT

    nrow = -(-(1 + bs) // SUBLANES) * SUBLANES
    c_rows = jnp.concatenate([c_ctx[None, :], c, jnp.zeros((nrow - 1 - bs, d), f32)], axis=0)
    mod = _ada(c_rows, ada_w, ada_b).reshape(depth, nrow, 6, d)

    tr_state = lambda s: jnp.swapaxes(s, -1, -2)
    s5_state = lambda s: jnp.transpose(s.reshape(bs, depth, 2, ntile, S5_SW), (1, 2, 3, 0, 4))
    lat_state = dict(gla=tr_state(state_gla), ret=tr_state(state_ret),
                     s5_re=s5_state(state_s5_re), s5_im=s5_state(state_s5_im))

    tm_ctx = min(lp, DENSE_TM)
    tm_lat = min(ls, DENSE_TM)
    yp = x_prompt.reshape(bp * lp, d)
    ys = x_sample.reshape(bs * ls, d)
    outs = []
    for l in range(depth):
        lre, lim, bbre, bbim = _s5_prep(s5_a_re[l], s5_a_im[l], s5_log_step[l], s5_b_re[l], s5_b_im[l])
        wb, cre, cim = _s5_block_weights(bbre, bbim, s5_c_re[l], s5_c_im[l])
        lam = jnp.stack([lre.reshape(2, ntile, S5_SW), lim.reshape(2, ntile, S5_SW)], axis=2)
        wg, bg, nw = _gla_gate_params(gla_w_gate[l], gla_b_gate[l], gla_norm_w[l])
        w1p = jnp.pad(hy_f_w1[l], ((0, LANES - hy_f_w1.shape[1]), (0, 0)))
        p = dict(
            w_in=_perm_w_in(w_in[l]),
            gla_wg=wg, gla_bg=bg, gla_nw=nw,
            ret_dexp=jnp.broadcast_to(ret_decay_exp[l][:, None, :, None], (2, SUBLANES, HEADS, LANES)).reshape(2, SUBLANES, GRP),
            s5_wb=wb, s5_lam=lam, s5_cre=cre, s5_cim=cim, s5_d=s5_d[l][None, :],
            s5_glu_w=s5_glu_w[l].astype(bf16), s5_glu_b=s5_glu_b[l][None, :],
            hy_cw=hy_conv_w[l], hy_cb=hy_conv_b[l][None, :],
            hy_w1=w1p, hy_b1=hy_f_b1[l][None, :], hy_w2=hy_f_w2[l], hy_b2=hy_f_b2[l][None, :],
            hy_freq=hy_f_freq[l][None, :], hy_w3=hy_f_w3[l], hy_decay=hy_decay[l][None, :],
            hy_d=hy_d[l][:, None, :],
            w_out=w_out[l].astype(bf16), ln1_g=ln1_g[l][None, :], ln1_b=ln1_b[l][None, :],
            w_up=w_up[l].astype(bf16), w_down=w_down[l].astype(bf16),
            ln2_g=ln2_g[l][None, :], ln2_b=ln2_b[l][None, :],
        )
        yp, st = _layer_group(yp, mod, l, p, None, nbatch=bp, seq=lp, rows=1, tm=tm_ctx,
                              row_of=lambda tile: (lambda i: 0), want_state=True, alpha=alpha)
        outs.append(st)
        ys, _ = _layer_group(ys, mod, l, p, lat_state, nbatch=bs, seq=ls, rows=ls // GRID_W, tm=tm_lat,
                             row_of=lambda tile: (lambda i: 1 + i // (ls // tile)), want_state=False, alpha=alpha)

    new_gla = jnp.stack([jnp.swapaxes(o[0], -1, -2) for o in outs], axis=1)
    new_ret = jnp.stack([jnp.swapaxes(o[1], -1, -2) for o in outs], axis=1)
    unpack = lambda h: jnp.transpose(h, (2, 0, 1, 3)).reshape(bp, 2, ngroup, S5_STATE)
    new_re = jnp.stack([unpack(o[2]) for o in outs], axis=1)
    new_im = jnp.stack([unpack(o[3]) for o in outs], axis=1)
    return (yp.reshape(bp, lp, d), ys.reshape(bs, ls, d), new_gla, new_ret, new_re, new_im)
```

```python
import functools
import math

import numpy as np
import jax
import jax.numpy as jnp
from jax import lax
from jax.experimental import pallas as pl
from jax.experimental.pallas import tpu as pltpu

f32 = jnp.float32
bf16 = jnp.bfloat16

GRID_W = 64
HEADS = 4
DK = 64
DV = 128
GLA_RANK = 16
GLA_TAU = 16.0
S5_GROUP = 16
S5_STATE = 64
HY_BANDS = 16
HY_SHORT = 3
CHUNK = 64
RET_CHUNK = 128
LN_EPS = 1e-5
NORM_EPS = 1e-6

LANES = 128
SUBLANES = 8
VMEM_LIMIT = 52 * 1024 * 1024

GRP = HEADS * DV
OFF_GLA = 0
OFF_RET = 3 * GRP
OFF_S5 = 6 * GRP
OFF_HY = 7 * GRP
OFF_LR = 10 * GRP
NP_IN = 5376
TN_IN = 1792
DENSE_TM = 512
MLP_TF = 1024


def _cparams(sem, vmem=VMEM_LIMIT):
    return pltpu.CompilerParams(dimension_semantics=sem, vmem_limit_bytes=vmem)


def _ln(x):
    mu = jnp.mean(x, axis=-1, keepdims=True)
    xc = x - mu
    var = jnp.mean(xc * xc, axis=-1, keepdims=True)
    return xc * lax.rsqrt(var + LN_EPS)


def _silu(x):
    return x * jax.nn.sigmoid(x)


def _bdot(a, b):
    return jnp.dot(a.astype(bf16), b.astype(bf16), preferred_element_type=f32)


def _bdot_nt(a, b):
    return lax.dot_general(a.astype(bf16), b.astype(bf16), (((1,), (1,)), ((), ())), preferred_element_type=f32)


def _bdot_tn(a, b):
    return lax.dot_general(a.astype(bf16), b.astype(bf16), (((0,), (0,)), ((), ())), preferred_element_type=f32)


def _dot_exact_lhs(a_bf, x):
    hi = x.astype(bf16)
    lo = (x - hi.astype(f32)).astype(bf16)
    d = lambda b: jnp.dot(a_bf, b, preferred_element_type=f32)
    return d(hi) + d(lo)


def _dot3(a, b):
    ah = a.astype(bf16)
    al = (a - ah.astype(f32)).astype(bf16)
    bh = b.astype(bf16)
    bl = (b - bh.astype(f32)).astype(bf16)
    d = lambda p, q: jnp.dot(p, q, preferred_element_type=f32)
    return d(ah, bh) + (d(al, bh) + d(ah, bl))


def _ada_kernel(c_ref, w_ref, b_ref, o_ref):
    s = _silu(c_ref[...])
    o_ref[...] = _bdot(s, w_ref[...]) + b_ref[...]


def _ada(c_rows, ada_w, ada_b):
    depth, d, n = ada_w.shape
    r = c_rows.shape[0]
    tn = 1024
    return pl.pallas_call(
        _ada_kernel,
        out_shape=jax.ShapeDtypeStruct((depth, r, n), f32),
        grid=(depth, n // tn),
        in_specs=[pl.BlockSpec((r, d), lambda l, j: (0, 0)),
                  pl.BlockSpec((None, d, tn), lambda l, j: (l, 0, j)),
                  pl.BlockSpec((None, 1, tn), lambda l, j: (l, 0, j))],
        out_specs=pl.BlockSpec((None, r, tn), lambda l, j: (l, 0, j)),
        compiler_params=_cparams(("parallel", "parallel")),
        name="ada",
    )(c_rows, ada_w, ada_b.reshape(depth, 1, n))


def _in_kernel(x_ref, mod_ref, w_ref, o_ref, u_ref, h_ref, *, nsub, sub):
    j = pl.program_id(1)

    @pl.when(j == 0)
    def _():
        y = _ln(x_ref[...])
        h_ref[...] = (y * (1.0 + mod_ref[1:2, :]) + mod_ref[0:1, :]).astype(bf16)

    cols = pl.ds(pl.multiple_of(j * TN_IN, LANES), TN_IN)
    o_ref[...] = jnp.dot(h_ref[...], w_ref[:, cols], preferred_element_type=f32)

    @pl.when(j == OFF_S5 // TN_IN)
    def _():
        c0 = OFF_S5 % TN_IN
        for k in range(nsub):
            u_ref[:, k * GRP:(k + 1) * GRP] = o_ref[k * sub:(k + 1) * sub, c0:c0 + GRP]


def _in_proj(x, mod, layer, row_fn, w_perm, tm, seq):
    t, d = x.shape
    assert OFF_S5 // TN_IN == (OFF_S5 + GRP - 1) // TN_IN
    sub = min(tm, seq)
    nsub = tm // sub
    tps = seq // sub
    return pl.pallas_call(
        functools.partial(_in_kernel, nsub=nsub, sub=sub),
        out_shape=[jax.ShapeDtypeStruct((t, NP_IN), f32), jax.ShapeDtypeStruct((seq, (t // seq) * GRP), f32)],
        grid=(t // tm, NP_IN // TN_IN),
        in_specs=[pl.BlockSpec((tm, d), lambda i, j: (i, 0)),
                  pl.BlockSpec((None, None, 6, d), lambda i, j: (layer, row_fn(i), 0, 0)),
                  pl.BlockSpec((d, NP_IN), lambda i, j: (0, 0), pipeline_mode=pl.Buffered(1))],
        out_specs=[pl.BlockSpec((tm, TN_IN), lambda i, j: (i, j)),
                   pl.BlockSpec((sub, nsub * GRP), lambda i, j: (i % tps, i // tps))],
        scratch_shapes=[pltpu.VMEM((tm, d), bf16)],
        compiler_params=_cparams(("parallel", "arbitrary")),
        name="in_proj",
    )(x, mod, w_perm)


LA_TAIL = 256


def _linattn_kernel(*refs, mode, seq, has_s0, want_state):
    it = iter(refs)
    qk_ref, v_ref, g_ref = next(it), next(it), next(it)
    if mode == "gla":
        lr_ref, wg_ref, bg_ref, nw_ref = next(it), next(it), next(it), next(it)
    else:
        dexp_ref = next(it)
    s0_ref = next(it) if has_s0 else None
    o_ref = next(it)
    sf_ref = next(it) if want_state else None
    ob_ref, st_ref = next(it), next(it)
    if mode == "gla":
        qkd_ref, ku_ref, dec_ref = next(it), next(it), next(it)

    c = CHUNK if mode == "gla" else RET_CHUNK
    nchunk = seq // c
    scale = DK ** -0.5
    row = lax.broadcasted_iota(jnp.int32, (c, c), 0)
    col = lax.broadcasted_iota(jnp.int32, (c, c), 1)
    is_k = (lax.broadcasted_iota(jnp.int32, (1, GRP), 1) & DK) != 0
    hq = lambda a, h: a[:, h * LANES:h * LANES + DK]
    hk = lambda a, h: a[:, h * LANES + DK:(h + 1) * LANES]
    hv = lambda h: slice(h * DV, (h + 1) * DV)

    if has_s0:
        st_ref[...] = s0_ref[...]
    else:
        st_ref[...] = jnp.zeros_like(st_ref)

    per_dir = []
    for d in (0, 1):
        causal = (row >= col) if d == 0 else (row <= col)
        if mode == "gla":
            tri = jnp.where(causal, 1.0, 0.0).astype(bf16)
            sgn = jnp.where(is_k, -1.0, 1.0)
            qsc = jnp.where(is_k, 1.0, scale)
            per_dir.append((causal, tri, sgn, qsc))
        else:
            lgr = jnp.log1p(-jnp.exp2(-dexp_ref[d]))[0:1, :]
            rowg = lax.broadcasted_iota(jnp.int32, (c, GRP), 0)
            pos = rowg if d == 0 else (c - 1 - rowg)
            pw = jnp.where(is_k, c - 1 - pos, pos + 1).astype(f32)
            wqk = jnp.exp(lgr * pw) * jnp.where(is_k, scale, 1.0)
            dist = ((row - col) if d == 0 else (col - row)).astype(f32)
            masks = [jnp.where(causal, jnp.exp(jnp.broadcast_to(lgr[:, h * LANES:h * LANES + c], (c, c)) * dist), 0.0) * scale
                     for h in range(HEADS)]
            dec = jnp.exp(lgr * float(c))
            per_dir.append((wqk, masks, dec))

    if mode == "gla":
        def pre(n, carry):
            rows = pl.ds(pl.multiple_of(n * c, c), c)
            qk = qk_ref[rows, :]
            lrc = lr_ref[rows, :]
            logits = [_bdot(lrc, wg_ref[d]) + bg_ref[d] for d in (0, 1)]
            la = [(jnp.minimum(x, 0.0) - jnp.log(1.0 + jnp.exp(-jnp.abs(x)))) * (1.0 / GLA_TAU) for x in logits]
            b = [_dot_exact_lhs(per_dir[d][1], la[d]) for d in (0, 1)]
            for d in (0, 1):
                _, _, sgn, qsc = per_dir[d]
                last = c - 1 if d == 0 else 0
                btot = b[d][last:last + 1, :]
                qkd_ref[d, rows, :] = (qk * (jnp.exp(b[d] * sgn) * qsc)).astype(bf16)
                ku_ref[d, rows, :] = (qk * jnp.exp(btot - b[d])).astype(bf16)
                dec_ref[d, n] = jnp.exp(btot)
            return carry

        lax.fori_loop(0, nchunk, pre, 0, unroll=2)

    def body(i, carry):
        rows, qka, qko, ku, dec = [], [], [], [], []
        for d in (0, 1):
            n = i if d == 0 else nchunk - 1 - i
            rows.append(pl.ds(pl.multiple_of(n * c, c), c))
            if mode == "gla":
                qka.append(qkd_ref[d, rows[d], :])
                qko.append(qka[d])
                ku.append(ku_ref[d, rows[d], :])
                dec.append(dec_ref[d, n])
            else:
                wqk, _, dcy = per_dir[d]
                qka.append(qk_ref[rows[d], :])
                qko.append(qka[d] * wqk)
                ku.append(qko[d])
                dec.append(dcy)
        chains = [(d, h) for d in (0, 1) for h in range(HEADS)]
        v = {(d, h): v_ref[rows[d], hv(h)] for d, h in chains}
        st = {(d, h): st_ref[d, h] for d, h in chains}
        att = {(d, h): _bdot_nt(hq(qka[d], h), hk(qka[d], h)) for d, h in chains}
        oi = {(d, h): _bdot_nt(hq(qko[d], h), st[d, h]) for d, h in chains}
        up = {(d, h): _bdot_tn(v[d, h], hk(ku[d], h)) for d, h in chains}
        for d, h in chains:
            if mode == "gla":
                a = jnp.where(per_dir[d][0], att[d, h], 0.0)
            else:
                a = att[d, h] * per_dir[d][1][h]
            o = _bdot(a, v[d, h]) + oi[d, h]
            st_ref[d, h] = hq(dec[d], h) * st[d, h] + up[d, h]
            if d == 0:
                o_ref[rows[d], hv(h)] = o
            else:
                ob_ref[rows[d], hv(h)] = o
        return carry

    lax.fori_loop(0, nchunk, body, 0, unroll=2)
    if want_state:
        sf_ref[...] = st_ref[...]

    tl = min(LA_TAIL, seq)

    def tail(i, carry):
        rows = pl.ds(pl.multiple_of(i * tl, tl), tl)
        for h in range(HEADS):
            o = o_ref[rows, hv(h)] + ob_ref[rows, hv(h)]
            if mode == "gla":
                o = o * lax.rsqrt(jnp.mean(o * o, axis=-1, keepdims=True) + NORM_EPS) * nw_ref[:, hv(h)]
            else:
                o = _ln(o)
            o_ref[rows, hv(h)] = o * _silu(g_ref[rows, hv(h)])
        return carry

    lax.fori_loop(0, seq // tl, tail, 0)


def _linattn(proj, col0, extra, s0t, layer, *, mode, nbatch, seq, want_state):
    cb = col0 // GRP
    full = lambda a: pl.BlockSpec(a.shape, lambda b: (0,) * a.ndim)
    in_specs = [pl.BlockSpec((seq, GRP), lambda b: (b, cb)),
                pl.BlockSpec((seq, GRP), lambda b: (b, cb + 1)),
                pl.BlockSpec((seq, GRP), lambda b: (b, cb + 2), pipeline_mode=pl.Buffered(1))]
    args = [proj, proj, proj]
    scratch = [pltpu.VMEM((seq, GRP), f32), pltpu.VMEM((2, HEADS, DV, DK), f32)]
    if mode == "gla":
        scratch += [pltpu.VMEM((2, seq, GRP), bf16), pltpu.VMEM((2, seq, GRP), bf16),
                    pltpu.VMEM((2, seq // CHUNK, 1, GRP), f32)]
    if mode == "gla":
        in_specs += [pl.BlockSpec((seq, LANES), lambda b: (b, OFF_LR // LANES))] + [full(a) for a in extra]
        args += [proj, *extra]
    else:
        in_specs += [full(a) for a in extra]
        args += list(extra)
    has_s0 = s0t is not None
    if has_s0:
        in_specs += [pl.BlockSpec((None, None, 2, HEADS, DV, DK), lambda b: (b, layer, 0, 0, 0, 0))]
        args += [s0t]
    out_shape = [jax.ShapeDtypeStruct((nbatch * seq, GRP), f32)]
    out_specs = [pl.BlockSpec((seq, GRP), lambda b: (b, 0))]
    if want_state:
        out_shape += [jax.ShapeDtypeStruct((nbatch, 2, HEADS, DV, DK), f32)]
        out_specs += [pl.BlockSpec((None, 2, HEADS, DV, DK), lambda b: (b, 0, 0, 0, 0))]
    res = pl.pallas_call(
        functools.partial(_linattn_kernel, mode=mode, seq=seq, has_s0=has_s0, want_state=want_state),
        out_shape=out_shape,
        grid=(nbatch,),
        in_specs=in_specs,
        out_specs=out_specs,
        scratch_shapes=scratch,
        compiler_params=_cparams(("parallel",)),
        name="linattn_" + mode,
    )(*args)
    return (res[0], res[1]) if want_state else (res[0], None)


def _s5_prep_kernel(are_ref, aim_ref, lstep_ref, bre_ref, bim_ref, lre_ref, lim_ref, bbre_ref, bbim_ref):
    ar, ai = are_ref[...], aim_ref[...]
    st = jnp.exp(lstep_ref[...])
    mag = jnp.exp(ar * st)
    lr = mag * jnp.cos(ai * st)
    li = mag * jnp.sin(ai * st)
    nr, ni = lr - 1.0, li
    den = ar * ar + ai * ai
    kr = (nr * ar + ni * ai) / den
    ki = (ni * ar - nr * ai) / den
    br, bi = bre_ref[...], bim_ref[...]
    lre_ref[...] = lr
    lim_ref[...] = li
    bbre_ref[...] = kr * br - ki * bi
    bbim_ref[...] = kr * bi + ki * br


def _s5_prep(a_re, a_im, log_step, b_re, b_im):
    _, g, p = a_re.shape
    hch = b_re.shape[-1]
    rows = 2 * g * hch
    bc = lambda a: jnp.broadcast_to(a[:, :, None, :], (2, g, hch, p)).reshape(rows, p)
    tr = lambda b: jnp.transpose(b, (0, 1, 3, 2)).reshape(rows, p)
    ls = jnp.broadcast_to(log_step[:, :, None, None], (2, g, hch, p)).reshape(rows, p)
    spec = pl.BlockSpec((rows, p), lambda: (0, 0))
    lre, lim, bbre, bbim = pl.pallas_call(
        _s5_prep_kernel,
        out_shape=[jax.ShapeDtypeStruct((rows, p), f32)] * 4,
        in_specs=[spec] * 5,
        out_specs=[spec] * 4,
        name="s5_prep",
    )(bc(a_re), bc(a_im), ls, tr(b_re), tr(b_im))
    r4 = lambda a: a.reshape(2, g, hch, p)
    return r4(lre)[:, :, 0, :], r4(lim)[:, :, 0, :], r4(bbre), r4(bbim)


S5_TC = 64
S5_GPT = LANES // S5_GROUP
S5_SW = S5_GPT * S5_STATE


def _s5_scan_kernel(*refs, seq, nb, has_h0, want_state):
    it = iter(refs)
    u_ref, wb_ref, lam_ref, cre_ref, cim_ref, dsk_ref = (next(it) for _ in range(6))
    h0re_ref = next(it) if has_h0 else None
    h0im_ref = next(it) if has_h0 else None
    y_ref = next(it)
    hfre_ref = next(it) if want_state else None
    hfim_ref = next(it) if want_state else None
    bu_ref, hs_ref = next(it), next(it)

    tc = S5_TC
    nchunk = seq // tc
    sw = S5_SW

    def skip(i, carry):
        rows = pl.ds(pl.multiple_of(i * tc, tc), tc)
        y_ref[rows] = u_ref[rows] * dsk_ref[...]
        return carry

    lax.fori_loop(0, nchunk, skip, 0)

    lam = [(jnp.broadcast_to(lam_ref[d, 0:1, :], (nb, sw)), jnp.broadcast_to(lam_ref[d, 1:2, :], (nb, sw)))
           for d in (0, 1)]

    def chunk(i, carry):
        h = [list(carry[0:2]), list(carry[2:4])]
        t0 = [pl.multiple_of(i * tc, tc), pl.multiple_of((nchunk - 1 - i) * tc, tc)]
        for d in (0, 1):
            ub = u_ref[pl.ds(t0[d], tc)]
            bu_ref[d] = _bdot(ub.reshape(tc * nb, LANES), wb_ref[d])
        for s in range(tc):
            for d in (0, 1):
                t = s if d == 0 else tc - 1 - s
                rs = slice(t * nb, (t + 1) * nb)
                lam_r, lam_i = lam[d]
                hre, him = h[d]
                nre = lam_r * hre - lam_i * him + bu_ref[d, rs, 0:sw]
                nim = lam_r * him + lam_i * hre + bu_ref[d, rs, sw:2 * sw]
                h[d] = [nre, nim]
                hs_ref[d, rs, 0:sw] = nre
                hs_ref[d, rs, sw:2 * sw] = nim
        for d in (0, 1):
            y = _bdot(hs_ref[d, :, 0:sw], cre_ref[d]) - _bdot(hs_ref[d, :, sw:2 * sw], cim_ref[d])
            rows = pl.ds(t0[d], tc)
            y_ref[rows] = y_ref[rows] + y.reshape(tc, nb, LANES)
        return (*h[0], *h[1])

    if has_h0:
        h0 = (h0re_ref[0], h0im_ref[0], h0re_ref[1], h0im_ref[1])
    else:
        h0 = (jnp.zeros((nb, sw), f32),) * 4
    hfin = lax.fori_loop(0, nchunk, chunk, h0)
    if want_state:
        for d in (0, 1):
            hfre_ref[d] = hfin[2 * d]
            hfim_ref[d] = hfin[2 * d + 1]


def _s5_scan(u_t, wb, lam, cre, cim, dsk, h0, *, want_state):
    seq, nb, ch = u_t.shape
    ntile = ch // LANES
    sw = S5_SW
    in_specs = [pl.BlockSpec((seq, nb, LANES), lambda j: (0, 0, j)),
                pl.BlockSpec((2, None, LANES, 2 * sw), lambda j: (0, j, 0, 0)),
                pl.BlockSpec((2, None, 2, sw), lambda j: (0, j, 0, 0)),
                pl.BlockSpec((2, None, sw, LANES), lambda j: (0, j, 0, 0)),
                pl.BlockSpec((2, None, sw, LANES), lambda j: (0, j, 0, 0)),
                pl.BlockSpec((1, LANES), lambda j: (0, j))]
    args = [u_t, wb, lam, cre, cim, dsk]
    has_h0 = h0 is not None
    st_spec = pl.BlockSpec((2, None, nb, sw), lambda j: (0, j, 0, 0))
    if has_h0:
        in_specs += [st_spec, st_spec]
        args += list(h0)
    out_shape = [jax.ShapeDtypeStruct((seq, nb, ch), f32)]
    out_specs = [pl.BlockSpec((seq, nb, LANES), lambda j: (0, 0, j))]
    if want_state:
        out_shape += [jax.ShapeDtypeStruct((2, ntile, nb, sw), f32)] * 2
        out_specs += [st_spec, st_spec]
    res = pl.pallas_call(
        functools.partial(_s5_scan_kernel, seq=seq, nb=nb, has_h0=has_h0, want_state=want_state),
        out_shape=out_shape,
        grid=(ntile,),
        in_specs=in_specs,
        out_specs=out_specs,
        scratch_shapes=[pltpu.VMEM((2, S5_TC * nb, 2 * sw), f32), pltpu.VMEM((2, S5_TC * nb, 2 * sw), f32)],
        compiler_params=_cparams(("parallel",)),
        name="s5_scan",
    )(*args)
    return (res[0], res[1], res[2]) if want_state else (res[0], None, None)


HY_TK = 256
HY_CT = 256


def _hy_filter_kernel(feat_ref, w1_ref, b1_ref, w2_ref, b2_ref, fq_ref, w3f_ref, w3b_ref, dcf_ref, dcb_ref,
                      fre_ref, fim_ref, fa_ref, fb_ref, fd_ref, fs_ref, fdif_ref, nyq_ref, *, seq):
    j = pl.program_id(1)

    @pl.when(j == 0)
    def _():
        feat = feat_ref[...]
        fq = fq_ref[...]
        h = jnp.sin(fq * (_dot3(feat, w1_ref[...]) + b1_ref[...]))
        h = jnp.sin(fq * (_dot3(h, w2_ref[...]) + b2_ref[...]))
        t = feat[:, 0:1]
        rowi = lax.broadcasted_iota(jnp.int32, (seq, HY_CT), 0)

        def filt(w3_ref, dc_ref):
            f = _dot3(h, w3_ref[...]) * jnp.exp(-t * jnp.abs(dc_ref[...]))
            return f / jnp.sum(jnp.abs(f), axis=0, keepdims=True)

        ff = filt(w3f_ref, dcf_ref)
        fb = jnp.where(rowi == 0, 0.0, filt(w3b_ref, dcb_ref))
        fsum = ff + fb
        fs_ref[...] = fsum.astype(bf16)
        fdif_ref[...] = (ff - fb).astype(bf16)
        sign = jnp.where((rowi & 1) == 0, 1.0, -1.0)
        nyq_ref[...] = jnp.sum(fsum * sign, axis=0, keepdims=True)

    scl = 1.0 / seq
    fa = jnp.dot(fre_ref[...], fs_ref[...], preferred_element_type=f32) * scl
    fb = jnp.dot(fim_ref[...], fdif_ref[...], preferred_element_type=f32) * scl
    first = jnp.logical_and(lax.broadcasted_iota(jnp.int32, (HY_TK, HY_CT), 0) == 0, j == 0)
    fa_ref[...] = jnp.where(first, 0.5 * fa, fa)
    fb_ref[...] = jnp.where(first, 0.0, fb)
    fd_ref[...] = jnp.where(first, (0.5 * scl) * nyq_ref[...], fa)


def _hy_filter(feat, w1p, b1, w2, b2, fq, w3, dc, fre, fim):
    seq = feat.shape[0]
    ncol = w3.shape[1] // 2
    nct = ncol // HY_CT
    tk = min(HY_TK, seq)
    assert tk == HY_TK
    full = lambda a: pl.BlockSpec(a.shape, lambda c, j: (0,) * a.ndim)
    out_spec = pl.BlockSpec((tk, HY_CT), lambda c, j: (j, c))
    return pl.pallas_call(
        functools.partial(_hy_filter_kernel, seq=seq),
        out_shape=[jax.ShapeDtypeStruct((seq, ncol), f32)] * 3,
        grid=(nct, seq // tk),
        in_specs=[full(feat), full(w1p), full(b1), full(w2), full(b2), full(fq),
                  pl.BlockSpec((w3.shape[0], HY_CT), lambda c, j: (0, c)),
                  pl.BlockSpec((w3.shape[0], HY_CT), lambda c, j: (0, nct + c)),
                  pl.BlockSpec((1, HY_CT), lambda c, j: (0, c)),
                  pl.BlockSpec((1, HY_CT), lambda c, j: (0, nct + c)),
                  pl.BlockSpec((tk, seq), lambda c, j: (j, 0)),
                  pl.BlockSpec((tk, seq), lambda c, j: (j, 0))],
        out_specs=[out_spec] * 3,
        scratch_shapes=[pltpu.VMEM((seq, HY_CT), bf16), pltpu.VMEM((seq, HY_CT), bf16), pltpu.VMEM((1, HY_CT), f32)],
        compiler_params=_cparams(("parallel", "arbitrary")),
        name="hy_filter",
    )(feat, w1p, b1, w2, b2, fq, w3, w3, dc, dc, fre, fim)


def _short_conv(x, w_ref, b_ref, seg):
    rows = x.shape[0]
    r = lax.broadcasted_iota(jnp.int32, x.shape, 0) & (seg - 1)
    xp = jnp.where(r == 0, 0.0, pltpu.roll(x, 1, 0))
    xn = jnp.where(r == seg - 1, 0.0, pltpu.roll(x, rows - 1, 0))
    return xp * w_ref[0:1, :] + x * w_ref[1:2, :] + xn * w_ref[2:3, :] + b_ref[...]


def _longconv_kernel(yin_ref, gate_ref, cwy_ref, cby_ref, cwg_ref, cbg_ref, fre_ref, fim_ref, ire_ref, iim_ref,
                     fa_ref, fb_ref, fd_ref, dsk_ref, o_ref, y32_ref, ybf_ref, acc_ref, *, seg, conv_yin):
    j = pl.program_id(1)

    @pl.when(j == 0)
    def _():
        y = yin_ref[...]
        if conv_yin:
            y = _short_conv(y, cwy_ref, cby_ref, seg)
        y32_ref[...] = y
        ybf_ref[...] = y.astype(bf16)
        acc_ref[...] = jnp.zeros_like(acc_ref)

    yb = ybf_ref[...]
    yre = jnp.dot(fre_ref[...], yb, preferred_element_type=f32)
    yim = jnp.dot(fim_ref[...], yb, preferred_element_type=f32)
    fa, fb, fd = fa_ref[...], fb_ref[...], fd_ref[...]
    zre = (yre * fa - yim * fb).astype(bf16)
    zim = (yre * fb + yim * fd).astype(bf16)
    acc_ref[...] += (jnp.dot(ire_ref[...], zre, preferred_element_type=f32)
                     + jnp.dot(iim_ref[...], zim, preferred_element_type=f32))

    @pl.when(j == pl.num_programs(1) - 1)
    def _():
        gate = _short_conv(gate_ref[...], cwg_ref, cbg_ref, seg)
        o_ref[...] = gate * (acc_ref[...] + dsk_ref[...] * y32_ref[...])


def _longconv(yin, yin_cb, gate_arr, gate_cb, cw, cb, part_y, part_g, fre, fim, iim, fa, fb, fd, dsk, order,
              *, nbatch, seq, seg, conv_yin):
    tk = HY_TK
    once = pl.Buffered(1)
    cw_spec = lambda p: pl.BlockSpec((HY_SHORT, GRP), lambda b, j: (0, p))
    cb_spec = lambda p: pl.BlockSpec((1, GRP), lambda b, j: (0, p))
    filt_spec = pl.BlockSpec((tk, GRP), lambda b, j: (j, order))
    return pl.pallas_call(
        functools.partial(_longconv_kernel, seg=seg, conv_yin=conv_yin),
        out_shape=jax.ShapeDtypeStruct((nbatch * seq, GRP), f32),
        grid=(nbatch, seq // tk),
        in_specs=[pl.BlockSpec((seq, GRP), lambda b, j: (b, yin_cb), pipeline_mode=once),
                  pl.BlockSpec((seq, GRP), lambda b, j: (b, gate_cb), pipeline_mode=once),
                  cw_spec(part_y), cb_spec(part_y), cw_spec(part_g), cb_spec(part_g),
                  pl.BlockSpec((tk, seq), lambda b, j: (j, 0)),
                  pl.BlockSpec((tk, seq), lambda b, j: (j, 0)),
                  pl.BlockSpec((seq, tk), lambda b, j: (0, j)),
                  pl.BlockSpec((seq, tk), lambda b, j: (0, j)),
                  filt_spec, filt_spec, filt_spec,
                  pl.BlockSpec((None, 1, GRP), lambda b, j: (order, 0, 0))],
        out_specs=pl.BlockSpec((seq, GRP), lambda b, j: (b, 0), pipeline_mode=once),
        scratch_shapes=[pltpu.VMEM((seq, GRP), f32), pltpu.VMEM((seq, GRP), bf16), pltpu.VMEM((seq, GRP), f32)],
        compiler_params=_cparams(("parallel", "arbitrary")),
        name="hy_longconv",
    )(yin, gate_arr, cw, cb, cw, cb, fre, fim, fre, iim, fa, fb, fd, dsk)


def _out_kernel(m0_ref, m1_ref, y2_ref, m3_ref, gw_ref, gb_ref, x_ref, mod_ref, w_ref, g_ref, b_ref, o_ref, *, alpha):
    y = y2_ref[...]
    z = y * (0.5 * (1.0 + jnp.tanh(math.sqrt(2.0 / math.pi) * (y + 0.044715 * (y * y * y)))))
    m2 = z * jax.nn.sigmoid(_bdot(z, gw_ref[...]) + gb_ref[...])
    acc = _bdot(m0_ref[...], w_ref[0 * GRP:1 * GRP, :])
    acc += _bdot(m1_ref[...], w_ref[1 * GRP:2 * GRP, :])
    acc += _bdot(m2, w_ref[2 * GRP:3 * GRP, :])
    acc += _bdot(m3_ref[...], w_ref[3 * GRP:4 * GRP, :])
    r = alpha * x_ref[...] + mod_ref[2:3, :] * acc
    o_ref[...] = _ln(r) * g_ref[...] + b_ref[...]


def _out_proj(mixes, s5_spec, glu_w, glu_b, x, mod, layer, row_fn, w_out, g, b, tm, alpha):
    t, d = x.shape
    mspec = pl.BlockSpec((tm, GRP), lambda i: (i, 0))
    const = lambda a: pl.BlockSpec(a.shape, lambda i: (0,) * a.ndim)
    return pl.pallas_call(
        functools.partial(_out_kernel, alpha=alpha),
        out_shape=jax.ShapeDtypeStruct((t, d), f32),
        grid=(t // tm,),
        in_specs=[mspec, mspec, s5_spec, mspec, const(glu_w), const(glu_b),
                  pl.BlockSpec((tm, d), lambda i: (i, 0)),
                  pl.BlockSpec((None, None, 6, d), lambda i: (layer, row_fn(i), 0, 0)),
                  pl.BlockSpec((d, d), lambda i: (0, 0), pipeline_mode=pl.Buffered(1)),
                  pl.BlockSpec((1, d), lambda i: (0, 0)),
                  pl.BlockSpec((1, d), lambda i: (0, 0))],
        out_specs=pl.BlockSpec((tm, d), lambda i: (i, 0)),
        compiler_params=_cparams(("parallel",)),
        name="out_proj",
    )(*mixes, glu_w, glu_b, x, mod, w_out, g, b)


def _mlp_kernel(x_ref, mod_ref, wu_ref, wd_ref, g_ref, b_ref, o_ref, h_ref, acc_ref, *, alpha):
    j = pl.program_id(1)

    @pl.when(j == 0)
    def _():
        y = _ln(x_ref[...])
        h_ref[...] = (y * (1.0 + mod_ref[4:5, :]) + mod_ref[3:4, :]).astype(bf16)
        acc_ref[...] = jnp.zeros_like(acc_ref)

    u = jnp.maximum(jnp.dot(h_ref[...], wu_ref[...], preferred_element_type=f32), 0.0)
    acc_ref[...] += jnp.dot((u * u).astype(bf16), wd_ref[...], preferred_element_type=f32)

    @pl.when(j == pl.num_programs(1) - 1)
    def _():
        r = alpha * x_ref[...] + mod_ref[5:6, :] * acc_ref[...]
        o_ref[...] = _ln(r) * g_ref[...] + b_ref[...]


def _mlp(x, mod, layer, row_fn, w_up, w_down, g, b, tm, tf, alpha):
    t, d = x.shape
    dff = w_up.shape[1]
    return pl.pallas_call(
        functools.partial(_mlp_kernel, alpha=alpha),
        out_shape=jax.ShapeDtypeStruct((t, d), f32),
        grid=(t // tm, dff // tf),
        in_specs=[pl.BlockSpec((tm, d), lambda i, j: (i, 0)),
                  pl.BlockSpec((None, None, 6, d), lambda i, j: (layer, row_fn(i), 0, 0)),
                  pl.BlockSpec((d, tf), lambda i, j: (0, j)),
                  pl.BlockSpec((tf, d), lambda i, j: (j, 0)),
                  pl.BlockSpec((1, d), lambda i, j: (0, 0)),
                  pl.BlockSpec((1, d), lambda i, j: (0, 0))],
        out_specs=pl.BlockSpec((tm, d), lambda i, j: (i, 0)),
        scratch_shapes=[pltpu.VMEM((tm, d), bf16), pltpu.VMEM((tm, d), f32)],
        compiler_params=_cparams(("parallel", "arbitrary")),
        name="mlp",
    )(x, mod, w_up, w_down, g, b)


def _perm_w_in(w):
    d = w.shape[0]
    w = w.astype(bf16)
    hk, hv = HEADS * DK, HEADS * DV
    o = 0
    parts = {}
    for name, width in (("gq", hk), ("gk", hk), ("gv", hv), ("gg", GRP), ("glr", 2 * GLA_RANK),
                        ("rq", hk), ("rk", hk), ("rv", hv), ("rg", GRP), ("su", GRP), ("hy", 3 * GRP)):
        parts[name] = w[:, o:o + width]
        o += width
    qk = lambda q, k: jnp.concatenate([q.reshape(d, HEADS, DK), k.reshape(d, HEADS, DK)], axis=-1).reshape(d, 2 * hk)
    cols = [qk(parts["gq"], parts["gk"]), parts["gv"], parts["gg"],
            qk(parts["rq"], parts["rk"]), parts["rv"], parts["rg"],
            parts["su"], parts["hy"], parts["glr"]]
    wp = jnp.concatenate(cols, axis=1)
    return jnp.pad(wp, ((0, 0), (0, NP_IN - wp.shape[1])))


def _gla_gate_params(w_gate, b_gate, norm_w):
    r = w_gate.shape[1]
    dup = lambda a: jnp.concatenate([a, a], axis=-1).reshape(*a.shape[:-2], 2 * HEADS * DK)
    wg = dup(w_gate.reshape(2, r, HEADS, DK))
    full = jnp.zeros((2, LANES, 2 * HEADS * DK), f32)
    for d in range(2):
        full = full.at[d, d * r:(d + 1) * r, :].set(wg[d])
    return full.astype(bf16), dup(b_gate.reshape(2, 1, HEADS, DK)), jnp.tile(norm_w[None, :], (1, HEADS))


def _s5_block_weights(bbre, bbim, c_re, c_im):
    _, g, hch, p = bbre.shape
    ntile = g // S5_GPT
    eye = jnp.eye(S5_GPT, dtype=f32)

    def in_map(b):
        b = b.reshape(2, ntile, S5_GPT, hch, p)
        return jnp.einsum("dtghp,gk->dtghkp", b, eye).reshape(2, ntile, S5_GPT * hch, S5_GPT * p)

    def out_map(c):
        c = c.reshape(2, ntile, S5_GPT, hch, p)
        return jnp.einsum("dtghp,gk->dtgpkh", c, eye).reshape(2, ntile, S5_GPT * p, S5_GPT * hch)

    wb = jnp.concatenate([in_map(bbre), in_map(bbim)], axis=-1).astype(bf16)
    return wb, out_map(c_re).astype(bf16), out_map(c_im).astype(bf16)


@functools.lru_cache(maxsize=None)
def _dft_tables(seq):
    n = 2 * seq
    k = np.arange(seq, dtype=np.int64)[:, None]
    s1 = np.arange(seq // 64, dtype=np.int64)[None, :]
    s2 = np.arange(64, dtype=np.int64)[None, :]
    a = 2.0 * np.pi * ((k * s1 * 64) % n) / n
    b = 2.0 * np.pi * ((k * s2) % n) / n
    return tuple(np.asarray(t, np.float32) for t in (np.cos(a), np.sin(a), np.cos(b), np.sin(b)))


def _dft_mats(seq):
    ca, sa, cb, sb = (jnp.asarray(t) for t in _dft_tables(seq))
    cosm = (ca[:, :, None] * cb[:, None, :] - sa[:, :, None] * sb[:, None, :]).reshape(seq, seq)
    sinm = (sa[:, :, None] * cb[:, None, :] + ca[:, :, None] * sb[:, None, :]).reshape(seq, seq)
    sign = jnp.where((jnp.arange(seq) & 1) == 0, 1.0, -1.0).astype(f32)
    ri = jnp.arange(seq)[:, None]
    ci = jnp.arange(seq)[None, :]
    fim = jnp.where(ri == 0, sign[None, :], -sinm)
    iim = jnp.where(ci == 0, sign[:, None], -sinm)
    return cosm.astype(bf16), fim.astype(bf16), iim.astype(bf16)


@functools.lru_cache(maxsize=None)
def _hy_feats(seq):
    t = np.linspace(0.0, 1.0, seq, dtype=np.float32)[:, None]
    w = (2.0 * math.pi * np.arange(seq, dtype=np.float32)[:, None] / seq).astype(np.float32)
    fr = np.linspace(1e-4, HY_BANDS - 1.0, HY_BANDS, dtype=np.float32)[None, :]
    feats = np.concatenate([t, np.cos(fr * w), -np.sin(fr * w)], axis=-1).astype(np.float32)
    return np.pad(feats, ((0, 0), (0, LANES - feats.shape[1])))


def _layer_group(x, mod, layer, p, state, *, nbatch, seq, rows, tm, row_of, want_state, alpha):
    tm_dense = min(DENSE_TM, nbatch * seq)
    proj, u_t = _in_proj(x, mod, layer, row_of(tm_dense), p["w_in"], tm_dense, seq)

    o_gla, s_gla = _linattn(proj, OFF_GLA, (p["gla_wg"], p["gla_bg"], p["gla_nw"]),
                            None if state is None else state["gla"], layer,
                            mode="gla", nbatch=nbatch, seq=seq, want_state=want_state)
    o_ret, s_ret = _linattn(proj, OFF_RET, (p["ret_dexp"],),
                            None if state is None else state["ret"], layer,
                            mode="ret", nbatch=nbatch, seq=seq, want_state=want_state)

    u_t = u_t.reshape(seq, nbatch, GRP)
    h0 =None if state is None else (state["s5_re"][layer], state["s5_im"][layer])
    y_t, hf_re, hf_im = _s5_scan(u_t, p["s5_wb"], p["s5_lam"], p["s5_cre"], p["s5_cim"], p["s5_d"], h0,
                                 want_state=want_state)
    y_s5 = y_t.reshape(seq, nbatch * GRP)
    tpb = seq // tm
    s5_spec = pl.BlockSpec((tm, GRP), lambda i: (i % tpb, i // tpb))

    fre, fim, iim = _dft_mats(seq)
    fa, fb, fd = _hy_filter(jnp.asarray(_hy_feats(seq)), p["hy_w1"], p["hy_b1"], p["hy_w2"], p["hy_b2"],
                            p["hy_freq"], p["hy_w3"], p["hy_decay"], fre, fim)
    seg = seq // rows
    hy_cb = OFF_HY // GRP
    y1 = _longconv(proj, hy_cb, proj, hy_cb + 1, p["hy_cw"], p["hy_cb"], 0, 1, fre, fim, iim, fa, fb, fd,
                   p["hy_d"], 0, nbatch=nbatch, seq=seq, seg=seg, conv_yin=True)
    o_hy = _longconv(y1, 0, proj, hy_cb + 2, p["hy_cw"], p["hy_cb"], 0, 2, fre, fim, iim, fa, fb, fd,
                     p["hy_d"], 1, nbatch=nbatch, seq=seq, seg=seg, conv_yin=False)

    x = _out_proj((o_gla, o_ret, y_s5, o_hy), s5_spec, p["s5_glu_w"], p["s5_glu_b"], x, mod, layer, row_of(tm), p["w_out"],
                  p["ln1_g"], p["ln1_b"], tm, alpha)
    x = _mlp(x, mod, layer, row_of(tm_dense), p["w_up"], p["w_down"], p["ln2_g"], p["ln2_b"], tm_dense, MLP_TF, alpha)
    return x, (s_gla, s_ret, hf_re, hf_im)


def kernel(x_prompt, x_sample, state_gla, state_ret, state_s5_re, state_s5_im, c, c_ctx, ada_w, ada_b, w_in, gla_w_gate, gla_b_gate, gla_norm_w, ret_decay_exp, s5_a_re, s5_a_im, s5_log_step, s5_b_re, s5_b_im, s5_c_re, s5_c_im, s5_d, s5_glu_w, s5_glu_b, hy_conv_w, hy_conv_b, hy_f_w1, hy_f_b1, hy_f_w2, hy_f_b2, hy_f_freq, hy_f_w3, hy_decay, hy_d, w_out, ln1_g, ln1_b, w_up, w_down, ln2_g, ln2_b):
    bp, lp, d = x_prompt.shape
    bs, ls, _ = x_sample.shape
    depth = w_in.shape[0]
    alpha = (2 * depth) ** 0.25
    ngroup = s5_a_re.shape[2]
    ntile = ngroup // S5_GPT

    nrow = -(-(1 + bs) // SUBLANES) * SUBLANES
    c_rows = jnp.concatenate([c_ctx[None, :], c, jnp.zeros((nrow - 1 - bs, d), f32)], axis=0)
    mod = _ada(c_rows, ada_w, ada_b).reshape(depth, nrow, 6, d)

    tr_state = lambda s: jnp.swapaxes(s, -1, -2)
    s5_state = lambda s: jnp.transpose(s.reshape(bs, depth, 2, ntile, S5_SW), (1, 2, 3, 0, 4))
    lat_state = dict(gla=tr_state(state_gla), ret=tr_state(state_ret),
                     s5_re=s5_state(state_s5_re), s5_im=s5_state(state_s5_im))

    tm_ctx = min(lp, DENSE_TM)
    tm_lat = min(ls, DENSE_TM)
    yp = x_prompt.reshape(bp * lp, d)
    ys = x_sample.reshape(bs * ls, d)
    outs = []
    for l in range(depth):
        lre, lim, bbre, bbim = _s5_prep(s5_a_re[l], s5_a_im[l], s5_log_step[l], s5_b_re[l], s5_b_im[l])
        wb, cre, cim = _s5_block_weights(bbre, bbim, s5_c_re[l], s5_c_im[l])
        lam = jnp.stack([lre.reshape(2, ntile, S5_SW), lim.reshape(2, ntile, S5_SW)], axis=2)
        wg, bg, nw = _gla_gate_params(gla_w_gate[l], gla_b_gate[l], gla_norm_w[l])
        w1p = jnp.pad(hy_f_w1[l], ((0, LANES - hy_f_w1.shape[1]), (0, 0)))
        p = dict(
            w_in=_perm_w_in(w_in[l]),
            gla_wg=wg, gla_bg=bg, gla_nw=nw,
            ret_dexp=jnp.broadcast_to(ret_decay_exp[l][:, None, :, None], (2, SUBLANES, HEADS, LANES)).reshape(2, SUBLANES, GRP),
            s5_wb=wb, s5_lam=lam, s5_cre=cre, s5_cim=cim, s5_d=s5_d[l][None, :],
            s5_glu_w=s5_glu_w[l].astype(bf16), s5_glu_b=s5_glu_b[l][None, :],
            hy_cw=hy_conv_w[l], hy_cb=hy_conv_b[l][None, :],
            hy_w1=w1p, hy_b1=hy_f_b1[l][None, :], hy_w2=hy_f_w2[l], hy_b2=hy_f_b2[l][None, :],
            hy_freq=hy_f_freq[l][None, :], hy_w3=hy_f_w3[l], hy_decay=hy_decay[l][None, :],
            hy_d=hy_d[l][:, None, :],
            w_out=w_out[l].astype(bf16), ln1_g=ln1_g[l][None, :], ln1_b=ln1_b[l][None, :],
            w_up=w_up[l].astype(bf16), w_down=w_down[l].astype(bf16),
            ln2_g=ln2_g[l][None, :], ln2_b=ln2_b[l][None, :],
        )
        yp, st = _layer_group(yp, mod, l, p, None, nbatch=bp, seq=lp, rows=1, tm=tm_ctx,
                              row_of=lambda tile: (lambda i: 0), want_state=True, alpha=alpha)
        outs.append(st)
        ys, _ = _layer_group(ys, mod, l, p, lat_state, nbatch=bs, seq=ls, rows=ls // GRID_W, tm=tm_lat,
                             row_of=lambda tile: (lambda i: 1 + i // (ls // tile)), want_state=False, alpha=alpha)

    new_gla = jnp.stack([jnp.swapaxes(o[0], -1, -2) for o in outs], axis=1)
    new_ret = jnp.stack([jnp.swapaxes(o[1], -1, -2) for o in outs], axis=1)
    unpack = lambda h: jnp.transpose(h, (2, 0, 1, 3)).reshape(bp, 2, ngroup, S5_STATE)
    new_re = jnp.stack([unpack(o[2]) for o in outs], axis=1)
    new_im = jnp.stack([unpack(o[3]) for o in outs], axis=1)
    return (yp.reshape(bp, lp, d), ys.reshape(bs, ls, d), new_gla, new_ret, new_re, new_im)
```

```python
import functools
import math

import numpy as np
import jax
import jax.numpy as jnp
from jax import lax
from jax.experimental import pallas as pl
from jax.experimental.pallas import tpu as pltpu

f32 = jnp.float32
bf16 = jnp.bfloat16

GRID_W = 64
HEADS = 4
DK = 64
DV = 128
GLA_RANK = 16
GLA_TAU = 16.0
S5_GROUP = 16
S5_STATE = 64
HY_BANDS = 16
HY_SHORT = 3
CHUNK = 64
RET_CHUNK = 128
LN_EPS = 1e-5
NORM_EPS = 1e-6

LANES = 128
SUBLANES = 8
VMEM_LIMIT = 52 * 1024 * 1024

GRP = HEADS * DV
OFF_GLA = 0
OFF_RET = 3 * GRP
OFF_S5 = 6 * GRP
OFF_HY = 7 * GRP
OFF_LR = 10 * GRP
NP_IN = 5376
TN_IN = 1792
DENSE_TM = 512
MLP_TF = 1024


def _cparams(sem, vmem=VMEM_LIMIT):
    return pltpu.CompilerParams(dimension_semantics=sem, vmem_limit_bytes=vmem)


def _ln(x):
    mu = jnp.mean(x, axis=-1, keepdims=True)
    xc = x - mu
    var = jnp.mean(xc * xc, axis=-1, keepdims=True)
    return xc * lax.rsqrt(var + LN_EPS)


def _silu(x):
    return x * jax.nn.sigmoid(x)


def _bdot(a, b):
    return jnp.dot(a.astype(bf16), b.astype(bf16), preferred_element_type=f32)


def _bdot_nt(a, b):
    return lax.dot_general(a.astype(bf16), b.astype(bf16), (((1,), (1,)), ((), ())), preferred_element_type=f32)


def _bdot_tn(a, b):
    return lax.dot_general(a.astype(bf16), b.astype(bf16), (((0,), (0,)), ((), ())), preferred_element_type=f32)


def _dot_exact_lhs(a_bf, x):
    hi = x.astype(bf16)
    lo = (x - hi.astype(f32)).astype(bf16)
    d = lambda b: jnp.dot(a_bf, b, preferred_element_type=f32)
    return d(hi) + d(lo)


def _dot3(a, b):
    ah = a.astype(bf16)
    al = (a - ah.astype(f32)).astype(bf16)
    bh = b.astype(bf16)
    bl = (b - bh.astype(f32)).astype(bf16)
    d = lambda p, q: jnp.dot(p, q, preferred_element_type=f32)
    return d(ah, bh) + (d(al, bh) + d(ah, bl))


def _ada_kernel(c_ref, w_ref, b_ref, o_ref):
    s = _silu(c_ref[...])
    o_ref[...] = _bdot(s, w_ref[...]) + b_ref[...]


def _ada(c_rows, ada_w, ada_b):
    depth, d, n = ada_w.shape
    r = c_rows.shape[0]
    tn = 1024
    return pl.pallas_call(
        _ada_kernel,
        out_shape=jax.ShapeDtypeStruct((depth, r, n), f32),
        grid=(depth, n // tn),
        in_specs=[pl.BlockSpec((r, d), lambda l, j: (0, 0)),
                  pl.BlockSpec((None, d, tn), lambda l, j: (l, 0, j)),
                  pl.BlockSpec((None, 1, tn), lambda l, j: (l, 0, j))],
        out_specs=pl.BlockSpec((None, r, tn), lambda l, j: (l, 0, j)),
        compiler_params=_cparams(("parallel", "parallel")),
        name="ada",
    )(c_rows, ada_w, ada_b.reshape(depth, 1, n))


def _in_kernel(x_ref, mod_ref, w_ref, o_ref, u_ref, hy_ref, h_ref, *, nsub, sub):
    j = pl.program_id(1)

    @pl.when(j == 0)
    def _():
        y = _ln(x_ref[...])
        h_ref[...] = (y * (1.0 + mod_ref[1:2, :]) + mod_ref[0:1, :]).astype(bf16)

    cols = pl.ds(pl.multiple_of(j * TN_IN, LANES), TN_IN)
    o_ref[...] = jnp.dot(h_ref[...], w_ref[:, cols], preferred_element_type=f32)

    @pl.when(j == OFF_S5 // TN_IN)
    def _():
        c0 = OFF_S5 % TN_IN
        for k in range(nsub):
            u_ref[:, k * GRP:(k + 1) * GRP] = o_ref[k * sub:(k + 1) * sub, c0:c0 + GRP]

    @pl.when(j == OFF_HY // TN_IN)
    def _():
        c0 = OFF_HY % TN_IN
        hy_ref[...] = o_ref[:, c0:c0 + 3 * GRP]


def _in_proj(x, mod, layer, row_fn, w_perm, tm, seq):
    t, d = x.shape
    assert OFF_S5 // TN_IN == (OFF_S5 + GRP - 1) // TN_IN
    assert OFF_HY // TN_IN == (OFF_HY + 3 * GRP - 1) // TN_IN
    sub = min(tm, seq)
    nsub = tm // sub
    tps = seq // sub
    return pl.pallas_call(
        functools.partial(_in_kernel, nsub=nsub, sub=sub),
        out_shape=[jax.ShapeDtypeStruct((t, NP_IN), f32), jax.ShapeDtypeStruct((seq, (t // seq) * GRP), f32),
                   jax.ShapeDtypeStruct((t, 3 * GRP), f32)],
        grid=(t // tm, NP_IN // TN_IN),
        in_specs=[pl.BlockSpec((tm, d), lambda i, j: (i, 0)),
                  pl.BlockSpec((None, None, 6, d), lambda i, j: (layer, row_fn(i), 0, 0)),
                  pl.BlockSpec((d, NP_IN), lambda i, j: (0, 0), pipeline_mode=pl.Buffered(1))],
        out_specs=[pl.BlockSpec((tm, TN_IN), lambda i, j: (i, j)),
                   pl.BlockSpec((sub, nsub * GRP), lambda i, j: (i % tps, i // tps)),
                   pl.BlockSpec((tm, 3 * GRP), lambda i, j: (i, 0))],
        scratch_shapes=[pltpu.VMEM((tm, d), bf16)],
        compiler_params=_cparams(("parallel", "arbitrary")),
        name="in_proj",
    )(x, mod, w_perm)


LA_TAIL = 256


def _linattn_kernel(*refs, mode, seq, has_s0, want_state):
    it = iter(refs)
    qk_ref, v_ref, g_ref = next(it), next(it), next(it)
    if mode == "gla":
        lr_ref, wg_ref, bg_ref, nw_ref = next(it), next(it), next(it), next(it)
    else:
        dexp_ref = next(it)
    s0_ref = next(it) if has_s0 else None
    o_ref = next(it)
    sf_ref = next(it) if want_state else None
    ob_ref, st_ref = next(it), next(it)
    if mode == "gla":
        qkd_ref, ku_ref, dec_ref = next(it), next(it), next(it)

    c = CHUNK if mode == "gla" else RET_CHUNK
    nchunk = seq // c
    scale = DK ** -0.5
    row = lax.broadcasted_iota(jnp.int32, (c, c), 0)
    col = lax.broadcasted_iota(jnp.int32, (c, c), 1)
    is_k = (lax.broadcasted_iota(jnp.int32, (1, GRP), 1) & DK) != 0
    hq = lambda a, h: a[:, h * LANES:h * LANES + DK]
    hk = lambda a, h: a[:, h * LANES + DK:(h + 1) * LANES]
    hv = lambda h: slice(h * DV, (h + 1) * DV)

    if has_s0:
        st_ref[...] = s0_ref[...]
    else:
        st_ref[...] = jnp.zeros_like(st_ref)

    per_dir = []
    for d in (0, 1):
        causal = (row >= col) if d == 0 else (row <= col)
        if mode == "gla":
            tri = jnp.where(causal, 1.0, 0.0).astype(bf16)
            sgn = jnp.where(is_k, -1.0, 1.0)
            qsc = jnp.where(is_k, 1.0, scale)
            per_dir.append((causal, tri, sgn, qsc))
        else:
            lgr = jnp.log1p(-jnp.exp2(-dexp_ref[d]))[0:1, :]
            rowg = lax.broadcasted_iota(jnp.int32, (c, GRP), 0)
            pos = rowg if d == 0 else (c - 1 - rowg)
            pw = jnp.where(is_k, c - 1 - pos, pos + 1).astype(f32)
            wqk = jnp.exp(lgr * pw) * jnp.where(is_k, scale, 1.0)
            dist = ((row - col) if d == 0 else (col - row)).astype(f32)
            masks = [jnp.where(causal, jnp.exp(jnp.broadcast_to(lgr[:, h * LANES:h * LANES + c], (c, c)) * dist), 0.0) * scale
                     for h in range(HEADS)]
            dec = jnp.exp(lgr * float(c))
            per_dir.append((wqk, masks, dec))

    if mode == "gla":
        def pre(n, carry):
            rows = pl.ds(pl.multiple_of(n * c, c), c)
            qk = qk_ref[rows, :]
            lrc = lr_ref[rows, :]
            logits = [_bdot(lrc, wg_ref[d]) + bg_ref[d] for d in (0, 1)]
            la = [(jnp.minimum(x, 0.0) - jnp.log(1.0 + jnp.exp(-jnp.abs(x)))) * (1.0 / GLA_TAU) for x in logits]
            b = [_dot_exact_lhs(per_dir[d][1], la[d]) for d in (0, 1)]
            for d in (0, 1):
                _, _, sgn, qsc = per_dir[d]
                last = c - 1 if d == 0 else 0
                btot = b[d][last:last + 1, :]
                qkd_ref[d, rows, :] = (qk * (jnp.exp(b[d] * sgn) * qsc)).astype(bf16)
                ku_ref[d, rows, :] = (qk * jnp.exp(btot - b[d])).astype(bf16)
                dec_ref[d, n] = jnp.exp(btot)
            return carry

        lax.fori_loop(0, nchunk, pre, 0, unroll=2)

    def body(i, carry):
        rows, qka, qko, ku, dec = [], [], [], [], []
        for d in (0, 1):
            n = i if d == 0 else nchunk - 1 - i
            rows.append(pl.ds(pl.multiple_of(n * c, c), c))
            if mode == "gla":
                qka.append(qkd_ref[d, rows[d], :])
                qko.append(qka[d])
                ku.append(ku_ref[d, rows[d], :])
                dec.append(dec_ref[d, n])
            else:
                wqk, _, dcy = per_dir[d]
                qka.append(qk_ref[rows[d], :])
                qko.append(qka[d] * wqk)
                ku.append(qko[d])
                dec.append(dcy)
        chains = [(d, h) for d in (0, 1) for h in range(HEADS)]
        v = {(d, h): v_ref[rows[d], hv(h)] for d, h in chains}
        st = {(d, h): st_ref[d, h] for d, h in chains}
        att = {(d, h): _bdot_nt(hq(qka[d], h), hk(qka[d], h)) for d, h in chains}
        oi = {(d, h): _bdot_nt(hq(qko[d], h), st[d, h]) for d, h in chains}
        up = {(d, h): _bdot_tn(v[d, h], hk(ku[d], h)) for d, h in chains}
        for d, h in chains:
            if mode == "gla":
                a = jnp.where(per_dir[d][0], att[d, h], 0.0)
            else:
                a = att[d, h] * per_dir[d][1][h]
            o = _bdot(a, v[d, h]) + oi[d, h]
            st_ref[d, h] = hq(dec[d], h) * st[d, h] + up[d, h]
            if d == 0:
                o_ref[rows[d], hv(h)] = o
            else:
                ob_ref[rows[d], hv(h)] = o
        return carry

    lax.fori_loop(0, nchunk, body, 0, unroll=4 if mode == "gla" else 2)
    if want_state:
        sf_ref[...] = st_ref[...]

    tl = min(LA_TAIL, seq)

    def tail(i, carry):
        rows = pl.ds(pl.multiple_of(i * tl, tl), tl)
        for h in range(HEADS):
            o = o_ref[rows, hv(h)] + ob_ref[rows, hv(h)]
            if mode == "gla":
                o = o * lax.rsqrt(jnp.mean(o * o, axis=-1, keepdims=True) + NORM_EPS) * nw_ref[:, hv(h)]
            else:
                o = _ln(o)
            o_ref[rows, hv(h)] = o * _silu(g_ref[rows, hv(h)])
        return carry

    lax.fori_loop(0, seq // tl, tail, 0)


def _linattn(proj, col0, extra, s0t, layer, *, mode, nbatch, seq, want_state):
    cb = col0 // GRP
    full = lambda a: pl.BlockSpec(a.shape, lambda b: (0,) * a.ndim)
    in_specs = [pl.BlockSpec((seq, GRP), lambda b: (b, cb)),
                pl.BlockSpec((seq, GRP), lambda b: (b, cb + 1)),
                pl.BlockSpec((seq, GRP), lambda b: (b, cb + 2), pipeline_mode=pl.Buffered(1))]
    args = [proj, proj, proj]
    scratch = [pltpu.VMEM((seq, GRP), f32), pltpu.VMEM((2, HEADS, DV, DK), f32)]
    if mode == "gla":
        scratch += [pltpu.VMEM((2, seq, GRP), bf16), pltpu.VMEM((2, seq, GRP), bf16),
                    pltpu.VMEM((2, seq // CHUNK, 1, GRP), f32)]
    if mode == "gla":
        in_specs += [pl.BlockSpec((seq, LANES), lambda b: (b, OFF_LR // LANES))] + [full(a) for a in extra]
        args += [proj, *extra]
    else:
        in_specs += [full(a) for a in extra]
        args += list(extra)
    has_s0 = s0t is not None
    if has_s0:
        in_specs += [pl.BlockSpec((None, None, 2, HEADS, DV, DK), lambda b: (b, layer, 0, 0, 0, 0))]
        args += [s0t]
    out_shape = [jax.ShapeDtypeStruct((nbatch * seq, GRP), f32)]
    out_specs = [pl.BlockSpec((seq, GRP), lambda b: (b, 0))]
    if want_state:
        out_shape += [jax.ShapeDtypeStruct((nbatch, 2, HEADS, DV, DK), f32)]
        out_specs += [pl.BlockSpec((None, 2, HEADS, DV, DK), lambda b: (b, 0, 0, 0, 0))]
    res = pl.pallas_call(
        functools.partial(_linattn_kernel, mode=mode, seq=seq, has_s0=has_s0, want_state=want_state),
        out_shape=out_shape,
        grid=(nbatch,),
        in_specs=in_specs,
        out_specs=out_specs,
        scratch_shapes=scratch,
        compiler_params=_cparams(("parallel",)),
        name="linattn_" + mode,
    )(*args)
    return (res[0], res[1]) if want_state else (res[0], None)


def _s5_prep_kernel(are_ref, aim_ref, lstep_ref, bre_ref, bim_ref, lre_ref, lim_ref, bbre_ref, bbim_ref):
    ar, ai = are_ref[...], aim_ref[...]
    st = jnp.exp(lstep_ref[...])
    mag = jnp.exp(ar * st)
    lr = mag * jnp.cos(ai * st)
    li = mag * jnp.sin(ai * st)
    nr, ni = lr - 1.0, li
    den = ar * ar + ai * ai
    kr = (nr * ar + ni * ai) / den
    ki = (ni * ar - nr * ai) / den
    br, bi = bre_ref[...], bim_ref[...]
    lre_ref[...] = lr
    lim_ref[...] = li
    bbre_ref[...] = kr * br - ki * bi
    bbim_ref[...] = kr * bi + ki * br


def _s5_prep(a_re, a_im, log_step, b_re, b_im):
    _, g, p = a_re.shape
    hch = b_re.shape[-1]
    rows = 2 * g * hch
    bc = lambda a: jnp.broadcast_to(a[:, :, None, :], (2, g, hch, p)).reshape(rows, p)
    tr = lambda b: jnp.transpose(b, (0, 1, 3, 2)).reshape(rows, p)
    ls = jnp.broadcast_to(log_step[:, :, None, None], (2, g, hch, p)).reshape(rows, p)
    spec = pl.BlockSpec((rows, p), lambda: (0, 0))
    lre, lim, bbre, bbim = pl.pallas_call(
        _s5_prep_kernel,
        out_shape=[jax.ShapeDtypeStruct((rows, p), f32)] * 4,
        in_specs=[spec] * 5,
        out_specs=[spec] * 4,
        name="s5_prep",
    )(bc(a_re), bc(a_im), ls, tr(b_re), tr(b_im))
    r4 = lambda a: a.reshape(2, g, hch, p)
    return r4(lre)[:, :, 0, :], r4(lim)[:, :, 0, :], r4(bbre), r4(bbim)


S5_TC = 64
S5_GPT = LANES // S5_GROUP
S5_SW = S5_GPT * S5_STATE


def _s5_scan_kernel(*refs, seq, nb, has_h0, want_state):
    it = iter(refs)
    u_ref, wb_ref, lam_ref, cre_ref, cim_ref, dsk_ref = (next(it) for _ in range(6))
    h0re_ref = next(it) if has_h0 else None
    h0im_ref = next(it) if has_h0 else None
    y_ref = next(it)
    hfre_ref = next(it) if want_state else None
    hfim_ref = next(it) if want_state else None
    bu_ref, hs_ref = next(it), next(it)

    tc = S5_TC
    nchunk = seq // tc
    sw = S5_SW

    def skip(i, carry):
        rows = pl.ds(pl.multiple_of(i * tc, tc), tc)
        y_ref[rows] = u_ref[rows] * dsk_ref[...]
        return carry

    lax.fori_loop(0, nchunk, skip, 0)

    lam = [(jnp.broadcast_to(lam_ref[d, 0:1, :], (nb, sw)), jnp.broadcast_to(lam_ref[d, 1:2, :], (nb, sw)))
           for d in (0, 1)]

    def chunk(i, carry):
        h = [list(carry[0:2]), list(carry[2:4])]
        t0 = [pl.multiple_of(i * tc, tc), pl.multiple_of((nchunk - 1 - i) * tc, tc)]
        for d in (0, 1):
            ub = u_ref[pl.ds(t0[d], tc)]
            bu_ref[d] = _bdot(ub.reshape(tc * nb, LANES), wb_ref[d])
        for s in range(tc):
            for d in (0, 1):
                t = s if d == 0 else tc - 1 - s
                rs = slice(t * nb, (t + 1) * nb)
                lam_r, lam_i = lam[d]
                hre, him = h[d]
                nre = lam_r * hre - lam_i * him + bu_ref[d, rs, 0:sw]
                nim = lam_r * him + lam_i * hre + bu_ref[d, rs, sw:2 * sw]
                h[d] = [nre, nim]
                hs_ref[d, rs, 0:sw] = nre
                hs_ref[d, rs, sw:2 * sw] = nim
        for d in (0, 1):
            y = _bdot(hs_ref[d, :, 0:sw], cre_ref[d]) - _bdot(hs_ref[d, :, sw:2 * sw], cim_ref[d])
            rows = pl.ds(t0[d], tc)
            y_ref[rows] = y_ref[rows] + y.reshape(tc, nb, LANES)
        return (*h[0], *h[1])

    if has_h0:
        h0 = (h0re_ref[0], h0im_ref[0], h0re_ref[1], h0im_ref[1])
    else:
        h0 = (jnp.zeros((nb, sw), f32),) * 4
    hfin = lax.fori_loop(0, nchunk, chunk, h0)
    if want_state:
        for d in (0, 1):
            hfre_ref[d] = hfin[2 * d]
            hfim_ref[d] = hfin[2 * d + 1]


def _s5_scan(u_t, wb, lam, cre, cim, dsk, h0, *, want_state):
    seq, nb, ch = u_t.shape
    ntile = ch // LANES
    sw = S5_SW
    in_specs = [pl.BlockSpec((seq, nb, LANES), lambda j: (0, 0, j)),
                pl.BlockSpec((2, None, LANES, 2 * sw), lambda j: (0, j, 0, 0)),
                pl.BlockSpec((2, None, 2, sw), lambda j: (0, j, 0, 0)),
                pl.BlockSpec((2, None, sw, LANES), lambda j: (0, j, 0, 0)),
                pl.BlockSpec((2, None, sw, LANES), lambda j: (0, j, 0, 0)),
                pl.BlockSpec((1, LANES), lambda j: (0, j))]
    args = [u_t, wb, lam, cre, cim, dsk]
    has_h0 = h0 is not None
    st_spec = pl.BlockSpec((2, None, nb, sw), lambda j: (0, j, 0, 0))
    if has_h0:
        in_specs += [st_spec, st_spec]
        args += list(h0)
    out_shape = [jax.ShapeDtypeStruct((seq, nb, ch), f32)]
    out_specs = [pl.BlockSpec((seq, nb, LANES), lambda j: (0, 0, j))]
    if want_state:
        out_shape += [jax.ShapeDtypeStruct((2, ntile, nb, sw), f32)] * 2
        out_specs += [st_spec, st_spec]
    res = pl.pallas_call(
        functools.partial(_s5_scan_kernel, seq=seq, nb=nb, has_h0=has_h0, want_state=want_state),
        out_shape=out_shape,
        grid=(ntile,),
        in_specs=in_specs,
        out_specs=out_specs,
        scratch_shapes=[pltpu.VMEM((2, S5_TC * nb, 2 * sw), f32), pltpu.VMEM((2, S5_TC * nb, 2 * sw), f32)],
        compiler_params=_cparams(("parallel",)),
        name="s5_scan",
    )(*args)
    return (res[0], res[1], res[2]) if want_state else (res[0], None, None)


HY_TK = 256
HY_CT = 256


def _pk_mul(xr, xi, a, b, d):
    return xr * a - xi * b, xr * b + xi * d


def _hy_filter_kernel(fte_ref, fto_ref, w1_ref, b1_ref, w2_ref, b2_ref, fq_ref, w3f_ref, w3b_ref, dcf_ref, dcb_ref,
                      fre_ref, fim_ref, cph_ref, sph_ref,
                      fea_ref, feb_ref, foa_ref, fob_ref, ga_ref, gb_ref, ny_ref, xs_ref, *, lh, tk):
    j = pl.program_id(1)
    ct = HY_CT

    @pl.when(j == 0)
    def _():
        fq = fq_ref[...]

        def hidden(feat):
            h = jnp.sin(fq * (_dot3(feat, w1_ref[...]) + b1_ref[...]))
            return jnp.sin(fq * (_dot3(h, w2_ref[...]) + b2_ref[...])), feat[:, 0:1]

        he, te = hidden(fte_ref[...])
        ho, to = hidden(fto_ref[...])
        raw = lambda h, t, w3_ref, dc_ref: _dot3(h, w3_ref[...]) * jnp.exp(-t * jnp.abs(dc_ref[...]))
        ffe, ffo = raw(he, te, w3f_ref, dcf_ref), raw(ho, to, w3f_ref, dcf_ref)
        fbe, fbo = raw(he, te, w3b_ref, dcb_ref), raw(ho, to, w3b_ref, dcb_ref)
        sf = jnp.sum(jnp.abs(ffe), axis=0, keepdims=True) + jnp.sum(jnp.abs(ffo), axis=0, keepdims=True)
        sb = jnp.sum(jnp.abs(fbe), axis=0, keepdims=True) + jnp.sum(jnp.abs(fbo), axis=0, keepdims=True)
        rowi = lax.broadcasted_iota(jnp.int32, (lh, ct), 0)
        xs_ref[:, 0 * ct:1 * ct] = (ffe / sf).astype(bf16)
        xs_ref[:, 1 * ct:2 * ct] = jnp.where(rowi == 0, 0.0, fbe / sb).astype(bf16)
        xs_ref[:, 2 * ct:3 * ct] = (ffo / sf).astype(bf16)
        xs_ref[:, 3 * ct:4 * ct] = (fbo / sb).astype(bf16)

    xs = xs_ref[...]
    r = jnp.dot(fre_ref[...], xs, preferred_element_type=f32)
    i = jnp.dot(fim_ref[...], xs, preferred_element_type=f32)
    r1, r2, r3, r4 = (r[:, q * ct:(q + 1) * ct] for q in range(4))
    i1, i2, i3, i4 = (i[:, q * ct:(q + 1) * ct] for q in range(4))
    c, s = cph_ref[...], sph_ref[...]
    fe_re, fe_im = r1 + r2, i1 - i2
    fo_re = r3 + c * r4 + s * i4
    fo_im = i3 + s * r4 - c * i4
    g_re = c * fo_re + s * fo_im
    g_im = c * fo_im - s * fo_re
    scl = 1.0 / lh
    first = jnp.logical_and(lax.broadcasted_iota(jnp.int32, (tk, ct), 0) == 0, j == 0)
    for a_ref, b_ref, re, im in ((fea_ref, feb_ref, fe_re, fe_im), (foa_ref, fob_ref, fo_re, fo_im),
                                 (ga_ref, gb_ref, g_re, g_im)):
        a_ref[...] = jnp.where(first, 0.5 * scl, scl) * re
        b_ref[...] = jnp.where(first, 0.0, scl * im)

    @pl.when(j == 0)
    def _():
        fo_ny = i3[0:1, :] - i4[0:1, :]
        ny_ref[...] = jnp.zeros_like(ny_ref)
        ny_ref[0:1, :] = (0.5 * scl) * (i1[0:1, :] + i2[0:1, :])
        ny_ref[1:2, :] = (0.5 * scl) * fo_ny
        ny_ref[2:3, :] = (-0.5 * scl) * fo_ny


def _hy_filter(feat_e, feat_o, w1p, b1, w2, b2, fq, w3, dc, fre, fim, cph, sph):
    lh = feat_e.shape[0]
    ncol = w3.shape[1] // 2
    nct = ncol // HY_CT
    tk = min(HY_TK, lh)
    full = lambda a: pl.BlockSpec(a.shape, lambda c, j: (0,) * a.ndim)
    out_spec = pl.BlockSpec((tk, HY_CT), lambda c, j: (j, c))
    return pl.pallas_call(
        functools.partial(_hy_filter_kernel, lh=lh, tk=tk),
        out_shape=[jax.ShapeDtypeStruct((lh, ncol), f32)] * 6 + [jax.ShapeDtypeStruct((SUBLANES, ncol), f32)],
        grid=(nct, lh // tk),
        in_specs=[full(feat_e), full(feat_o), full(w1p), full(b1), full(w2), full(b2), full(fq),
                  pl.BlockSpec((w3.shape[0], HY_CT), lambda c, j: (0, c)),
                  pl.BlockSpec((w3.shape[0], HY_CT), lambda c, j: (0, nct + c)),
                  pl.BlockSpec((1, HY_CT), lambda c, j: (0, c)),
                  pl.BlockSpec((1, HY_CT), lambda c, j: (0, nct + c)),
                  pl.BlockSpec((tk, lh), lambda c, j: (j, 0)),
                  pl.BlockSpec((tk, lh), lambda c, j: (j, 0)),
                  pl.BlockSpec((tk, HY_CT), lambda c, j: (j, 0)),
                  pl.BlockSpec((tk, HY_CT), lambda c, j: (j, 0))],
        out_specs=[out_spec] * 6 + [pl.BlockSpec((SUBLANES, HY_CT), lambda c, j: (0, c))],
        scratch_shapes=[pltpu.VMEM((lh, 4 * HY_CT), bf16)],
        compiler_params=_cparams(("parallel", "arbitrary")),
        name="hy_filter",
    )(feat_e, feat_o, w1p, b1, w2, b2, fq, w3, w3, dc, dc, fre, fim, cph, sph)


def _short_conv2(e, o, w_ref, b_ref, seg2):
    rows = e.shape[0]
    r = lax.broadcasted_iota(jnp.int32, e.shape, 0) & (seg2 - 1)
    o_prev = jnp.where(r == 0, 0.0, pltpu.roll(o, 1, 0))
    e_next = jnp.where(r == seg2 - 1, 0.0, pltpu.roll(e, rows - 1, 0))
    w0, w1, w2, b = w_ref[0:1, :], w_ref[1:2, :], w_ref[2:3, :], b_ref[...]
    return o_prev * w0 + e * w1 + o * w2 + b, e * w0 + o * w1 + e_next * w2 + b


def _longconv_kernel(ye_ref, yo_ref, ge_ref, go_ref, cwy_ref, cby_ref, cwg_ref, cbg_ref, fre_ref, fim_ref, iim_ref,
                     fea_ref, feb_ref, foa_ref, fob_ref, ga_ref, gb_ref, ny_ref, dsk_ref,
                     o_ref, y32_ref, ybf_ref, acc_ref, *, seg2, conv_yin, tk):
    j = pl.program_id(1)
    g = GRP

    @pl.when(j == 0)
    def _():
        e, o = ye_ref[...], yo_ref[...]
        if conv_yin:
            e, o = _short_conv2(e, o, cwy_ref, cby_ref, seg2)
        y32_ref[:, 0:g] = e
        y32_ref[:, g:2 * g] = o
        ybf_ref[:, 0:g] = e.astype(bf16)
        ybf_ref[:, g:2 * g] = o.astype(bf16)
        acc_ref[...] = jnp.zeros_like(acc_ref)

    freq = pl.ds(pl.multiple_of(j * tk, tk), tk)
    yb = ybf_ref[...]
    r = jnp.dot(fre_ref[freq, :], yb, preferred_element_type=f32)
    i = jnp.dot(fim_ref[freq, :], yb, preferred_element_type=f32)
    first = jnp.logical_and(lax.broadcasted_iota(jnp.int32, (tk, g), 0) == 0, j == 0)
    fe_a, fo_a, g_a = fea_ref[...], foa_ref[...], ga_ref[...]
    f_e = (fe_a, feb_ref[...], jnp.where(first, ny_ref[0:1, :], fe_a))
    f_o = (fo_a, fob_ref[...], jnp.where(first, ny_ref[1:2, :], fo_a))
    f_g = (g_a, gb_ref[...], jnp.where(first, ny_ref[2:3, :], g_a))
    er, ei, orr, oi = r[:, 0:g], i[:, 0:g], r[:, g:2 * g], i[:, g:2 * g]
    p1, p2 = _pk_mul(er, ei, *f_e), _pk_mul(orr, oi, *f_g)
    p3, p4 = _pk_mul(er, ei, *f_o), _pk_mul(orr, oi, *f_e)
    zr = jnp.concatenate([p1[0] + p2[0], p3[0] + p4[0]], axis=1).astype(bf16)
    zi = jnp.concatenate([p1[1] + p2[1], p3[1] + p4[1]], axis=1).astype(bf16)
    acc_ref[...] += (jnp.dot(fre_ref[:, freq], zr, preferred_element_type=f32)
                     + jnp.dot(iim_ref[:, freq], zi, preferred_element_type=f32))

    @pl.when(j == pl.num_programs(1) - 1)
    def _():
        ge, go = _short_conv2(ge_ref[...], go_ref[...], cwg_ref, cbg_ref, seg2)
        d = dsk_ref[...]
        o_ref[:, 0:g] = ge * (acc_ref[:, 0:g] + d * y32_ref[:, 0:g])
        o_ref[:, g:2 * g] = go * (acc_ref[:, g:2 * g] + d * y32_ref[:, g:2 * g])


def _longconv(yin, yin_cols, gate_arr, gate_cols, cw, cb, part_y, part_g, fre, fim, iim, filt, dsk, order,
              *, nbatch, lh, seg2, conv_yin):
    tk = min(HY_TK, lh)
    once = pl.Buffered(1)
    seq_spec = lambda cb_, mode=None: pl.BlockSpec((lh, GRP), lambda b, j: (b, cb_), pipeline_mode=mode)
    cw_spec = lambda p: pl.BlockSpec((HY_SHORT, GRP), lambda b, j: (0, p))
    cb_spec = lambda p: pl.BlockSpec((1, GRP), lambda b, j: (0, p))
    dft_spec = pl.BlockSpec((lh, lh), lambda b, j: (0, 0), pipeline_mode=once)
    filt_spec = pl.BlockSpec((tk, GRP), lambda b, j: (j, order))
    return pl.pallas_call(
        functools.partial(_longconv_kernel, seg2=seg2, conv_yin=conv_yin, tk=tk),
        out_shape=jax.ShapeDtypeStruct((nbatch * lh, 2 * GRP), f32),
        grid=(nbatch, lh // tk),
        in_specs=[seq_spec(yin_cols[0]), seq_spec(yin_cols[1]),
                  seq_spec(gate_cols[0], once), seq_spec(gate_cols[1], once),
                  cw_spec(part_y), cb_spec(part_y), cw_spec(part_g), cb_spec(part_g),
                  dft_spec, dft_spec, dft_spec,
                  filt_spec, filt_spec, filt_spec, filt_spec, filt_spec, filt_spec,
                  pl.BlockSpec((SUBLANES, GRP), lambda b, j: (0, order)),
                  pl.BlockSpec((None, 1, GRP), lambda b, j: (order, 0, 0))],
        out_specs=pl.BlockSpec((lh, 2 * GRP), lambda b, j: (b, 0)),
        scratch_shapes=[pltpu.VMEM((lh, 2 * GRP), f32), pltpu.VMEM((lh, 2 * GRP), bf16), pltpu.VMEM((lh, 2 * GRP), f32)],
        compiler_params=_cparams(("parallel", "arbitrary")),
        name="hy_longconv",
    )(yin, yin, gate_arr, gate_arr, cw, cb, cw, cb, fre, fim, iim, *filt, dsk)


def _out_kernel(m0_ref, m1_ref, y2_ref, m3_ref, gw_ref, gb_ref, x_ref, mod_ref, w_ref, g_ref, b_ref, o_ref, *, alpha):
    y = y2_ref[...]
    z = y * (0.5 * (1.0 + jnp.tanh(math.sqrt(2.0 / math.pi) * (y + 0.044715 * (y * y * y)))))
    m2 = z * jax.nn.sigmoid(_bdot(z, gw_ref[...]) + gb_ref[...])
    acc = _bdot(m0_ref[...], w_ref[0 * GRP:1 * GRP, :])
    acc += _bdot(m1_ref[...], w_ref[1 * GRP:2 * GRP, :])
    acc += _bdot(m2, w_ref[2 * GRP:3 * GRP, :])
    acc += _bdot(m3_ref[...], w_ref[3 * GRP:4 * GRP, :])
    r = alpha * x_ref[...] + mod_ref[2:3, :] * acc
    o_ref[...] = _ln(r) * g_ref[...] + b_ref[...]


def _out_proj(mixes, s5_spec, glu_w, glu_b, x, mod, layer, row_fn, w_out, g, b, tm, alpha):
    t, d = x.shape
    mspec = pl.BlockSpec((tm, GRP), lambda i: (i, 0))
    const = lambda a: pl.BlockSpec(a.shape, lambda i: (0,) * a.ndim)
    return pl.pallas_call(
        functools.partial(_out_kernel, alpha=alpha),
        out_shape=jax.ShapeDtypeStruct((t, d), f32),
        grid=(t // tm,),
        in_specs=[mspec, mspec, s5_spec, mspec, const(glu_w), const(glu_b),
                  pl.BlockSpec((tm, d), lambda i: (i, 0)),
                  pl.BlockSpec((None, None, 6, d), lambda i: (layer, row_fn(i), 0, 0)),
                  pl.BlockSpec((d, d), lambda i: (0, 0), pipeline_mode=pl.Buffered(1)),
                  pl.BlockSpec((1, d), lambda i: (0, 0)),
                  pl.BlockSpec((1, d), lambda i: (0, 0))],
        out_specs=pl.BlockSpec((tm, d), lambda i: (i, 0)),
        compiler_params=_cparams(("parallel",)),
        name="out_proj",
    )(*mixes, glu_w, glu_b, x, mod, w_out, g, b)


def _mlp_kernel(x_ref, mod_ref, wu_ref, wd_ref, g_ref, b_ref, o_ref, h_ref, acc_ref, *, alpha):
    j = pl.program_id(1)

    @pl.when(j == 0)
    def _():
        y = _ln(x_ref[...])
        h_ref[...] = (y * (1.0 + mod_ref[4:5, :]) + mod_ref[3:4, :]).astype(bf16)
        acc_ref[...] = jnp.zeros_like(acc_ref)

    u = jnp.maximum(jnp.dot(h_ref[...], wu_ref[...], preferred_element_type=f32), 0.0)
    acc_ref[...] += jnp.dot((u * u).astype(bf16), wd_ref[...], preferred_element_type=f32)

    @pl.when(j == pl.num_programs(1) - 1)
    def _():
        r = alpha * x_ref[...] + mod_ref[5:6, :] * acc_ref[...]
        o_ref[...] = _ln(r) * g_ref[...] + b_ref[...]


def _mlp(x, mod, layer, row_fn, w_up, w_down, g, b, tm, tf, alpha):
    t, d = x.shape
    dff = w_up.shape[1]
    return pl.pallas_call(
        functools.partial(_mlp_kernel, alpha=alpha),
        out_shape=jax.ShapeDtypeStruct((t, d), f32),
        grid=(t // tm, dff // tf),
        in_specs=[pl.BlockSpec((tm, d), lambda i, j: (i, 0)),
                  pl.BlockSpec((None, None, 6, d), lambda i, j: (layer, row_fn(i), 0, 0)),
                  pl.BlockSpec((d, tf), lambda i, j: (0, j)),
                  pl.BlockSpec((tf, d), lambda i, j: (j, 0)),
                  pl.BlockSpec((1, d), lambda i, j: (0, 0)),
                  pl.BlockSpec((1, d), lambda i, j: (0, 0))],
        out_specs=pl.BlockSpec((tm, d), lambda i, j: (i, 0)),
        scratch_shapes=[pltpu.VMEM((tm, d), bf16), pltpu.VMEM((tm, d), f32)],
        compiler_params=_cparams(("parallel", "arbitrary")),
        name="mlp",
    )(x, mod, w_up, w_down, g, b)


def _perm_w_in(w):
    d = w.shape[0]
    hk, hv = HEADS * DK, HEADS * DV
    o = 0
    parts = {}
    for name, width in (("gq", hk), ("gk", hk), ("gv", hv), ("gg", GRP), ("glr", 2 * GLA_RANK),
                        ("rq", hk), ("rk", hk), ("rv", hv), ("rg", GRP), ("su", GRP), ("hy", 3 * GRP)):
        parts[name] = w[:, o:o + width]
        o += width
    qk = lambda q, k: jnp.concatenate([q.reshape(d, HEADS, DK), k.reshape(d, HEADS, DK)], axis=-1).reshape(d, 2 * hk)
    cols = [qk(parts["gq"], parts["gk"]), parts["gv"], parts["gg"],
            qk(parts["rq"], parts["rk"]), parts["rv"], parts["rg"],
            parts["su"], parts["hy"], parts["glr"]]
    wp = jnp.concatenate(cols, axis=1)
    return jnp.pad(wp, ((0, 0), (0, NP_IN - wp.shape[1]))).astype(bf16)


def _gla_gate_params(w_gate, b_gate, norm_w):
    r = w_gate.shape[1]
    dup = lambda a: jnp.concatenate([a, a], axis=-1).reshape(*a.shape[:-2], 2 * HEADS * DK)
    wg = dup(w_gate.reshape(2, r, HEADS, DK))
    full = jnp.zeros((2, LANES, 2 * HEADS * DK), f32)
    for d in range(2):
        full = full.at[d, d * r:(d + 1) * r, :].set(wg[d])
    return full.astype(bf16), dup(b_gate.reshape(2, 1, HEADS, DK)), jnp.tile(norm_w[None, :], (1, HEADS))


def _s5_block_weights(bbre, bbim, c_re, c_im):
    _, g, hch, p = bbre.shape
    ntile = g // S5_GPT
    eye = jnp.eye(S5_GPT, dtype=f32)

    def in_map(b):
        b = b.reshape(2, ntile, S5_GPT, hch, p)
        return jnp.einsum("dtghp,gk->dtghkp", b, eye).reshape(2, ntile, S5_GPT * hch, S5_GPT * p)

    def out_map(c):
        c = c.reshape(2, ntile, S5_GPT, hch, p)
        return jnp.einsum("dtghp,gk->dtgpkh", c, eye).reshape(2, ntile, S5_GPT * p, S5_GPT * hch)

    wb = jnp.concatenate([in_map(bbre), in_map(bbim)], axis=-1).astype(bf16)
    return wb, out_map(c_re).astype(bf16), out_map(c_im).astype(bf16)


@functools.lru_cache(maxsize=None)
def _dft_tables(seq):
    n = 2 * seq
    k = np.arange(seq, dtype=np.int64)[:, None]
    s1 = np.arange(seq // 64, dtype=np.int64)[None, :]
    s2 = np.arange(64, dtype=np.int64)[None, :]
    a = 2.0 * np.pi * ((k * s1 * 64) % n) / n
    b = 2.0 * np.pi * ((k * s2) % n) / n
    return tuple(np.asarray(t, np.float32) for t in (np.cos(a), np.sin(a), np.cos(b), np.sin(b)))


def _dft_mats(seq):
    ca, sa, cb, sb = (jnp.asarray(t) for t in _dft_tables(seq))
    cosm = (ca[:, :, None] * cb[:, None, :] - sa[:, :, None] * sb[:, None, :]).reshape(seq, seq)
    sinm = (sa[:, :, None] * cb[:, None, :] + ca[:, :, None] * sb[:, None, :]).reshape(seq, seq)
    sign = jnp.where((jnp.arange(seq) & 1) == 0, 1.0, -1.0).astype(f32)
    ri = jnp.arange(seq)[:, None]
    ci = jnp.arange(seq)[None, :]
    fim = jnp.where(ri == 0, sign[None, :], -sinm)
    iim = jnp.where(ci == 0, sign[:, None], -sinm)
    return cosm.astype(bf16), fim.astype(bf16), iim.astype(bf16)


@functools.lru_cache(maxsize=None)
def _hy_feats(seq):
    t = np.linspace(0.0, 1.0, seq)[:, None]
    w = 2.0 * math.pi * np.arange(seq, dtype=np.float64)[:, None] / seq
    fr = np.linspace(1e-4, HY_BANDS - 1.0, HY_BANDS)[None, :]
    feats = np.concatenate([t, np.cos(fr * w), -np.sin(fr * w)], axis=-1).astype(np.float32)
    return np.pad(feats, ((0, 0), (0, LANES - feats.shape[1])))


@functools.lru_cache(maxsize=None)
def _hy_twiddles(lh):
    ang = np.pi * np.arange(lh, dtype=np.float64)[:, None] / lh
    return np.cos(ang).astype(np.float32), np.sin(ang).astype(np.float32)


def _layer_group(x, mod, layer, p, state, *, nbatch, seq, rows, tm, row_of, want_state, alpha):
    tm_dense = min(DENSE_TM, nbatch * seq)
    proj, u_t, hy = _in_proj(x, mod, layer, row_of(tm_dense), p["w_in"], tm_dense, seq)

    o_gla, s_gla = _linattn(proj, OFF_GLA, (p["gla_wg"], p["gla_bg"], p["gla_nw"]),
                            None if state is None else state["gla"], layer,
                            mode="gla", nbatch=nbatch, seq=seq, want_state=want_state)
    o_ret, s_ret = _linattn(proj, OFF_RET, (p["ret_dexp"],),
                            None if state is None else state["ret"], layer,
                            mode="ret", nbatch=nbatch, seq=seq, want_state=want_state)

    u_t = u_t.reshape(seq, nbatch, GRP)
    h0 =None if state is None else (state["s5_re"][layer], state["s5_im"][layer])
    y_t, hf_re, hf_im = _s5_scan(u_t, p["s5_wb"], p["s5_lam"], p["s5_cre"], p["s5_cim"], p["s5_d"], h0,
                                 want_state=want_state)
    y_s5 = y_t.reshape(seq, nbatch * GRP)
    tpb = seq // tm
    s5_spec = pl.BlockSpec((tm, GRP), lambda i: (i % tpb, i // tpb))

    lh = seq // 2
    hy2 = hy.reshape(nbatch * lh, 6 * GRP)
    fre, fim, iim = _dft_mats(lh)
    feats = _hy_feats(seq)
    cph, sph = (jnp.broadcast_to(jnp.asarray(t), (lh, HY_CT)) for t in _hy_twiddles(lh))
    filt = _hy_filter(jnp.asarray(feats[0::2]), jnp.asarray(feats[1::2]), p["hy_w1"], p["hy_b1"], p["hy_w2"],
                      p["hy_b2"], p["hy_freq"], p["hy_w3"], p["hy_decay"], fre, fim, cph, sph)
    seg2 = seq // rows // 2
    y1 = _longconv(hy2, (0, 3), hy2, (1, 4), p["hy_cw"], p["hy_cb"], 0, 1, fre, fim, iim, filt,
                   p["hy_d"], 0, nbatch=nbatch, lh=lh, seg2=seg2, conv_yin=True)
    o_hy = _longconv(y1, (0, 1), hy2, (2, 5), p["hy_cw"], p["hy_cb"], 0, 2, fre, fim, iim, filt,
                     p["hy_d"], 1, nbatch=nbatch, lh=lh, seg2=seg2, conv_yin=False)
    o_hy = o_hy.reshape(nbatch * seq, GRP)

    x = _out_proj((o_gla, o_ret, y_s5, o_hy), s5_spec, p["s5_glu_w"], p["s5_glu_b"], x, mod, layer, row_of(tm), p["w_out"],
                  p["ln1_g"], p["ln1_b"], tm, alpha)
    x = _mlp(x, mod, layer, row_of(tm_dense), p["w_up"], p["w_down"], p["ln2_g"], p["ln2_b"], tm_dense, MLP_TF, alpha)
    return x, (s_gla, s_ret, hf_re, hf_im)


def kernel(x_prompt, x_sample, state_gla, state_ret, state_s5_re, state_s5_im, c, c_ctx, ada_w, ada_b, w_in, gla_w_gate, gla_b_gate, gla_norm_w, ret_decay_exp, s5_a_re, s5_a_im, s5_log_step, s5_b_re, s5_b_im, s5_c_re, s5_c_im, s5_d, s5_glu_w, s5_glu_b, hy_conv_w, hy_conv_b, hy_f_w1, hy_f_b1, hy_f_w2, hy_f_b2, hy_f_freq, hy_f_w3, hy_decay, hy_d, w_out, ln1_g, ln1_b, w_up, w_down, ln2_g, ln2_b):
    bp, lp, d = x_prompt.shape
    bs, ls, _ = x_sample.shape
    depth = w_in.shape[0]
    alpha = (2 * depth) ** 0.25
    ngroup = s5_a_re.shape[2]
    ntile = ngroup // S5_GPT

    nrow = -(-(1 + bs) // SUBLANES) * SUBLANES
    c_rows = jnp.concatenate([c_ctx[None, :], c, jnp.zeros((nrow - 1 - bs, d), f32)], axis=0)
    mod = _ada(c_rows, ada_w, ada_b).reshape(depth, nrow, 6, d)

    tr_state = lambda s: jnp.swapaxes(s, -1, -2)
    s5_state = lambda s: jnp.transpose(s.reshape(bs, depth, 2, ntile, S5_SW), (1, 2, 3, 0, 4))
    lat_state = dict(gla=tr_state(state_gla), ret=tr_state(state_ret),
                     s5_re=s5_state(state_s5_re), s5_im=s5_state(state_s5_im))

    tm_ctx = min(lp, DENSE_TM)
    tm_lat = min(ls, DENSE_TM)
    yp = x_prompt.reshape(bp * lp, d)
    ys = x_sample.reshape(bs * ls, d)
    outs = []
    for l in range(depth):
        lre, lim, bbre, bbim = _s5_prep(s5_a_re[l], s5_a_im[l], s5_log_step[l], s5_b_re[l], s5_b_im[l])
        wb, cre, cim = _s5_block_weights(bbre, bbim, s5_c_re[l], s5_c_im[l])
        lam = jnp.stack([lre.reshape(2, ntile, S5_SW), lim.reshape(2, ntile, S5_SW)], axis=2)
        wg, bg, nw = _gla_gate_params(gla_w_gate[l], gla_b_gate[l], gla_norm_w[l])
        w1p = jnp.pad(hy_f_w1[l], ((0, LANES - hy_f_w1.shape[1]), (0, 0)))
        p = dict(
            w_in=_perm_w_in(w_in[l]),
            gla_wg=wg, gla_bg=bg, gla_nw=nw,
            ret_dexp=jnp.broadcast_to(ret_decay_exp[l][:, None, :, None], (2, SUBLANES, HEADS, LANES)).reshape(2, SUBLANES, GRP),
            s5_wb=wb, s5_lam=lam, s5_cre=cre, s5_cim=cim, s5_d=s5_d[l][None, :],
            s5_glu_w=s5_glu_w[l].astype(bf16), s5_glu_b=s5_glu_b[l][None, :],
            hy_cw=hy_conv_w[l], hy_cb=hy_conv_b[l][None, :],
            hy_w1=w1p, hy_b1=hy_f_b1[l][None, :], hy_w2=hy_f_w2[l], hy_b2=hy_f_b2[l][None, :],
            hy_freq=hy_f_freq[l][None, :], hy_w3=hy_f_w3[l], hy_decay=hy_decay[l][None, :],
            hy_d=hy_d[l][:, None, :],
            w_out=w_out[l].astype(bf16), ln1_g=ln1_g[l][None, :], ln1_b=ln1_b[l][None, :],
            w_up=w_up[l].astype(bf16), w_down=w_down[l].astype(bf16),
            ln2_g=ln2_g[l][None, :], ln2_b=ln2_b[l][None, :],
        )
        yp, st = _layer_group(yp, mod, l, p, None, nbatch=bp, seq=lp, rows=1, tm=tm_ctx,
                              row_of=lambda tile: (lambda i: 0), want_state=True, alpha=alpha)
        outs.append(st)
        ys, _ = _layer_group(ys, mod, l, p, lat_state, nbatch=bs, seq=ls, rows=ls // GRID_W, tm=tm_lat,
                             row_of=lambda tile: (lambda i: 1 + i // (ls // tile)), want_state=False, alpha=alpha)

    new_gla = jnp.stack([jnp.swapaxes(o[0], -1, -2) for o in outs], axis=1)
    new_ret = jnp.stack([jnp.swapaxes(o[1], -1, -2) for o in outs], axis=1)
    unpack = lambda h: jnp.transpose(h, (2, 0, 1, 3)).reshape(bp, 2, ngroup, S5_STATE)
    new_re = jnp.stack([unpack(o[2]) for o in outs], axis=1)
    new_im = jnp.stack([unpack(o[3]) for o in outs], axis=1)
    return (yp.reshape(bp, lp, d), ys.reshape(bs, ls, d), new_gla, new_ret, new_re, new_im)
```

```python
import functools
import math

import numpy as np
import jax
import jax.numpy as jnp
from jax import lax
from jax.experimental import pallas as pl
from jax.experimental.pallas import tpu as pltpu

f32 = jnp.float32
bf16 = jnp.bfloat16

GRID_W = 64
HEADS = 4
DK = 64
DV = 128
GLA_RANK = 16
GLA_TAU = 16.0
S5_GROUP = 16
S5_STATE = 64
HY_BANDS = 16
HY_SHORT = 3
CHUNK = 64
RET_CHUNK = 128
LN_EPS = 1e-5
NORM_EPS = 1e-6

LANES = 128
SUBLANES = 8
VMEM_LIMIT = 52 * 1024 * 1024

GRP = HEADS * DV
OFF_GLA = 0
OFF_RET = 3 * GRP
OFF_S5 = 6 * GRP
OFF_HY = 7 * GRP
OFF_LR = 10 * GRP
NP_IN = 5376
TN_IN = 1792
DENSE_TM = 512
MLP_TF = 1024


def _cparams(sem, vmem=VMEM_LIMIT):
    return pltpu.CompilerParams(dimension_semantics=sem, vmem_limit_bytes=vmem)


def _ln(x):
    mu = jnp.mean(x, axis=-1, keepdims=True)
    xc = x - mu
    var = jnp.mean(xc * xc, axis=-1, keepdims=True)
    return xc * lax.rsqrt(var + LN_EPS)


def _silu(x):
    return x * jax.nn.sigmoid(x)


def _bdot(a, b):
    return jnp.dot(a.astype(bf16), b.astype(bf16), preferred_element_type=f32)


def _bdot_nt(a, b):
    return lax.dot_general(a.astype(bf16), b.astype(bf16), (((1,), (1,)), ((), ())), preferred_element_type=f32)


def _bdot_tn(a, b):
    return lax.dot_general(a.astype(bf16), b.astype(bf16), (((0,), (0,)), ((), ())), preferred_element_type=f32)


def _dot_exact_lhs(a_bf, x):
    hi = x.astype(bf16)
    lo = (x - hi.astype(f32)).astype(bf16)
    d = lambda b: jnp.dot(a_bf, b, preferred_element_type=f32)
    return d(hi) + d(lo)


def _dot3(a, b):
    ah = a.astype(bf16)
    al = (a - ah.astype(f32)).astype(bf16)
    bh = b.astype(bf16)
    bl = (b - bh.astype(f32)).astype(bf16)
    d = lambda p, q: jnp.dot(p, q, preferred_element_type=f32)
    return d(ah, bh) + (d(al, bh) + d(ah, bl))


def _ada_kernel(c_ref, w_ref, b_ref, o_ref):
    s = _silu(c_ref[...])
    o_ref[...] = _bdot(s, w_ref[...]) + b_ref[...]


def _ada(c_rows, ada_w, ada_b):
    depth, d, n = ada_w.shape
    r = c_rows.shape[0]
    tn = 1024
    return pl.pallas_call(
        _ada_kernel,
        out_shape=jax.ShapeDtypeStruct((depth, r, n), f32),
        grid=(depth, n // tn),
        in_specs=[pl.BlockSpec((r, d), lambda l, j: (0, 0)),
                  pl.BlockSpec((None, d, tn), lambda l, j: (l, 0, j)),
                  pl.BlockSpec((None, 1, tn), lambda l, j: (l, 0, j))],
        out_specs=pl.BlockSpec((None, r, tn), lambda l, j: (l, 0, j)),
        compiler_params=_cparams(("parallel", "parallel")),
        name="ada",
    )(c_rows, ada_w, ada_b.reshape(depth, 1, n))


def _in_kernel(x_ref, mod_ref, w_ref, o_ref, u_ref, hy_ref, h_ref, *, nsub, sub):
    j = pl.program_id(1)

    @pl.when(j == 0)
    def _():
        y = _ln(x_ref[...])
        h_ref[...] = (y * (1.0 + mod_ref[1:2, :]) + mod_ref[0:1, :]).astype(bf16)

    cols = pl.ds(pl.multiple_of(j * TN_IN, LANES), TN_IN)
    o_ref[...] = jnp.dot(h_ref[...], w_ref[:, cols], preferred_element_type=f32)

    @pl.when(j == OFF_S5 // TN_IN)
    def _():
        c0 = OFF_S5 % TN_IN
        for k in range(nsub):
            u_ref[:, k * GRP:(k + 1) * GRP] = o_ref[k * sub:(k + 1) * sub, c0:c0 + GRP]

    @pl.when(j == OFF_HY // TN_IN)
    def _():
        c0 = OFF_HY % TN_IN
        hy_ref[...] = o_ref[:, c0:c0 + 3 * GRP]


def _in_proj(x, mod, layer, row_fn, w_perm, tm, seq):
    t, d = x.shape
    assert OFF_S5 // TN_IN == (OFF_S5 + GRP - 1) // TN_IN
    assert OFF_HY // TN_IN == (OFF_HY + 3 * GRP - 1) // TN_IN
    sub = min(tm, seq)
    nsub = tm // sub
    tps = seq // sub
    return pl.pallas_call(
        functools.partial(_in_kernel, nsub=nsub, sub=sub),
        out_shape=[jax.ShapeDtypeStruct((t, NP_IN), f32), jax.ShapeDtypeStruct((seq, (t // seq) * GRP), f32),
                   jax.ShapeDtypeStruct((t, 3 * GRP), f32)],
        grid=(t // tm, NP_IN // TN_IN),
        in_specs=[pl.BlockSpec((tm, d), lambda i, j: (i, 0)),
                  pl.BlockSpec((None, None, 6, d), lambda i, j: (layer, row_fn(i), 0, 0)),
                  pl.BlockSpec((d, NP_IN), lambda i, j: (0, 0), pipeline_mode=pl.Buffered(1))],
        out_specs=[pl.BlockSpec((tm, TN_IN), lambda i, j: (i, j)),
                   pl.BlockSpec((sub, nsub * GRP), lambda i, j: (i % tps, i // tps)),
                   pl.BlockSpec((tm, 3 * GRP), lambda i, j: (i, 0))],
        scratch_shapes=[pltpu.VMEM((tm, d), bf16)],
        compiler_params=_cparams(("parallel", "arbitrary")),
        name="in_proj",
    )(x, mod, w_perm)


LA_TAIL = 256


def _linattn_kernel(*refs, mode, seq, has_s0, want_state):
    it = iter(refs)
    qk_ref, v_ref, g_ref = next(it), next(it), next(it)
    if mode == "gla":
        lr_ref, wg_ref, bg_ref, nw_ref = next(it), next(it), next(it), next(it)
    else:
        dexp_ref = next(it)
    s0_ref = next(it) if has_s0 else None
    o_ref = next(it)
    sf_ref = next(it) if want_state else None
    ob_ref, st_ref = next(it), next(it)
    if mode == "gla":
        qkd_ref, ku_ref, dec_ref = next(it), next(it), next(it)

    c = CHUNK if mode == "gla" else RET_CHUNK
    nchunk = seq // c
    scale = DK ** -0.5
    row = lax.broadcasted_iota(jnp.int32, (c, c), 0)
    col = lax.broadcasted_iota(jnp.int32, (c, c), 1)
    is_k = (lax.broadcasted_iota(jnp.int32, (1, GRP), 1) & DK) != 0
    hq = lambda a, h: a[:, h * LANES:h * LANES + DK]
    hk = lambda a, h: a[:, h * LANES + DK:(h + 1) * LANES]
    hv = lambda h: slice(h * DV, (h + 1) * DV)

    if has_s0:
        st_ref[...] = s0_ref[...]
    else:
        st_ref[...] = jnp.zeros_like(st_ref)

    per_dir = []
    for d in (0, 1):
        causal = (row >= col) if d == 0 else (row <= col)
        if mode == "gla":
            tri = jnp.where(causal, 1.0, 0.0).astype(bf16)
            sgn = jnp.where(is_k, -1.0, 1.0)
            qsc = jnp.where(is_k, 1.0, scale)
            per_dir.append((causal, tri, sgn, qsc))
        else:
            lgr = jnp.log1p(-jnp.exp2(-dexp_ref[d]))[0:1, :]
            rowg = lax.broadcasted_iota(jnp.int32, (c, GRP), 0)
            pos = rowg if d == 0 else (c - 1 - rowg)
            pw = jnp.where(is_k, c - 1 - pos, pos + 1).astype(f32)
            wqk = jnp.exp(lgr * pw) * jnp.where(is_k, scale, 1.0)
            dist = ((row - col) if d == 0 else (col - row)).astype(f32)
            masks = [jnp.where(causal, jnp.exp(jnp.broadcast_to(lgr[:, h * LANES:h * LANES + c], (c, c)) * dist), 0.0) * scale
                     for h in range(HEADS)]
            dec = jnp.exp(lgr * float(c))
            per_dir.append((wqk, masks, dec))

    if mode == "gla":
        def pre(n, carry):
            rows = pl.ds(pl.multiple_of(n * c, c), c)
            qk = qk_ref[rows, :]
            lrc = lr_ref[rows, :]
            logits = [_bdot(lrc, wg_ref[d]) + bg_ref[d] for d in (0, 1)]
            la = [(jnp.minimum(x, 0.0) - jnp.log(1.0 + jnp.exp(-jnp.abs(x)))) * (1.0 / GLA_TAU) for x in logits]
            b = [_dot_exact_lhs(per_dir[d][1], la[d]) for d in (0, 1)]
            for d in (0, 1):
                _, _, sgn, qsc = per_dir[d]
                last = c - 1 if d == 0 else 0
                btot = b[d][last:last + 1, :]
                qkd_ref[d, rows, :] = (qk * (jnp.exp(b[d] * sgn) * qsc)).astype(bf16)
                ku_ref[d, rows, :] = (qk * jnp.exp(btot - b[d])).astype(bf16)
                dec_ref[d, n] = jnp.exp(btot)
            return carry

        lax.fori_loop(0, nchunk, pre, 0, unroll=2)

    def body(i, carry):
        rows, qka, qko, ku, dec = [], [], [], [], []
        for d in (0, 1):
            n = i if d == 0 else nchunk - 1 - i
            rows.append(pl.ds(pl.multiple_of(n * c, c), c))
            if mode == "gla":
                qka.append(qkd_ref[d, rows[d], :])
                qko.append(qka[d])
                ku.append(ku_ref[d, rows[d], :])
                dec.append(dec_ref[d, n])
            else:
                wqk, _, dcy = per_dir[d]
                qka.append(qk_ref[rows[d], :])
                qko.append(qka[d] * wqk)
                ku.append(qko[d])
                dec.append(dcy)
        chains = [(d, h) for d in (0, 1) for h in range(HEADS)]
        v = {(d, h): v_ref[rows[d], hv(h)] for d, h in chains}
        st = {(d, h): st_ref[d, h] for d, h in chains}
        att = {(d, h): _bdot_nt(hq(qka[d], h), hk(qka[d], h)) for d, h in chains}
        oi = {(d, h): _bdot_nt(hq(qko[d], h), st[d, h]) for d, h in chains}
        up = {(d, h): _bdot_tn(v[d, h], hk(ku[d], h)) for d, h in chains}
        for d, h in chains:
            if mode == "gla":
                a = jnp.where(per_dir[d][0], att[d, h], 0.0)
            else:
                a = att[d, h] * per_dir[d][1][h]
            o = _bdot(a, v[d, h]) + oi[d, h]
            st_ref[d, h] = hq(dec[d], h) * st[d, h] + up[d, h]
            if d == 0:
                o_ref[rows[d], hv(h)] = o
            else:
                ob_ref[rows[d], hv(h)] = o
        return carry

    lax.fori_loop(0, nchunk, body, 0, unroll=4 if mode == "gla" else 2)
    if want_state:
        sf_ref[...] = st_ref[...]

    tl = min(LA_TAIL, seq)

    def tail(i, carry):
        rows = pl.ds(pl.multiple_of(i * tl, tl), tl)
        for h in range(HEADS):
            o = o_ref[rows, hv(h)] + ob_ref[rows, hv(h)]
            if mode == "gla":
                o = o * lax.rsqrt(jnp.mean(o * o, axis=-1, keepdims=True) + NORM_EPS) * nw_ref[:, hv(h)]
            else:
                o = _ln(o)
            o_ref[rows, hv(h)] = o * _silu(g_ref[rows, hv(h)])
        return carry

    lax.fori_loop(0, seq // tl, tail, 0)


def _linattn(proj, col0, extra, s0t, layer, *, mode, nbatch, seq, want_state):
    cb = col0 // GRP
    full = lambda a: pl.BlockSpec(a.shape, lambda b: (0,) * a.ndim)
    in_specs = [pl.BlockSpec((seq, GRP), lambda b: (b, cb)),
                pl.BlockSpec((seq, GRP), lambda b: (b, cb + 1)),
                pl.BlockSpec((seq, GRP), lambda b: (b, cb + 2), pipeline_mode=pl.Buffered(1))]
    args = [proj, proj, proj]
    scratch = [pltpu.VMEM((seq, GRP), f32), pltpu.VMEM((2, HEADS, DV, DK), f32)]
    if mode == "gla":
        scratch += [pltpu.VMEM((2, seq, GRP), bf16), pltpu.VMEM((2, seq, GRP), bf16),
                    pltpu.VMEM((2, seq // CHUNK, 1, GRP), f32)]
    if mode == "gla":
        in_specs += [pl.BlockSpec((seq, LANES), lambda b: (b, OFF_LR // LANES))] + [full(a) for a in extra]
        args += [proj, *extra]
    else:
        in_specs += [full(a) for a in extra]
        args += list(extra)
    has_s0 = s0t is not None
    if has_s0:
        in_specs += [pl.BlockSpec((None, None, 2, HEADS, DV, DK), lambda b: (b, layer, 0, 0, 0, 0))]
        args += [s0t]
    out_shape = [jax.ShapeDtypeStruct((nbatch * seq, GRP), f32)]
    out_specs = [pl.BlockSpec((seq, GRP), lambda b: (b, 0))]
    if want_state:
        out_shape += [jax.ShapeDtypeStruct((nbatch, 2, HEADS, DV, DK), f32)]
        out_specs += [pl.BlockSpec((None, 2, HEADS, DV, DK), lambda b: (b, 0, 0, 0, 0))]
    res = pl.pallas_call(
        functools.partial(_linattn_kernel, mode=mode, seq=seq, has_s0=has_s0, want_state=want_state),
        out_shape=out_shape,
        grid=(nbatch,),
        in_specs=in_specs,
        out_specs=out_specs,
        scratch_shapes=scratch,
        compiler_params=_cparams(("parallel",)),
        name="linattn_" + mode,
    )(*args)
    return (res[0], res[1]) if want_state else (res[0], None)


def _s5_prep_kernel(are_ref, aim_ref, lstep_ref, bre_ref, bim_ref, lre_ref, lim_ref, bbre_ref, bbim_ref):
    ar, ai = are_ref[...], aim_ref[...]
    st = jnp.exp(lstep_ref[...])
    mag = jnp.exp(ar * st)
    lr = mag * jnp.cos(ai * st)
    li = mag * jnp.sin(ai * st)
    nr, ni = lr - 1.0, li
    den = ar * ar + ai * ai
    kr = (nr * ar + ni * ai) / den
    ki = (ni * ar - nr * ai) / den
    br, bi = bre_ref[...], bim_ref[...]
    lre_ref[...] = lr
    lim_ref[...] = li
    bbre_ref[...] = kr * br - ki * bi
    bbim_ref[...] = kr * bi + ki * br


def _s5_prep(a_re, a_im, log_step, b_re, b_im):
    _, g, p = a_re.shape
    hch = b_re.shape[-1]
    rows = 2 * g * hch
    bc = lambda a: jnp.broadcast_to(a[:, :, None, :], (2, g, hch, p)).reshape(rows, p)
    tr = lambda b: jnp.transpose(b, (0, 1, 3, 2)).reshape(rows, p)
    ls = jnp.broadcast_to(log_step[:, :, None, None], (2, g, hch, p)).reshape(rows, p)
    spec = pl.BlockSpec((rows, p), lambda: (0, 0))
    lre, lim, bbre, bbim = pl.pallas_call(
        _s5_prep_kernel,
        out_shape=[jax.ShapeDtypeStruct((rows, p), f32)] * 4,
        in_specs=[spec] * 5,
        out_specs=[spec] * 4,
        name="s5_prep",
    )(bc(a_re), bc(a_im), ls, tr(b_re), tr(b_im))
    r4 = lambda a: a.reshape(2, g, hch, p)
    return r4(lre)[:, :, 0, :], r4(lim)[:, :, 0, :], r4(bbre), r4(bbim)


S5_TC = 64
S5_GPT = LANES // S5_GROUP
S5_SW = S5_GPT * S5_STATE


def _s5_scan_kernel(*refs, seq, nb, has_h0, want_state):
    it = iter(refs)
    u_ref, wb_ref, lam_ref, cre_ref, cim_ref, dsk_ref = (next(it) for _ in range(6))
    h0re_ref = next(it) if has_h0 else None
    h0im_ref = next(it) if has_h0 else None
    y_ref = next(it)
    hfre_ref = next(it) if want_state else None
    hfim_ref = next(it) if want_state else None
    bu_ref, hs_ref = next(it), next(it)

    tc = S5_TC
    nchunk = seq // tc
    sw = S5_SW

    def skip(i, carry):
        rows = pl.ds(pl.multiple_of(i * tc, tc), tc)
        y_ref[rows] = u_ref[rows] * dsk_ref[...]
        return carry

    lax.fori_loop(0, nchunk, skip, 0)

    lam = [(jnp.broadcast_to(lam_ref[d, 0:1, :], (nb, sw)), jnp.broadcast_to(lam_ref[d, 1:2, :], (nb, sw)))
           for d in (0, 1)]

    def chunk(i, carry):
        h = [list(carry[0:2]), list(carry[2:4])]
        t0 = [pl.multiple_of(i * tc, tc), pl.multiple_of((nchunk - 1 - i) * tc, tc)]
        for d in (0, 1):
            ub = u_ref[pl.ds(t0[d], tc)]
            bu_ref[d] = _bdot(ub.reshape(tc * nb, LANES), wb_ref[d])
        for s in range(tc):
            for d in (0, 1):
                t = s if d == 0 else tc - 1 - s
                rs = slice(t * nb, (t + 1) * nb)
                lam_r, lam_i = lam[d]
                hre, him = h[d]
                nre = lam_r * hre - lam_i * him + bu_ref[d, rs, 0:sw]
                nim = lam_r * him + lam_i * hre + bu_ref[d, rs, sw:2 * sw]
                h[d] = [nre, nim]
                hs_ref[d, rs, 0:sw] = nre
                hs_ref[d, rs, sw:2 * sw] = nim
        for d in (0, 1):
            y = _bdot(hs_ref[d, :, 0:sw], cre_ref[d]) - _bdot(hs_ref[d, :, sw:2 * sw], cim_ref[d])
            rows = pl.ds(t0[d], tc)
            y_ref[rows] = y_ref[rows] + y.reshape(tc, nb, LANES)
        return (*h[0], *h[1])

    if has_h0:
        h0 = (h0re_ref[0], h0im_ref[0], h0re_ref[1], h0im_ref[1])
    else:
        h0 = (jnp.zeros((nb, sw), f32),) * 4
    hfin = lax.fori_loop(0, nchunk, chunk, h0)
    if want_state:
        for d in (0, 1):
            hfre_ref[d] = hfin[2 * d]
            hfim_ref[d] = hfin[2 * d + 1]


def _s5_scan(u_t, wb, lam, cre, cim, dsk, h0, *, want_state):
    seq, nb, ch = u_t.shape
    ntile = ch // LANES
    sw = S5_SW
    in_specs = [pl.BlockSpec((seq, nb, LANES), lambda j: (0, 0, j)),
                pl.BlockSpec((2, None, LANES, 2 * sw), lambda j: (0, j, 0, 0)),
                pl.BlockSpec((2, None, 2, sw), lambda j: (0, j, 0, 0)),
                pl.BlockSpec((2, None, sw, LANES), lambda j: (0, j, 0, 0)),
                pl.BlockSpec((2, None, sw, LANES), lambda j: (0, j, 0, 0)),
                pl.BlockSpec((1, LANES), lambda j: (0, j))]
    args = [u_t, wb, lam, cre, cim, dsk]
    has_h0 = h0 is not None
    st_spec = pl.BlockSpec((2, None, nb, sw), lambda j: (0, j, 0, 0))
    if has_h0:
        in_specs += [st_spec, st_spec]
        args += list(h0)
    out_shape = [jax.ShapeDtypeStruct((seq, nb, ch), f32)]
    out_specs = [pl.BlockSpec((seq, nb, LANES), lambda j: (0, 0, j))]
    if want_state:
        out_shape += [jax.ShapeDtypeStruct((2, ntile, nb, sw), f32)] * 2
        out_specs += [st_spec, st_spec]
    res = pl.pallas_call(
        functools.partial(_s5_scan_kernel, seq=seq, nb=nb, has_h0=has_h0, want_state=want_state),
        out_shape=out_shape,
        grid=(ntile,),
        in_specs=in_specs,
        out_specs=out_specs,
        scratch_shapes=[pltpu.VMEM((2, S5_TC * nb, 2 * sw), f32), pltpu.VMEM((2, S5_TC * nb, 2 * sw), f32)],
        compiler_params=_cparams(("parallel",)),
        name="s5_scan",
    )(*args)
    return (res[0], res[1], res[2]) if want_state else (res[0], None, None)


HY_TK = 256
HY_CT = 256


def _pk_mul(xr, xi, a, b, d):
    return xr * a - xi * b, xr * b + xi * d


def _hy_filter_kernel(fte_ref, fto_ref, w1_ref, b1_ref, w2_ref, b2_ref, fq_ref, w3f_ref, w3b_ref, dcf_ref, dcb_ref,
                      fre_ref, fim_ref, cph_ref, sph_ref,
                      fea_ref, feb_ref, foa_ref, fob_ref, ga_ref, gb_ref, ny_ref, xs_ref, *, lh, tk):
    j = pl.program_id(1)
    ct = HY_CT

    @pl.when(j == 0)
    def _():
        fq = fq_ref[...]

        def hidden(feat):
            h = jnp.sin(fq * (_dot3(feat, w1_ref[...]) + b1_ref[...]))
            return jnp.sin(fq * (_dot3(h, w2_ref[...]) + b2_ref[...])), feat[:, 0:1]

        he, te = hidden(fte_ref[...])
        ho, to = hidden(fto_ref[...])
        raw = lambda h, t, w3_ref, dc_ref: _dot3(h, w3_ref[...]) * jnp.exp(-t * jnp.abs(dc_ref[...]))
        ffe, ffo = raw(he, te, w3f_ref, dcf_ref), raw(ho, to, w3f_ref, dcf_ref)
        fbe, fbo = raw(he, te, w3b_ref, dcb_ref), raw(ho, to, w3b_ref, dcb_ref)
        sf = jnp.sum(jnp.abs(ffe), axis=0, keepdims=True) + jnp.sum(jnp.abs(ffo), axis=0, keepdims=True)
        sb = jnp.sum(jnp.abs(fbe), axis=0, keepdims=True) + jnp.sum(jnp.abs(fbo), axis=0, keepdims=True)
        rowi = lax.broadcasted_iota(jnp.int32, (lh, ct), 0)
        xs_ref[:, 0 * ct:1 * ct] = (ffe / sf).astype(bf16)
        xs_ref[:, 1 * ct:2 * ct] = jnp.where(rowi == 0, 0.0, fbe / sb).astype(bf16)
        xs_ref[:, 2 * ct:3 * ct] = (ffo / sf).astype(bf16)
        xs_ref[:, 3 * ct:4 * ct] = (fbo / sb).astype(bf16)

    xs = xs_ref[...]
    r = jnp.dot(fre_ref[...], xs, preferred_element_type=f32)
    i = jnp.dot(fim_ref[...], xs, preferred_element_type=f32)
    r1, r2, r3, r4 = (r[:, q * ct:(q + 1) * ct] for q in range(4))
    i1, i2, i3, i4 = (i[:, q * ct:(q + 1) * ct] for q in range(4))
    c, s = cph_ref[...], sph_ref[...]
    fe_re, fe_im = r1 + r2, i1 - i2
    fo_re = r3 + c * r4 + s * i4
    fo_im = i3 + s * r4 - c * i4
    g_re = c * fo_re + s * fo_im
    g_im = c * fo_im - s * fo_re
    scl = 1.0 / lh
    first = jnp.logical_and(lax.broadcasted_iota(jnp.int32, (tk, ct), 0) == 0, j == 0)
    for a_ref, b_ref, re, im in ((fea_ref, feb_ref, fe_re, fe_im), (foa_ref, fob_ref, fo_re, fo_im),
                                 (ga_ref, gb_ref, g_re, g_im)):
        a_ref[...] = jnp.where(first, 0.5 * scl, scl) * re
        b_ref[...] = jnp.where(first, 0.0, scl * im)

    @pl.when(j == 0)
    def _():
        fo_ny = i3[0:1, :] - i4[0:1, :]
        ny_ref[...] = jnp.zeros_like(ny_ref)
        ny_ref[0:1, :] = (0.5 * scl) * (i1[0:1, :] + i2[0:1, :])
        ny_ref[1:2, :] = (0.5 * scl) * fo_ny
        ny_ref[2:3, :] = (-0.5 * scl) * fo_ny


def _hy_filter(feat_e, feat_o, w1p, b1, w2, b2, fq, w3, dc, fre, fim, cph, sph):
    lh = feat_e.shape[0]
    ncol = w3.shape[1] // 2
    nct = ncol // HY_CT
    tk = min(HY_TK, lh)
    full = lambda a: pl.BlockSpec(a.shape, lambda c, j: (0,) * a.ndim)
    out_spec = pl.BlockSpec((tk, HY_CT), lambda c, j: (j, c))
    return pl.pallas_call(
        functools.partial(_hy_filter_kernel, lh=lh, tk=tk),
        out_shape=[jax.ShapeDtypeStruct((lh, ncol), f32)] * 6 + [jax.ShapeDtypeStruct((SUBLANES, ncol), f32)],
        grid=(nct, lh // tk),
        in_specs=[full(feat_e), full(feat_o), full(w1p), full(b1), full(w2), full(b2), full(fq),
                  pl.BlockSpec((w3.shape[0], HY_CT), lambda c, j: (0, c)),
                  pl.BlockSpec((w3.shape[0], HY_CT), lambda c, j: (0, nct + c)),
                  pl.BlockSpec((1, HY_CT), lambda c, j: (0, c)),
                  pl.BlockSpec((1, HY_CT), lambda c, j: (0, nct + c)),
                  pl.BlockSpec((tk, lh), lambda c, j: (j, 0)),
                  pl.BlockSpec((tk, lh), lambda c, j: (j, 0)),
                  pl.BlockSpec((tk, HY_CT), lambda c, j: (j, 0)),
                  pl.BlockSpec((tk, HY_CT), lambda c, j: (j, 0))],
        out_specs=[out_spec] * 6 + [pl.BlockSpec((SUBLANES, HY_CT), lambda c, j: (0, c))],
        scratch_shapes=[pltpu.VMEM((lh, 4 * HY_CT), bf16)],
        compiler_params=_cparams(("parallel", "arbitrary")),
        name="hy_filter",
    )(feat_e, feat_o, w1p, b1, w2, b2, fq, w3, w3, dc, dc, fre, fim, cph, sph)


def _short_conv(x, w_ref, b_ref, seg):
    rows = x.shape[0]
    r = lax.broadcasted_iota(jnp.int32, x.shape, 0) & (seg - 1)
    xp = jnp.where(r == 0, 0.0, pltpu.roll(x, 1, 0))
    xn = jnp.where(r == seg - 1, 0.0, pltpu.roll(x, rows - 1, 0))
    return xp * w_ref[0:1, :] + x * w_ref[1:2, :] + xn * w_ref[2:3, :] + b_ref[...]


HY_PB = 256


def _longconv_kernel(yin_ref, gate_ref, cwy_ref, cby_ref, cwg_ref, cbg_ref, fre_ref, fim_ref, iim_ref,
                     fea_ref, feb_ref, foa_ref, fob_ref, ga_ref, gb_ref, ny_ref, dsk_ref,
                     o_ref, y32_ref, ybf_ref, acc_ref, *, seg, conv_yin, tk):
    j = pl.program_id(1)
    g = GRP
    seq = yin_ref.shape[0]
    pb = min(HY_PB, seq)
    hb = pb // 2
    pr = lax.broadcasted_iota(jnp.int32, (pb, pb), 0)
    pc = lax.broadcasted_iota(jnp.int32, (pb, pb), 1)

    @pl.when(j == 0)
    def _():
        y = yin_ref[...]
        if conv_yin:
            y = _short_conv(y, cwy_ref, cby_ref, seg)
        y32_ref[...] = y
        split = jnp.where(pc == jnp.where(pr < hb, 2 * pr, 2 * (pr - hb) + 1), 1.0, 0.0).astype(bf16)
        for k in range(seq // pb):
            t = jnp.dot(split, y[k * pb:(k + 1) * pb, :].astype(bf16), preferred_element_type=f32).astype(bf16)
            ybf_ref[k * hb:(k + 1) * hb, 0:g] = t[0:hb, :]
            ybf_ref[k * hb:(k + 1) * hb, g:2 * g] = t[hb:pb, :]
        acc_ref[...] = jnp.zeros_like(acc_ref)

    freq = pl.ds(pl.multiple_of(j * tk, tk), tk)
    yb = ybf_ref[...]
    r = jnp.dot(fre_ref[freq, :], yb, preferred_element_type=f32)
    i = jnp.dot(fim_ref[freq, :], yb, preferred_element_type=f32)
    first = jnp.logical_and(lax.broadcasted_iota(jnp.int32, (tk, g), 0) == 0, j == 0)
    fe_a, fo_a, g_a = fea_ref[...], foa_ref[...], ga_ref[...]
    f_e = (fe_a, feb_ref[...], jnp.where(first, ny_ref[0:1, :], fe_a))
    f_o = (fo_a, fob_ref[...], jnp.where(first, ny_ref[1:2, :], fo_a))
    f_g = (g_a, gb_ref[...], jnp.where(first, ny_ref[2:3, :], g_a))
    er, ei, orr, oi = r[:, 0:g], i[:, 0:g], r[:, g:2 * g], i[:, g:2 * g]
    p1, p2 = _pk_mul(er, ei, *f_e), _pk_mul(orr, oi, *f_g)
    p3, p4 = _pk_mul(er, ei, *f_o), _pk_mul(orr, oi, *f_e)
    zr = jnp.concatenate([p1[0] + p2[0], p3[0] + p4[0]], axis=1).astype(bf16)
    zi = jnp.concatenate([p1[1] + p2[1], p3[1] + p4[1]], axis=1).astype(bf16)
    acc_ref[...] += (jnp.dot(fre_ref[:, freq], zr, preferred_element_type=f32)
                     + jnp.dot(iim_ref[:, freq], zi, preferred_element_type=f32))

    @pl.when(j == pl.num_programs(1) - 1)
    def _():
        gate = _short_conv(gate_ref[...], cwg_ref, cbg_ref, seg)
        d = dsk_ref[...]
        merge = jnp.where(pr == jnp.where(pc < hb, 2 * pc, 2 * (pc - hb) + 1), 1.0, 0.0).astype(bf16)
        for k in range(seq // pb):
            stacked = jnp.concatenate([acc_ref[k * hb:(k + 1) * hb, 0:g], acc_ref[k * hb:(k + 1) * hb, g:2 * g]], axis=0)
            conv = _dot_exact_lhs(merge, stacked)
            rows = slice(k * pb, (k + 1) * pb)
            o_ref[rows, :] = gate[rows, :] * (conv + d * y32_ref[rows, :])


def _longconv(yin, yin_cb, gate_arr, gate_cb, cw, cb, part_y, part_g, fre, fim, iim, filt, dsk, order,
              *, nbatch, seq, seg, conv_yin):
    lh = seq // 2
    tk = min(HY_TK, lh)
    once = pl.Buffered(1)
    cw_spec = lambda p: pl.BlockSpec((HY_SHORT, GRP), lambda b, j: (0, p))
    cb_spec = lambda p: pl.BlockSpec((1, GRP), lambda b, j: (0, p))
    dft_spec = pl.BlockSpec((lh, lh), lambda b, j: (0, 0), pipeline_mode=once)
    filt_spec = pl.BlockSpec((tk, GRP), lambda b, j: (j, order))
    return pl.pallas_call(
        functools.partial(_longconv_kernel, seg=seg, conv_yin=conv_yin, tk=tk),
        out_shape=jax.ShapeDtypeStruct((nbatch * seq, GRP), f32),
        grid=(nbatch, lh // tk),
        in_specs=[pl.BlockSpec((seq, GRP), lambda b, j: (b, yin_cb)),
                  pl.BlockSpec((seq, GRP), lambda b, j: (b, gate_cb), pipeline_mode=once),
                  cw_spec(part_y), cb_spec(part_y), cw_spec(part_g), cb_spec(part_g),
                  dft_spec, dft_spec, dft_spec,
                  filt_spec, filt_spec, filt_spec, filt_spec, filt_spec, filt_spec,
                  pl.BlockSpec((SUBLANES, GRP), lambda b, j: (0, order)),
                  pl.BlockSpec((None, 1, GRP), lambda b, j: (order, 0, 0))],
        out_specs=pl.BlockSpec((seq, GRP), lambda b, j: (b, 0)),
        scratch_shapes=[pltpu.VMEM((seq, GRP), f32), pltpu.VMEM((lh, 2 * GRP), bf16), pltpu.VMEM((lh, 2 * GRP), f32)],
        compiler_params=_cparams(("parallel", "arbitrary")),
        name="hy_longconv",
    )(yin, gate_arr, cw, cb, cw, cb, fre, fim, iim, *filt, dsk)


def _out_kernel(m0_ref, m1_ref, y2_ref, m3_ref, gw_ref, gb_ref, x_ref, mod_ref, w_ref, g_ref, b_ref, o_ref, *, alpha):
    y = y2_ref[...]
    z = y * (0.5 * (1.0 + jnp.tanh(math.sqrt(2.0 / math.pi) * (y + 0.044715 * (y * y * y)))))
    m2 = z * jax.nn.sigmoid(_bdot(z, gw_ref[...]) + gb_ref[...])
    acc = _bdot(m0_ref[...], w_ref[0 * GRP:1 * GRP, :])
    acc += _bdot(m1_ref[...], w_ref[1 * GRP:2 * GRP, :])
    acc += _bdot(m2, w_ref[2 * GRP:3 * GRP, :])
    acc += _bdot(m3_ref[...], w_ref[3 * GRP:4 * GRP, :])
    r = alpha * x_ref[...] + mod_ref[2:3, :] * acc
    o_ref[...] = _ln(r) * g_ref[...] + b_ref[...]


def _out_proj(mixes, s5_spec, glu_w, glu_b, x, mod, layer, row_fn, w_out, g, b, tm, alpha):
    t, d = x.shape
    mspec = pl.BlockSpec((tm, GRP), lambda i: (i, 0))
    const = lambda a: pl.BlockSpec(a.shape, lambda i: (0,) * a.ndim)
    return pl.pallas_call(
        functools.partial(_out_kernel, alpha=alpha),
        out_shape=jax.ShapeDtypeStruct((t, d), f32),
        grid=(t // tm,),
        in_specs=[mspec, mspec, s5_spec, mspec, const(glu_w), const(glu_b),
                  pl.BlockSpec((tm, d), lambda i: (i, 0)),
                  pl.BlockSpec((None, None, 6, d), lambda i: (layer, row_fn(i), 0, 0)),
                  pl.BlockSpec((d, d), lambda i: (0, 0), pipeline_mode=pl.Buffered(1)),
                  pl.BlockSpec((1, d), lambda i: (0, 0)),
                  pl.BlockSpec((1, d), lambda i: (0, 0))],
        out_specs=pl.BlockSpec((tm, d), lambda i: (i, 0)),
        compiler_params=_cparams(("parallel",)),
        name="out_proj",
    )(*mixes, glu_w, glu_b, x, mod, w_out, g, b)


def _mlp_kernel(x_ref, mod_ref, wu_ref, wd_ref, g_ref, b_ref, o_ref, h_ref, acc_ref, *, alpha):
    j = pl.program_id(1)

    @pl.when(j == 0)
    def _():
        y = _ln(x_ref[...])
        h_ref[...] = (y * (1.0 + mod_ref[4:5, :]) + mod_ref[3:4, :]).astype(bf16)
        acc_ref[...] = jnp.zeros_like(acc_ref)

    u = jnp.maximum(jnp.dot(h_ref[...], wu_ref[...], preferred_element_type=f32), 0.0)
    acc_ref[...] += jnp.dot((u * u).astype(bf16), wd_ref[...], preferred_element_type=f32)

    @pl.when(j == pl.num_programs(1) - 1)
    def _():
        r = alpha * x_ref[...] + mod_ref[5:6, :] * acc_ref[...]
        o_ref[...] = _ln(r) * g_ref[...] + b_ref[...]


def _mlp(x, mod, layer, row_fn, w_up, w_down, g, b, tm, tf, alpha):
    t, d = x.shape
    dff = w_up.shape[1]
    return pl.pallas_call(
        functools.partial(_mlp_kernel, alpha=alpha),
        out_shape=jax.ShapeDtypeStruct((t, d), f32),
        grid=(t // tm, dff // tf),
        in_specs=[pl.BlockSpec((tm, d), lambda i, j: (i, 0)),
                  pl.BlockSpec((None, None, 6, d), lambda i, j: (layer, row_fn(i), 0, 0)),
                  pl.BlockSpec((d, tf), lambda i, j: (0, j)),
                  pl.BlockSpec((tf, d), lambda i, j: (j, 0)),
                  pl.BlockSpec((1, d), lambda i, j: (0, 0)),
                  pl.BlockSpec((1, d), lambda i, j: (0, 0))],
        out_specs=pl.BlockSpec((tm, d), lambda i, j: (i, 0)),
        scratch_shapes=[pltpu.VMEM((tm, d), bf16), pltpu.VMEM((tm, d), f32)],
        compiler_params=_cparams(("parallel", "arbitrary")),
        name="mlp",
    )(x, mod, w_up, w_down, g, b)


def _perm_w_in(w):
    d = w.shape[0]
    hk, hv = HEADS * DK, HEADS * DV
    o = 0
    parts = {}
    for name, width in (("gq", hk), ("gk", hk), ("gv", hv), ("gg", GRP), ("glr", 2 * GLA_RANK),
                        ("rq", hk), ("rk", hk), ("rv", hv), ("rg", GRP), ("su", GRP), ("hy", 3 * GRP)):
        parts[name] = w[:, o:o + width]
        o += width
    qk = lambda q, k: jnp.concatenate([q.reshape(d, HEADS, DK), k.reshape(d, HEADS, DK)], axis=-1).reshape(d, 2 * hk)
    cols = [qk(parts["gq"], parts["gk"]), parts["gv"], parts["gg"],
            qk(parts["rq"], parts["rk"]), parts["rv"], parts["rg"],
            parts["su"], parts["hy"], parts["glr"]]
    wp = jnp.concatenate(cols, axis=1)
    return jnp.pad(wp, ((0, 0), (0, NP_IN - wp.shape[1]))).astype(bf16)


def _gla_gate_params(w_gate, b_gate, norm_w):
    r = w_gate.shape[1]
    dup = lambda a: jnp.concatenate([a, a], axis=-1).reshape(*a.shape[:-2], 2 * HEADS * DK)
    wg = dup(w_gate.reshape(2, r, HEADS, DK))
    full = jnp.zeros((2, LANES, 2 * HEADS * DK), f32)
    for d in range(2):
        full = full.at[d, d * r:(d + 1) * r, :].set(wg[d])
    return full.astype(bf16), dup(b_gate.reshape(2, 1, HEADS, DK)), jnp.tile(norm_w[None, :], (1, HEADS))


def _s5_block_weights(bbre, bbim, c_re, c_im):
    _, g, hch, p = bbre.shape
    ntile = g // S5_GPT
    eye = jnp.eye(S5_GPT, dtype=f32)

    def in_map(b):
        b = b.reshape(2, ntile, S5_GPT, hch, p)
        return jnp.einsum("dtghp,gk->dtghkp", b, eye).reshape(2, ntile, S5_GPT * hch, S5_GPT * p)

    def out_map(c):
        c = c.reshape(2, ntile, S5_GPT, hch, p)
        return jnp.einsum("dtghp,gk->dtgpkh", c, eye).reshape(2, ntile, S5_GPT * p, S5_GPT * hch)

    wb = jnp.concatenate([in_map(bbre), in_map(bbim)], axis=-1).astype(bf16)
    return wb, out_map(c_re).astype(bf16), out_map(c_im).astype(bf16)


@functools.lru_cache(maxsize=None)
def _dft_tables(seq):
    n = 2 * seq
    k = np.arange(seq, dtype=np.int64)[:, None]
    s1 = np.arange(seq // 64, dtype=np.int64)[None, :]
    s2 = np.arange(64, dtype=np.int64)[None, :]
    a = 2.0 * np.pi * ((k * s1 * 64) % n) / n
    b = 2.0 * np.pi * ((k * s2) % n) / n
    return tuple(np.asarray(t, np.float32) for t in (np.cos(a), np.sin(a), np.cos(b), np.sin(b)))


def _dft_mats(seq):
    ca, sa, cb, sb = (jnp.asarray(t) for t in _dft_tables(seq))
    cosm = (ca[:, :, None] * cb[:, None, :] - sa[:, :, None] * sb[:, None, :]).reshape(seq, seq)
    sinm = (sa[:, :, None] * cb[:, None, :] + ca[:, :, None] * sb[:, None, :]).reshape(seq, seq)
    sign = jnp.where((jnp.arange(seq) & 1) == 0, 1.0, -1.0).astype(f32)
    ri = jnp.arange(seq)[:, None]
    ci = jnp.arange(seq)[None, :]
    fim = jnp.where(ri == 0, sign[None, :], -sinm)
    iim = jnp.where(ci == 0, sign[:, None], -sinm)
    return cosm.astype(bf16), fim.astype(bf16), iim.astype(bf16)


@functools.lru_cache(maxsize=None)
def _hy_feats(seq):
    t = np.linspace(0.0, 1.0, seq)[:, None]
    w = 2.0 * math.pi * np.arange(seq, dtype=np.float64)[:, None] / seq
    fr = np.linspace(1e-4, HY_BANDS - 1.0, HY_BANDS)[None, :]
    feats = np.concatenate([t, np.cos(fr * w), -np.sin(fr * w)], axis=-1).astype(np.float32)
    return np.pad(feats, ((0, 0), (0, LANES - feats.shape[1])))


@functools.lru_cache(maxsize=None)
def _hy_twiddles(lh):
    ang = np.pi * np.arange(lh, dtype=np.float64)[:, None] / lh
    return np.cos(ang).astype(np.float32), np.sin(ang).astype(np.float32)


def _layer_group(x, mod, layer, p, state, *, nbatch, seq, rows, tm, row_of, want_state, alpha):
    tm_dense = min(DENSE_TM, nbatch * seq)
    proj, u_t, hy = _in_proj(x, mod, layer, row_of(tm_dense), p["w_in"], tm_dense, seq)

    o_gla, s_gla = _linattn(proj, OFF_GLA, (p["gla_wg"], p["gla_bg"], p["gla_nw"]),
                            None if state is None else state["gla"], layer,
                            mode="gla", nbatch=nbatch, seq=seq, want_state=want_state)
    o_ret, s_ret = _linattn(proj, OFF_RET, (p["ret_dexp"],),
                            None if state is None else state["ret"], layer,
                            mode="ret", nbatch=nbatch, seq=seq, want_state=want_state)

    u_t = u_t.reshape(seq, nbatch, GRP)
    h0 =None if state is None else (state["s5_re"][layer], state["s5_im"][layer])
    y_t, hf_re, hf_im = _s5_scan(u_t, p["s5_wb"], p["s5_lam"], p["s5_cre"], p["s5_cim"], p["s5_d"], h0,
                                 want_state=want_state)
    y_s5 = y_t.reshape(seq, nbatch * GRP)
    tpb = seq // tm
    s5_spec = pl.BlockSpec((tm, GRP), lambda i: (i % tpb, i // tpb))

    lh = seq // 2
    fre, fim, iim = _dft_mats(lh)
    feats = _hy_feats(seq)
    cph, sph = (jnp.broadcast_to(jnp.asarray(t), (lh, HY_CT)) for t in _hy_twiddles(lh))
    filt = _hy_filter(jnp.asarray(feats[0::2]), jnp.asarray(feats[1::2]), p["hy_w1"], p["hy_b1"], p["hy_w2"],
                      p["hy_b2"], p["hy_freq"], p["hy_w3"], p["hy_decay"], fre, fim, cph, sph)
    seg = seq // rows
    y1 = _longconv(hy, 0, hy, 1, p["hy_cw"], p["hy_cb"], 0, 1, fre, fim, iim, filt,
                   p["hy_d"], 0, nbatch=nbatch, seq=seq, seg=seg, conv_yin=True)
    o_hy = _longconv(y1, 0, hy, 2, p["hy_cw"], p["hy_cb"], 0, 2, fre, fim, iim, filt,
                     p["hy_d"], 1, nbatch=nbatch, seq=seq, seg=seg, conv_yin=False)

    x = _out_proj((o_gla, o_ret, y_s5, o_hy), s5_spec, p["s5_glu_w"], p["s5_glu_b"], x, mod, layer, row_of(tm), p["w_out"],
                  p["ln1_g"], p["ln1_b"], tm, alpha)
    x = _mlp(x, mod, layer, row_of(tm_dense), p["w_up"], p["w_down"], p["ln2_g"], p["ln2_b"], tm_dense, MLP_TF, alpha)
    return x, (s_gla, s_ret, hf_re, hf_im)


def kernel(x_prompt, x_sample, state_gla, state_ret, state_s5_re, state_s5_im, c, c_ctx, ada_w, ada_b, w_in, gla_w_gate, gla_b_gate, gla_norm_w, ret_decay_exp, s5_a_re, s5_a_im, s5_log_step, s5_b_re, s5_b_im, s5_c_re, s5_c_im, s5_d, s5_glu_w, s5_glu_b, hy_conv_w, hy_conv_b, hy_f_w1, hy_f_b1, hy_f_w2, hy_f_b2, hy_f_freq, hy_f_w3, hy_decay, hy_d, w_out, ln1_g, ln1_b, w_up, w_down, ln2_g, ln2_b):
    bp, lp, d = x_prompt.shape
    bs, ls, _ = x_sample.shape
    depth = w_in.shape[0]
    alpha = (2 * depth) ** 0.25
    ngroup = s5_a_re.shape[2]
    ntile = ngroup // S5_GPT

    nrow = -(-(1 + bs) // SUBLANES) * SUBLANES
    c_rows = jnp.concatenate([c_ctx[None, :], c, jnp.zeros((nrow - 1 - bs, d), f32)], axis=0)
    mod = _ada(c_rows, ada_w, ada_b).reshape(depth, nrow, 6, d)

    tr_state = lambda s: jnp.swapaxes(s, -1, -2)
    s5_state = lambda s: jnp.transpose(s.reshape(bs, depth, 2, ntile, S5_SW), (1, 2, 3, 0, 4))
    lat_state = dict(gla=tr_state(state_gla), ret=tr_state(state_ret),
                     s5_re=s5_state(state_s5_re), s5_im=s5_state(state_s5_im))

    tm_ctx = min(lp, DENSE_TM)
    tm_lat = min(ls, DENSE_TM)
    yp = x_prompt.reshape(bp * lp, d)
    ys = x_sample.reshape(bs * ls, d)
    outs = []
    for l in range(depth):
        lre, lim, bbre, bbim = _s5_prep(s5_a_re[l], s5_a_im[l], s5_log_step[l], s5_b_re[l], s5_b_im[l])
        wb, cre, cim = _s5_block_weights(bbre, bbim, s5_c_re[l], s5_c_im[l])
        lam = jnp.stack([lre.reshape(2, ntile, S5_SW), lim.reshape(2, ntile, S5_SW)], axis=2)
        wg, bg, nw = _gla_gate_params(gla_w_gate[l], gla_b_gate[l], gla_norm_w[l])
        w1p = jnp.pad(hy_f_w1[l], ((0, LANES - hy_f_w1.shape[1]), (0, 0)))
        p = dict(
            w_in=_perm_w_in(w_in[l]),
            gla_wg=wg, gla_bg=bg, gla_nw=nw,
            ret_dexp=jnp.broadcast_to(ret_decay_exp[l][:, None, :, None], (2, SUBLANES, HEADS, LANES)).reshape(2, SUBLANES, GRP),
            s5_wb=wb, s5_lam=lam, s5_cre=cre, s5_cim=cim, s5_d=s5_d[l][None, :],
            s5_glu_w=s5_glu_w[l].astype(bf16), s5_glu_b=s5_glu_b[l][None, :],
            hy_cw=hy_conv_w[l], hy_cb=hy_conv_b[l][None, :],
            hy_w1=w1p, hy_b1=hy_f_b1[l][None, :], hy_w2=hy_f_w2[l], hy_b2=hy_f_b2[l][None, :],
            hy_freq=hy_f_freq[l][None, :], hy_w3=hy_f_w3[l], hy_decay=hy_decay[l][None, :],
            hy_d=hy_d[l][:, None, :],
            w_out=w_out[l].astype(bf16), ln1_g=ln1_g[l][None, :], ln1_b=ln1_b[l][None, :],
            w_up=w_up[l].astype(bf16), w_down=w_down[l].astype(bf16),
            ln2_g=ln2_g[l][None, :], ln2_b=ln2_b[l][None, :],
        )
        yp, st = _layer_group(yp, mod, l, p, None, nbatch=bp, seq=lp, rows=1, tm=tm_ctx,
                              row_of=lambda tile: (lambda i: 0), want_state=True, alpha=alpha)
        outs.append(st)
        ys, _ = _layer_group(ys, mod, l, p, lat_state, nbatch=bs, seq=ls, rows=ls // GRID_W, tm=tm_lat,
                             row_of=lambda tile: (lambda i: 1 + i // (ls // tile)), want_state=False, alpha=alpha)

    new_gla = jnp.stack([jnp.swapaxes(o[0], -1, -2) for o in outs], axis=1)
    new_ret = jnp.stack([jnp.swapaxes(o[1], -1, -2) for o in outs], axis=1)
    unpack = lambda h: jnp.transpose(h, (2, 0, 1, 3)).reshape(bp, 2, ngroup, S5_STATE)
    new_re = jnp.stack([unpack(o[2]) for o in outs], axis=1)
    new_im = jnp.stack([unpack(o[3]) for o in outs], axis=1)
    return (yp.reshape(bp, lp, d), ys.reshape(bs, ls, d), new_gla, new_ret, new_re, new_im)
```

```python
import functools
import math

import numpy as np
import jax
import jax.numpy as jnp
from jax import lax
from jax.experimental import pallas as pl
from jax.experimental.pallas import tpu as pltpu

f32 = jnp.float32
bf16 = jnp.bfloat16

GRID_W = 64
HEADS = 4
DK = 64
DV = 128
GLA_RANK = 16
GLA_TAU = 16.0
S5_GROUP = 16
S5_STATE = 64
HY_BANDS = 16
HY_SHORT = 3
CHUNK = 64
RET_CHUNK = 128
LN_EPS = 1e-5
NORM_EPS = 1e-6

LANES = 128
SUBLANES = 8
VMEM_LIMIT = 52 * 1024 * 1024

GRP = HEADS * DV
OFF_GLA = 0
OFF_RET = 3 * GRP
OFF_S5 = 6 * GRP
OFF_HY = 7 * GRP
OFF_LR = 10 * GRP
NP_IN = 5376
TN_IN = 1792
DENSE_TM = 512
MLP_TF = 1024


def _cparams(sem, vmem=VMEM_LIMIT):
    return pltpu.CompilerParams(dimension_semantics=sem, vmem_limit_bytes=vmem)


def _ln(x):
    mu = jnp.mean(x, axis=-1, keepdims=True)
    xc = x - mu
    var = jnp.mean(xc * xc, axis=-1, keepdims=True)
    return xc * lax.rsqrt(var + LN_EPS)


def _silu(x):
    return x * jax.nn.sigmoid(x)


def _bdot(a, b):
    return jnp.dot(a.astype(bf16), b.astype(bf16), preferred_element_type=f32)


def _bdot_nt(a, b):
    return lax.dot_general(a.astype(bf16), b.astype(bf16), (((1,), (1,)), ((), ())), preferred_element_type=f32)


def _bdot_tn(a, b):
    return lax.dot_general(a.astype(bf16), b.astype(bf16), (((0,), (0,)), ((), ())), preferred_element_type=f32)


def _dot_exact_lhs(a_bf, x):
    hi = x.astype(bf16)
    lo = (x - hi.astype(f32)).astype(bf16)
    d = lambda b: jnp.dot(a_bf, b, preferred_element_type=f32)
    return d(hi) + d(lo)


def _dot3(a, b):
    ah = a.astype(bf16)
    al = (a - ah.astype(f32)).astype(bf16)
    bh = b.astype(bf16)
    bl = (b - bh.astype(f32)).astype(bf16)
    d = lambda p, q: jnp.dot(p, q, preferred_element_type=f32)
    return d(ah, bh) + (d(al, bh) + d(ah, bl))


def _ada_kernel(c_ref, w_ref, b_ref, o_ref):
    s = _silu(c_ref[...])
    o_ref[...] = _bdot(s, w_ref[...]) + b_ref[...]


def _ada(c_rows, ada_w, ada_b):
    depth, d, n = ada_w.shape
    r = c_rows.shape[0]
    tn = 1024
    return pl.pallas_call(
        _ada_kernel,
        out_shape=jax.ShapeDtypeStruct((depth, r, n), f32),
        grid=(depth, n // tn),
        in_specs=[pl.BlockSpec((r, d), lambda l, j: (0, 0)),
                  pl.BlockSpec((None, d, tn), lambda l, j: (l, 0, j)),
                  pl.BlockSpec((None, 1, tn), lambda l, j: (l, 0, j))],
        out_specs=pl.BlockSpec((None, r, tn), lambda l, j: (l, 0, j)),
        compiler_params=_cparams(("parallel", "parallel")),
        name="ada",
    )(c_rows, ada_w, ada_b.reshape(depth, 1, n))


def _in_kernel(x_ref, mod_ref, w_ref, o_ref, u_ref, hy_ref, h_ref, *, nsub, sub):
    j = pl.program_id(1)

    @pl.when(j == 0)
    def _():
        y = _ln(x_ref[...])
        h_ref[...] = (y * (1.0 + mod_ref[1:2, :]) + mod_ref[0:1, :]).astype(bf16)

    cols = pl.ds(pl.multiple_of(j * TN_IN, LANES), TN_IN)
    o_ref[...] = jnp.dot(h_ref[...], w_ref[:, cols], preferred_element_type=f32)

    @pl.when(j == OFF_S5 // TN_IN)
    def _():
        c0 = OFF_S5 % TN_IN
        for k in range(nsub):
            u_ref[:, k * GRP:(k + 1) * GRP] = o_ref[k * sub:(k + 1) * sub, c0:c0 + GRP]

    @pl.when(j == OFF_HY // TN_IN)
    def _():
        c0 = OFF_HY % TN_IN
        hy_ref[...] = o_ref[:, c0:c0 + 3 * GRP]


def _in_proj(x, mod, layer, row_fn, w_perm, tm, seq):
    t, d = x.shape
    assert OFF_S5 // TN_IN == (OFF_S5 + GRP - 1) // TN_IN
    assert OFF_HY // TN_IN == (OFF_HY + 3 * GRP - 1) // TN_IN
    sub = min(tm, seq)
    nsub = tm // sub
    tps = seq // sub
    return pl.pallas_call(
        functools.partial(_in_kernel, nsub=nsub, sub=sub),
        out_shape=[jax.ShapeDtypeStruct((t, NP_IN), f32), jax.ShapeDtypeStruct((seq, (t // seq) * GRP), f32),
                   jax.ShapeDtypeStruct((t, 3 * GRP), f32)],
        grid=(t // tm, NP_IN // TN_IN),
        in_specs=[pl.BlockSpec((tm, d), lambda i, j: (i, 0)),
                  pl.BlockSpec((None, None, 6, d), lambda i, j: (layer, row_fn(i), 0, 0)),
                  pl.BlockSpec((d, NP_IN), lambda i, j: (0, 0), pipeline_mode=pl.Buffered(1))],
        out_specs=[pl.BlockSpec((tm, TN_IN), lambda i, j: (i, j)),
                   pl.BlockSpec((sub, nsub * GRP), lambda i, j: (i % tps, i // tps)),
                   pl.BlockSpec((tm, 3 * GRP), lambda i, j: (i, 0))],
        scratch_shapes=[pltpu.VMEM((tm, d), bf16)],
        compiler_params=_cparams(("parallel", "arbitrary")),
        name="in_proj",
    )(x, mod, w_perm)


LA_TAIL = 256


def _linattn_kernel(*refs, mode, seq, has_s0, want_state):
    it = iter(refs)
    qk_ref, v_ref, g_ref = next(it), next(it), next(it)
    if mode == "gla":
        lr_ref, wg_ref, bg_ref, nw_ref = next(it), next(it), next(it), next(it)
    else:
        dexp_ref = next(it)
    s0_ref = next(it) if has_s0 else None
    o_ref = next(it)
    sf_ref = next(it) if want_state else None
    ob_ref, st_ref = next(it), next(it)
    if mode == "gla":
        qkd_ref, ku_ref, dec_ref = next(it), next(it), next(it)

    c = CHUNK if mode == "gla" else RET_CHUNK
    nchunk = seq // c
    scale = DK ** -0.5
    row = lax.broadcasted_iota(jnp.int32, (c, c), 0)
    col = lax.broadcasted_iota(jnp.int32, (c, c), 1)
    is_k = (lax.broadcasted_iota(jnp.int32, (1, GRP), 1) & DK) != 0
    hq = lambda a, h: a[:, h * LANES:h * LANES + DK]
    hk = lambda a, h: a[:, h * LANES + DK:(h + 1) * LANES]
    hv = lambda h: slice(h * DV, (h + 1) * DV)

    if has_s0:
        st_ref[...] = s0_ref[...]
    else:
        st_ref[...] = jnp.zeros_like(st_ref)

    per_dir = []
    for d in (0, 1):
        causal = (row >= col) if d == 0 else (row <= col)
        if mode == "gla":
            tri = jnp.where(causal, 1.0, 0.0).astype(bf16)
            sgn = jnp.where(is_k, -1.0, 1.0)
            qsc = jnp.where(is_k, 1.0, scale)
            per_dir.append((causal, tri, sgn, qsc))
        else:
            lgr = jnp.log1p(-jnp.exp2(-dexp_ref[d]))[0:1, :]
            rowg = lax.broadcasted_iota(jnp.int32, (c, GRP), 0)
            pos = rowg if d == 0 else (c - 1 - rowg)
            pw = jnp.where(is_k, c - 1 - pos, pos + 1).astype(f32)
            wqk = jnp.exp(lgr * pw) * jnp.where(is_k, scale, 1.0)
            dist = ((row - col) if d == 0 else (col - row)).astype(f32)
            masks = [jnp.where(causal, jnp.exp(jnp.broadcast_to(lgr[:, h * LANES:h * LANES + c], (c, c)) * dist), 0.0) * scale
                     for h in range(HEADS)]
            dec = jnp.exp(lgr * float(c))
            per_dir.append((wqk, masks, dec))

    if mode == "gla":
        def pre(n, carry):
            rows = pl.ds(pl.multiple_of(n * c, c), c)
            qk = qk_ref[rows, :]
            lrc = lr_ref[rows, :]
            logits = [_bdot(lrc, wg_ref[d]) + bg_ref[d] for d in (0, 1)]
            la = [(jnp.minimum(x, 0.0) - jnp.log(1.0 + jnp.exp(-jnp.abs(x)))) * (1.0 / GLA_TAU) for x in logits]
            b = [_dot_exact_lhs(per_dir[d][1], la[d]) for d in (0, 1)]
            for d in (0, 1):
                _, _, sgn, qsc = per_dir[d]
                last = c - 1 if d == 0 else 0
                btot = b[d][last:last + 1, :]
                qkd_ref[d, rows, :] = (qk * (jnp.exp(b[d] * sgn) * qsc)).astype(bf16)
                ku_ref[d, rows, :] = (qk * jnp.exp(btot - b[d])).astype(bf16)
                dec_ref[d, n] = jnp.exp(btot)
            return carry

        lax.fori_loop(0, nchunk, pre, 0, unroll=2)

    def body(i, carry):
        rows, qka, qko, ku, dec = [], [], [], [], []
        for d in (0, 1):
            n = i if d == 0 else nchunk - 1 - i
            rows.append(pl.ds(pl.multiple_of(n * c, c), c))
            if mode == "gla":
                qka.append(qkd_ref[d, rows[d], :])
                qko.append(qka[d])
                ku.append(ku_ref[d, rows[d], :])
                dec.append(dec_ref[d, n])
            else:
                wqk, _, dcy = per_dir[d]
                qka.append(qk_ref[rows[d], :])
                qko.append(qka[d] * wqk)
                ku.append(qko[d])
                dec.append(dcy)
        chains = [(d, h) for d in (0, 1) for h in range(HEADS)]
        v = {(d, h): v_ref[rows[d], hv(h)] for d, h in chains}
        st = {(d, h): st_ref[d, h] for d, h in chains}
        att = {(d, h): _bdot_nt(hq(qka[d], h), hk(qka[d], h)) for d, h in chains}
        oi = {(d, h): _bdot_nt(hq(qko[d], h), st[d, h]) for d, h in chains}
        up = {(d, h): _bdot_tn(v[d, h], hk(ku[d], h)) for d, h in chains}
        for d, h in chains:
            if mode == "gla":
                a = jnp.where(per_dir[d][0], att[d, h], 0.0)
            else:
                a = att[d, h] * per_dir[d][1][h]
            o = _bdot(a, v[d, h]) + oi[d, h]
            st_ref[d, h] = hq(dec[d], h) * st[d, h] + up[d, h]
            if d == 0:
                o_ref[rows[d], hv(h)] = o
            else:
                ob_ref[rows[d], hv(h)] = o
        return carry

    lax.fori_loop(0, nchunk, body, 0, unroll=4 if mode == "gla" else 2)
    if want_state:
        sf_ref[...] = st_ref[...]

    tl = min(LA_TAIL, seq)

    def tail(i, carry):
        rows = pl.ds(pl.multiple_of(i * tl, tl), tl)
        for h in range(HEADS):
            o = o_ref[rows, hv(h)] + ob_ref[rows, hv(h)]
            if mode == "gla":
                o = o * lax.rsqrt(jnp.mean(o * o, axis=-1, keepdims=True) + NORM_EPS) * nw_ref[:, hv(h)]
            else:
                o = _ln(o)
            o_ref[rows, hv(h)] = o * _silu(g_ref[rows, hv(h)])
        return carry

    lax.fori_loop(0, seq // tl, tail, 0)


def _linattn(proj, col0, extra, s0t, layer, *, mode, nbatch, seq, want_state):
    cb = col0 // GRP
    full = lambda a: pl.BlockSpec(a.shape, lambda b: (0,) * a.ndim)
    in_specs = [pl.BlockSpec((seq, GRP), lambda b: (b, cb)),
                pl.BlockSpec((seq, GRP), lambda b: (b, cb + 1)),
                pl.BlockSpec((seq, GRP), lambda b: (b, cb + 2))]
    args = [proj, proj, proj]
    scratch = [pltpu.VMEM((seq, GRP), f32), pltpu.VMEM((2, HEADS, DV, DK), f32)]
    if mode == "gla":
        scratch += [pltpu.VMEM((2, seq, GRP), bf16), pltpu.VMEM((2, seq, GRP), bf16),
                    pltpu.VMEM((2, seq // CHUNK, 1, GRP), f32)]
    if mode == "gla":
        in_specs += [pl.BlockSpec((seq, LANES), lambda b: (b, OFF_LR // LANES))] + [full(a) for a in extra]
        args += [proj, *extra]
    else:
        in_specs += [full(a) for a in extra]
        args += list(extra)
    has_s0 = s0t is not None
    if has_s0:
        in_specs += [pl.BlockSpec((None, None, 2, HEADS, DV, DK), lambda b: (b, layer, 0, 0, 0, 0))]
        args += [s0t]
    out_shape = [jax.ShapeDtypeStruct((nbatch * seq, GRP), f32)]
    out_specs = [pl.BlockSpec((seq, GRP), lambda b: (b, 0))]
    if want_state:
        out_shape += [jax.ShapeDtypeStruct((nbatch, 2, HEADS, DV, DK), f32)]
        out_specs += [pl.BlockSpec((None, 2, HEADS, DV, DK), lambda b: (b, 0, 0, 0, 0))]
    res = pl.pallas_call(
        functools.partial(_linattn_kernel, mode=mode, seq=seq, has_s0=has_s0, want_state=want_state),
        out_shape=out_shape,
        grid=(nbatch,),
        in_specs=in_specs,
        out_specs=out_specs,
        scratch_shapes=scratch,
        compiler_params=_cparams(("parallel",)),
        name="linattn_" + mode,
    )(*args)
    return (res[0], res[1]) if want_state else (res[0], None)


def _s5_prep_kernel(are_ref, aim_ref, lstep_ref, bre_ref, bim_ref, lre_ref, lim_ref, bbre_ref, bbim_ref):
    ar, ai = are_ref[...], aim_ref[...]
    st = jnp.exp(lstep_ref[...])
    mag = jnp.exp(ar * st)
    lr = mag * jnp.cos(ai * st)
    li = mag * jnp.sin(ai * st)
    nr, ni = lr - 1.0, li
    den = ar * ar + ai * ai
    kr = (nr * ar + ni * ai) / den
    ki = (ni * ar - nr * ai) / den
    br, bi = bre_ref[...], bim_ref[...]
    lre_ref[...] = lr
    lim_ref[...] = li
    bbre_ref[...] = kr * br - ki * bi
    bbim_ref[...] = kr * bi + ki * br


def _s5_prep(a_re, a_im, log_step, b_re, b_im):
    _, g, p = a_re.shape
    hch = b_re.shape[-1]
    rows = 2 * g * hch
    bc = lambda a: jnp.broadcast_to(a[:, :, None, :], (2, g, hch, p)).reshape(rows, p)
    tr = lambda b: jnp.transpose(b, (0, 1, 3, 2)).reshape(rows, p)
    ls = jnp.broadcast_to(log_step[:, :, None, None], (2, g, hch, p)).reshape(rows, p)
    spec = pl.BlockSpec((rows, p), lambda: (0, 0))
    lre, lim, bbre, bbim = pl.pallas_call(
        _s5_prep_kernel,
        out_shape=[jax.ShapeDtypeStruct((rows, p), f32)] * 4,
        in_specs=[spec] * 5,
        out_specs=[spec] * 4,
        name="s5_prep",
    )(bc(a_re), bc(a_im), ls, tr(b_re), tr(b_im))
    r4 = lambda a: a.reshape(2, g, hch, p)
    return r4(lre)[:, :, 0, :], r4(lim)[:, :, 0, :], r4(bbre), r4(bbim)


S5_TC = 64
S5_GPT = LANES // S5_GROUP
S5_SW = S5_GPT * S5_STATE


def _s5_scan_kernel(*refs, seq, nb, has_h0, want_state):
    it = iter(refs)
    u_ref, wb_ref, lam_ref, cre_ref, cim_ref, dsk_ref = (next(it) for _ in range(6))
    h0re_ref = next(it) if has_h0 else None
    h0im_ref = next(it) if has_h0 else None
    y_ref = next(it)
    hfre_ref = next(it) if want_state else None
    hfim_ref = next(it) if want_state else None
    bu_ref, hs_ref = next(it), next(it)

    tc = S5_TC
    nchunk = seq // tc
    sw = S5_SW

    def skip(i, carry):
        rows = pl.ds(pl.multiple_of(i * tc, tc), tc)
        y_ref[rows] = u_ref[rows] * dsk_ref[...]
        return carry

    lax.fori_loop(0, nchunk, skip, 0)

    lam = [(jnp.broadcast_to(lam_ref[d, 0:1, :], (nb, sw)), jnp.broadcast_to(lam_ref[d, 1:2, :], (nb, sw)))
           for d in (0, 1)]

    def chunk(i, carry):
        h = [list(carry[0:2]), list(carry[2:4])]
        t0 = [pl.multiple_of(i * tc, tc), pl.multiple_of((nchunk - 1 - i) * tc, tc)]
        for d in (0, 1):
            ub = u_ref[pl.ds(t0[d], tc)]
            bu_ref[d] = _bdot(ub.reshape(tc * nb, LANES), wb_ref[d])
        for s in range(tc):
            for d in (0, 1):
                t = s if d == 0 else tc - 1 - s
                rs = slice(t * nb, (t + 1) * nb)
                lam_r, lam_i = lam[d]
                hre, him = h[d]
                nre = lam_r * hre - lam_i * him + bu_ref[d, rs, 0:sw]
                nim = lam_r * him + lam_i * hre + bu_ref[d, rs, sw:2 * sw]
                h[d] = [nre, nim]
                hs_ref[d, rs, 0:sw] = nre
                hs_ref[d, rs, sw:2 * sw] = nim
        for d in (0, 1):
            y = _bdot(hs_ref[d, :, 0:sw], cre_ref[d]) - _bdot(hs_ref[d, :, sw:2 * sw], cim_ref[d])
            rows = pl.ds(t0[d], tc)
            y_ref[rows] = y_ref[rows] + y.reshape(tc, nb, LANES)
        return (*h[0], *h[1])

    if has_h0:
        h0 = (h0re_ref[0], h0im_ref[0], h0re_ref[1], h0im_ref[1])
    else:
        h0 = (jnp.zeros((nb, sw), f32),) * 4
    hfin = lax.fori_loop(0, nchunk, chunk, h0)
    if want_state:
        for d in (0, 1):
            hfre_ref[d] = hfin[2 * d]
            hfim_ref[d] = hfin[2 * d + 1]


def _s5_scan(u_t, wb, lam, cre, cim, dsk, h0, *, want_state):
    seq, nb, ch = u_t.shape
    ntile = ch // LANES
    sw = S5_SW
    in_specs = [pl.BlockSpec((seq, nb, LANES), lambda j: (0, 0, j)),
                pl.BlockSpec((2, None, LANES, 2 * sw), lambda j: (0, j, 0, 0)),
                pl.BlockSpec((2, None, 2, sw), lambda j: (0, j, 0, 0)),
                pl.BlockSpec((2, None, sw, LANES), lambda j: (0, j, 0, 0)),
                pl.BlockSpec((2, None, sw, LANES), lambda j: (0, j, 0, 0)),
                pl.BlockSpec((1, LANES), lambda j: (0, j))]
    args = [u_t, wb, lam, cre, cim, dsk]
    has_h0 = h0 is not None
    st_spec = pl.BlockSpec((2, None, nb, sw), lambda j: (0, j, 0, 0))
    if has_h0:
        in_specs += [st_spec, st_spec]
        args += list(h0)
    out_shape = [jax.ShapeDtypeStruct((seq, nb, ch), f32)]
    out_specs = [pl.BlockSpec((seq, nb, LANES), lambda j: (0, 0, j))]
    if want_state:
        out_shape += [jax.ShapeDtypeStruct((2, ntile, nb, sw), f32)] * 2
        out_specs += [st_spec, st_spec]
    res = pl.pallas_call(
        functools.partial(_s5_scan_kernel, seq=seq, nb=nb, has_h0=has_h0, want_state=want_state),
        out_shape=out_shape,
        grid=(ntile,),
        in_specs=in_specs,
        out_specs=out_specs,
        scratch_shapes=[pltpu.VMEM((2, S5_TC * nb, 2 * sw), f32), pltpu.VMEM((2, S5_TC * nb, 2 * sw), f32)],
        compiler_params=_cparams(("parallel",)),
        name="s5_scan",
    )(*args)
    return (res[0], res[1], res[2]) if want_state else (res[0], None, None)


HY_TK = 256
HY_CT = 256


def _pk_mul(xr, xi, a, b, d):
    return xr * a - xi * b, xr * b + xi * d


def _hy_filter_kernel(fte_ref, fto_ref, w1_ref, b1_ref, w2_ref, b2_ref, fq_ref, w3f_ref, w3b_ref, dcf_ref, dcb_ref,
                      fre_ref, fim_ref, cph_ref, sph_ref,
                      fea_ref, feb_ref, foa_ref, fob_ref, ga_ref, gb_ref, ny_ref, xs_ref, *, lh, tk):
    j = pl.program_id(1)
    ct = HY_CT

    @pl.when(j == 0)
    def _():
        fq = fq_ref[...]

        def hidden(feat):
            h = jnp.sin(fq * (_dot3(feat, w1_ref[...]) + b1_ref[...]))
            return jnp.sin(fq * (_dot3(h, w2_ref[...]) + b2_ref[...])), feat[:, 0:1]

        he, te = hidden(fte_ref[...])
        ho, to = hidden(fto_ref[...])
        raw = lambda h, t, w3_ref, dc_ref: _dot3(h, w3_ref[...]) * jnp.exp(-t * jnp.abs(dc_ref[...]))
        ffe, ffo = raw(he, te, w3f_ref, dcf_ref), raw(ho, to, w3f_ref, dcf_ref)
        fbe, fbo = raw(he, te, w3b_ref, dcb_ref), raw(ho, to, w3b_ref, dcb_ref)
        sf = jnp.sum(jnp.abs(ffe), axis=0, keepdims=True) + jnp.sum(jnp.abs(ffo), axis=0, keepdims=True)
        sb = jnp.sum(jnp.abs(fbe), axis=0, keepdims=True) + jnp.sum(jnp.abs(fbo), axis=0, keepdims=True)
        rowi = lax.broadcasted_iota(jnp.int32, (lh, ct), 0)
        xs_ref[:, 0 * ct:1 * ct] = (ffe / sf).astype(bf16)
        xs_ref[:, 1 * ct:2 * ct] = jnp.where(rowi == 0, 0.0, fbe / sb).astype(bf16)
        xs_ref[:, 2 * ct:3 * ct] = (ffo / sf).astype(bf16)
        xs_ref[:, 3 * ct:4 * ct] = (fbo / sb).astype(bf16)

    xs = xs_ref[...]
    r = jnp.dot(fre_ref[...], xs, preferred_element_type=f32)
    i = jnp.dot(fim_ref[...], xs, preferred_element_type=f32)
    r1, r2, r3, r4 = (r[:, q * ct:(q + 1) * ct] for q in range(4))
    i1, i2, i3, i4 = (i[:, q * ct:(q + 1) * ct] for q in range(4))
    c, s = cph_ref[...], sph_ref[...]
    fe_re, fe_im = r1 + r2, i1 - i2
    fo_re = r3 + c * r4 + s * i4
    fo_im = i3 + s * r4 - c * i4
    g_re = c * fo_re + s * fo_im
    g_im = c * fo_im - s * fo_re
    scl = 1.0 / lh
    first = jnp.logical_and(lax.broadcasted_iota(jnp.int32, (tk, ct), 0) == 0, j == 0)
    for a_ref, b_ref, re, im in ((fea_ref, feb_ref, fe_re, fe_im), (foa_ref, fob_ref, fo_re, fo_im),
                                 (ga_ref, gb_ref, g_re, g_im)):
        a_ref[...] = jnp.where(first, 0.5 * scl, scl) * re
        b_ref[...] = jnp.where(first, 0.0, scl * im)

    @pl.when(j == 0)
    def _():
        fo_ny = i3[0:1, :] - i4[0:1, :]
        ny_ref[...] = jnp.zeros_like(ny_ref)
        ny_ref[0:1, :] = (0.5 * scl) * (i1[0:1, :] + i2[0:1, :])
        ny_ref[1:2, :] = (0.5 * scl) * fo_ny
        ny_ref[2:3, :] = (-0.5 * scl) * fo_ny


def _hy_filter(feat_e, feat_o, w1p, b1, w2, b2, fq, w3, dc, fre, fim, cph, sph):
    lh = feat_e.shape[0]
    ncol = w3.shape[1] // 2
    nct = ncol // HY_CT
    tk = min(HY_TK, lh)
    full = lambda a: pl.BlockSpec(a.shape, lambda c, j: (0,) * a.ndim)
    out_spec = pl.BlockSpec((tk, HY_CT), lambda c, j: (j, c))
    return pl.pallas_call(
        functools.partial(_hy_filter_kernel, lh=lh, tk=tk),
        out_shape=[jax.ShapeDtypeStruct((lh, ncol), f32)] * 6 + [jax.ShapeDtypeStruct((SUBLANES, ncol), f32)],
        grid=(nct, lh // tk),
        in_specs=[full(feat_e), full(feat_o), full(w1p), full(b1), full(w2), full(b2), full(fq),
                  pl.BlockSpec((w3.shape[0], HY_CT), lambda c, j: (0, c)),
                  pl.BlockSpec((w3.shape[0], HY_CT), lambda c, j: (0, nct + c)),
                  pl.BlockSpec((1, HY_CT), lambda c, j: (0, c)),
                  pl.BlockSpec((1, HY_CT), lambda c, j: (0, nct + c)),
                  pl.BlockSpec((tk, lh), lambda c, j: (j, 0)),
                  pl.BlockSpec((tk, lh), lambda c, j: (j, 0)),
                  pl.BlockSpec((tk, HY_CT), lambda c, j: (j, 0)),
                  pl.BlockSpec((tk, HY_CT), lambda c, j: (j, 0))],
        out_specs=[out_spec] * 6 + [pl.BlockSpec((SUBLANES, HY_CT), lambda c, j: (0, c))],
        scratch_shapes=[pltpu.VMEM((lh, 4 * HY_CT), bf16)],
        compiler_params=_cparams(("parallel", "arbitrary")),
        name="hy_filter",
    )(feat_e, feat_o, w1p, b1, w2, b2, fq, w3, w3, dc, dc, fre, fim, cph, sph)


def _short_conv(x, w_ref, b_ref, seg):
    rows = x.shape[0]
    r = lax.broadcasted_iota(jnp.int32, x.shape, 0) & (seg - 1)
    xp = jnp.where(r == 0, 0.0, pltpu.roll(x, 1, 0))
    xn = jnp.where(r == seg - 1, 0.0, pltpu.roll(x, rows - 1, 0))
    return xp * w_ref[0:1, :] + x * w_ref[1:2, :] + xn * w_ref[2:3, :] + b_ref[...]


HY_PB = 256


def _longconv_kernel(yin_ref, gate_ref, cwy_ref, cby_ref, cwg_ref, cbg_ref, fre_ref, fim_ref, iim_ref,
                     fea_ref, feb_ref, foa_ref, fob_ref, ga_ref, gb_ref, ny_ref, dsk_ref,
                     o_ref, y32_ref, ybf_ref, acc_ref, *, seg, conv_yin, tk):
    j = pl.program_id(1)
    g = GRP
    seq = yin_ref.shape[0]
    pb = min(HY_PB, seq)
    hb = pb // 2
    pr = lax.broadcasted_iota(jnp.int32, (pb, pb), 0)
    pc = lax.broadcasted_iota(jnp.int32, (pb, pb), 1)

    @pl.when(j == 0)
    def _():
        y = yin_ref[...]
        if conv_yin:
            y = _short_conv(y, cwy_ref, cby_ref, seg)
        y32_ref[...] = y
        split = jnp.where(pc == jnp.where(pr < hb, 2 * pr, 2 * (pr - hb) + 1), 1.0, 0.0).astype(bf16)
        for k in range(seq // pb):
            t = jnp.dot(split, y[k * pb:(k + 1) * pb, :].astype(bf16), preferred_element_type=f32).astype(bf16)
            ybf_ref[k * hb:(k + 1) * hb, 0:g] = t[0:hb, :]
            ybf_ref[k * hb:(k + 1) * hb, g:2 * g] = t[hb:pb, :]
        acc_ref[...] = jnp.zeros_like(acc_ref)

    freq = pl.ds(pl.multiple_of(j * tk, tk), tk)
    yb = ybf_ref[...]
    r = jnp.dot(fre_ref[freq, :], yb, preferred_element_type=f32)
    i = jnp.dot(fim_ref[freq, :], yb, preferred_element_type=f32)
    first = jnp.logical_and(lax.broadcasted_iota(jnp.int32, (tk, g), 0) == 0, j == 0)
    fe_a, fo_a, g_a = fea_ref[...], foa_ref[...], ga_ref[...]
    f_e = (fe_a, feb_ref[...], jnp.where(first, ny_ref[0:1, :], fe_a))
    f_o = (fo_a, fob_ref[...], jnp.where(first, ny_ref[1:2, :], fo_a))
    f_g = (g_a, gb_ref[...], jnp.where(first, ny_ref[2:3, :], g_a))
    er, ei, orr, oi = r[:, 0:g], i[:, 0:g], r[:, g:2 * g], i[:, g:2 * g]
    p1, p2 = _pk_mul(er, ei, *f_e), _pk_mul(orr, oi, *f_g)
    p3, p4 = _pk_mul(er, ei, *f_o), _pk_mul(orr, oi, *f_e)
    zr = jnp.concatenate([p1[0] + p2[0], p3[0] + p4[0]], axis=1).astype(bf16)
    zi = jnp.concatenate([p1[1] + p2[1], p3[1] + p4[1]], axis=1).astype(bf16)
    acc_ref[...] += (jnp.dot(fre_ref[:, freq], zr, preferred_element_type=f32)
                     + jnp.dot(iim_ref[:, freq], zi, preferred_element_type=f32))

    @pl.when(j == pl.num_programs(1) - 1)
    def _():
        gate = _short_conv(gate_ref[...], cwg_ref, cbg_ref, seg)
        d = dsk_ref[...]
        merge = jnp.where(pr == jnp.where(pc < hb, 2 * pc, 2 * (pc - hb) + 1), 1.0, 0.0).astype(bf16)
        for k in range(seq // pb):
            stacked = jnp.concatenate([acc_ref[k * hb:(k + 1) * hb, 0:g], acc_ref[k * hb:(k + 1) * hb, g:2 * g]], axis=0)
            conv = _dot_exact_lhs(merge, stacked)
            rows = slice(k * pb, (k + 1) * pb)
            o_ref[rows, :] = gate[rows, :] * (conv + d * y32_ref[rows, :])


def _longconv(yin, yin_cb, gate_arr, gate_cb, cw, cb, part_y, part_g, fre, fim, iim, filt, dsk, order,
              *, nbatch, seq, seg, conv_yin):
    lh = seq // 2
    tk = min(HY_TK, lh)
    once = pl.Buffered(1)
    cw_spec = lambda p: pl.BlockSpec((HY_SHORT, GRP), lambda b, j: (0, p))
    cb_spec = lambda p: pl.BlockSpec((1, GRP), lambda b, j: (0, p))
    dft_spec = pl.BlockSpec((lh, lh), lambda b, j: (0, 0), pipeline_mode=once)
    filt_spec = pl.BlockSpec((tk, GRP), lambda b, j: (j, order))
    return pl.pallas_call(
        functools.partial(_longconv_kernel, seg=seg, conv_yin=conv_yin, tk=tk),
        out_shape=jax.ShapeDtypeStruct((nbatch * seq, GRP), f32),
        grid=(nbatch, lh // tk),
        in_specs=[pl.BlockSpec((seq, GRP), lambda b, j: (b, yin_cb)),
                  pl.BlockSpec((seq, GRP), lambda b, j: (b, gate_cb)),
                  cw_spec(part_y), cb_spec(part_y), cw_spec(part_g), cb_spec(part_g),
                  dft_spec, dft_spec, dft_spec,
                  filt_spec, filt_spec, filt_spec, filt_spec, filt_spec, filt_spec,
                  pl.BlockSpec((SUBLANES, GRP), lambda b, j: (0, order)),
                  pl.BlockSpec((None, 1, GRP), lambda b, j: (order, 0, 0))],
        out_specs=pl.BlockSpec((seq, GRP), lambda b, j: (b, 0)),
        scratch_shapes=[pltpu.VMEM((seq, GRP), f32), pltpu.VMEM((lh, 2 * GRP), bf16), pltpu.VMEM((lh, 2 * GRP), f32)],
        compiler_params=_cparams(("parallel", "arbitrary")),
        name="hy_longconv",
    )(yin, gate_arr, cw, cb, cw, cb, fre, fim, iim, *filt, dsk)


def _out_kernel(m0_ref, m1_ref, y2_ref, m3_ref, gw_ref, gb_ref, x_ref, mod_ref, w_ref, g_ref, b_ref, o_ref, *, alpha):
    y = y2_ref[...]
    z = y * (0.5 * (1.0 + jnp.tanh(math.sqrt(2.0 / math.pi) * (y + 0.044715 * (y * y * y)))))
    m2 = z * jax.nn.sigmoid(_bdot(z, gw_ref[...]) + gb_ref[...])
    acc = _bdot(m0_ref[...], w_ref[0 * GRP:1 * GRP, :])
    acc += _bdot(m1_ref[...], w_ref[1 * GRP:2 * GRP, :])
    acc += _bdot(m2, w_ref[2 * GRP:3 * GRP, :])
    acc += _bdot(m3_ref[...], w_ref[3 * GRP:4 * GRP, :])
    r = alpha * x_ref[...] + mod_ref[2:3, :] * acc
    o_ref[...] = _ln(r) * g_ref[...] + b_ref[...]


def _out_proj(mixes, s5_spec, glu_w, glu_b, x, mod, layer, row_fn, w_out, g, b, tm, alpha):
    t, d = x.shape
    mspec = pl.BlockSpec((tm, GRP), lambda i: (i, 0))
    const = lambda a: pl.BlockSpec(a.shape, lambda i: (0,) * a.ndim)
    return pl.pallas_call(
        functools.partial(_out_kernel, alpha=alpha),
        out_shape=jax.ShapeDtypeStruct((t, d), f32),
        grid=(t // tm,),
        in_specs=[mspec, mspec, s5_spec, mspec, const(glu_w), const(glu_b),
                  pl.BlockSpec((tm, d), lambda i: (i, 0)),
                  pl.BlockSpec((None, None, 6, d), lambda i: (layer, row_fn(i), 0, 0)),
                  pl.BlockSpec((d, d), lambda i: (0, 0), pipeline_mode=pl.Buffered(1)),
                  pl.BlockSpec((1, d), lambda i: (0, 0)),
                  pl.BlockSpec((1, d), lambda i: (0, 0))],
        out_specs=pl.BlockSpec((tm, d), lambda i: (i, 0)),
        compiler_params=_cparams(("parallel",)),
        name="out_proj",
    )(*mixes, glu_w, glu_b, x, mod, w_out, g, b)


def _mlp_kernel(x_ref, mod_ref, wu_ref, wd_ref, g_ref, b_ref, o_ref, h_ref, acc_ref, *, alpha):
    j = pl.program_id(1)

    @pl.when(j == 0)
    def _():
        y = _ln(x_ref[...])
        h_ref[...] = (y * (1.0 + mod_ref[4:5, :]) + mod_ref[3:4, :]).astype(bf16)
        acc_ref[...] = jnp.zeros_like(acc_ref)

    u = jnp.maximum(jnp.dot(h_ref[...], wu_ref[...], preferred_element_type=f32), 0.0)
    acc_ref[...] += jnp.dot((u * u).astype(bf16), wd_ref[...], preferred_element_type=f32)

    @pl.when(j == pl.num_programs(1) - 1)
    def _():
        r = alpha * x_ref[...] + mod_ref[5:6, :] * acc_ref[...]
        o_ref[...] = _ln(r) * g_ref[...] + b_ref[...]


def _mlp(x, mod, layer, row_fn, w_up, w_down, g, b, tm, tf, alpha):
    t, d = x.shape
    dff = w_up.shape[1]
    return pl.pallas_call(
        functools.partial(_mlp_kernel, alpha=alpha),
        out_shape=jax.ShapeDtypeStruct((t, d), f32),
        grid=(t // tm, dff // tf),
        in_specs=[pl.BlockSpec((tm, d), lambda i, j: (i, 0)),
                  pl.BlockSpec((None, None, 6, d), lambda i, j: (layer, row_fn(i), 0, 0)),
                  pl.BlockSpec((d, tf), lambda i, j: (0, j)),
                  pl.BlockSpec((tf, d), lambda i, j: (j, 0)),
                  pl.BlockSpec((1, d), lambda i, j: (0, 0)),
                  pl.BlockSpec((1, d), lambda i, j: (0, 0))],
        out_specs=pl.BlockSpec((tm, d), lambda i, j: (i, 0)),
        scratch_shapes=[pltpu.VMEM((tm, d), bf16), pltpu.VMEM((tm, d), f32)],
        compiler_params=_cparams(("parallel", "arbitrary")),
        name="mlp",
    )(x, mod, w_up, w_down, g, b)


def _perm_w_in(w):
    d = w.shape[0]
    hk, hv = HEADS * DK, HEADS * DV
    o = 0
    parts = {}
    for name, width in (("gq", hk), ("gk", hk), ("gv", hv), ("gg", GRP), ("glr", 2 * GLA_RANK),
                        ("rq", hk), ("rk", hk), ("rv", hv), ("rg", GRP), ("su", GRP), ("hy", 3 * GRP)):
        parts[name] = w[:, o:o + width]
        o += width
    qk = lambda q, k: jnp.concatenate([q.reshape(d, HEADS, DK), k.reshape(d, HEADS, DK)], axis=-1).reshape(d, 2 * hk)
    cols = [qk(parts["gq"], parts["gk"]), parts["gv"], parts["gg"],
            qk(parts["rq"], parts["rk"]), parts["rv"], parts["rg"],
            parts["su"], parts["hy"], parts["glr"]]
    wp = jnp.concatenate(cols, axis=1)
    return jnp.pad(wp, ((0, 0), (0, NP_IN - wp.shape[1]))).astype(bf16)


def _gla_gate_params(w_gate, b_gate, norm_w):
    r = w_gate.shape[1]
    dup = lambda a: jnp.concatenate([a, a], axis=-1).reshape(*a.shape[:-2], 2 * HEADS * DK)
    wg = dup(w_gate.reshape(2, r, HEADS, DK))
    full = jnp.zeros((2, LANES, 2 * HEADS * DK), f32)
    for d in range(2):
        full = full.at[d, d * r:(d + 1) * r, :].set(wg[d])
    return full.astype(bf16), dup(b_gate.reshape(2, 1, HEADS, DK)), jnp.tile(norm_w[None, :], (1, HEADS))


def _s5_block_weights(bbre, bbim, c_re, c_im):
    _, g, hch, p = bbre.shape
    ntile = g // S5_GPT
    eye = jnp.eye(S5_GPT, dtype=f32)

    def in_map(b):
        b = b.reshape(2, ntile, S5_GPT, hch, p)
        return jnp.einsum("dtghp,gk->dtghkp", b, eye).reshape(2, ntile, S5_GPT * hch, S5_GPT * p)

    def out_map(c):
        c = c.reshape(2, ntile, S5_GPT, hch, p)
        return jnp.einsum("dtghp,gk->dtgpkh", c, eye).reshape(2, ntile, S5_GPT * p, S5_GPT * hch)

    wb = jnp.concatenate([in_map(bbre), in_map(bbim)], axis=-1).astype(bf16)
    return wb, out_map(c_re).astype(bf16), out_map(c_im).astype(bf16)


@functools.lru_cache(maxsize=None)
def _dft_tables(seq):
    n = 2 * seq
    k = np.arange(seq, dtype=np.int64)[:, None]
    s1 = np.arange(seq // 64, dtype=np.int64)[None, :]
    s2 = np.arange(64, dtype=np.int64)[None, :]
    a = 2.0 * np.pi * ((k * s1 * 64) % n) / n
    b = 2.0 * np.pi * ((k * s2) % n) / n
    return tuple(np.asarray(t, np.float32) for t in (np.cos(a), np.sin(a), np.cos(b), np.sin(b)))


def _dft_mats(seq):
    ca, sa, cb, sb = (jnp.asarray(t) for t in _dft_tables(seq))
    cosm = (ca[:, :, None] * cb[:, None, :] - sa[:, :, None] * sb[:, None, :]).reshape(seq, seq)
    sinm = (sa[:, :, None] * cb[:, None, :] + ca[:, :, None] * sb[:, None, :]).reshape(seq, seq)
    sign = jnp.where((jnp.arange(seq) & 1) == 0, 1.0, -1.0).astype(f32)
    ri = jnp.arange(seq)[:, None]
    ci = jnp.arange(seq)[None, :]
    fim = jnp.where(ri == 0, sign[None, :], -sinm)
    iim = jnp.where(ci == 0, sign[:, None], -sinm)
    return cosm.astype(bf16), fim.astype(bf16), iim.astype(bf16)


@functools.lru_cache(maxsize=None)
def _hy_feats(seq):
    t = np.linspace(0.0, 1.0, seq)[:, None]
    w = 2.0 * math.pi * np.arange(seq, dtype=np.float64)[:, None] / seq
    fr = np.linspace(1e-4, HY_BANDS - 1.0, HY_BANDS)[None, :]
    feats = np.concatenate([t, np.cos(fr * w), -np.sin(fr * w)], axis=-1).astype(np.float32)
    return np.pad(feats, ((0, 0), (0, LANES - feats.shape[1])))


@functools.lru_cache(maxsize=None)
def _hy_twiddles(lh):
    ang = np.pi * np.arange(lh, dtype=np.float64)[:, None] / lh
    return np.cos(ang).astype(np.float32), np.sin(ang).astype(np.float32)


def _layer_group(x, mod, layer, p, state, *, nbatch, seq, rows, tm, row_of, want_state, alpha):
    tm_dense = min(DENSE_TM, nbatch * seq)
    proj, u_t, hy = _in_proj(x, mod, layer, row_of(tm_dense), p["w_in"], tm_dense, seq)

    o_gla, s_gla = _linattn(proj, OFF_GLA, (p["gla_wg"], p["gla_bg"], p["gla_nw"]),
                            None if state is None else state["gla"], layer,
                            mode="gla", nbatch=nbatch, seq=seq, want_state=want_state)
    o_ret, s_ret = _linattn(proj, OFF_RET, (p["ret_dexp"],),
                            None if state is None else state["ret"], layer,
                            mode="ret", nbatch=nbatch, seq=seq, want_state=want_state)

    u_t = u_t.reshape(seq, nbatch, GRP)
    h0 =None if state is None else (state["s5_re"][layer], state["s5_im"][layer])
    y_t, hf_re, hf_im = _s5_scan(u_t, p["s5_wb"], p["s5_lam"], p["s5_cre"], p["s5_cim"], p["s5_d"], h0,
                                 want_state=want_state)
    y_s5 = y_t.reshape(seq, nbatch * GRP)
    tpb = seq // tm
    s5_spec = pl.BlockSpec((tm, GRP), lambda i: (i % tpb, i // tpb))

    lh = seq // 2
    fre, fim, iim = _dft_mats(lh)
    feats = _hy_feats(seq)
    cph, sph = (jnp.broadcast_to(jnp.asarray(t), (lh, HY_CT)) for t in _hy_twiddles(lh))
    filt = _hy_filter(jnp.asarray(feats[0::2]), jnp.asarray(feats[1::2]), p["hy_w1"], p["hy_b1"], p["hy_w2"],
                      p["hy_b2"], p["hy_freq"], p["hy_w3"], p["hy_decay"], fre, fim, cph, sph)
    seg = seq // rows
    y1 = _longconv(hy, 0, hy, 1, p["hy_cw"], p["hy_cb"], 0, 1, fre, fim, iim, filt,
                   p["hy_d"], 0, nbatch=nbatch, seq=seq, seg=seg, conv_yin=True)
    o_hy = _longconv(y1, 0, hy, 2, p["hy_cw"], p["hy_cb"], 0, 2, fre, fim, iim, filt,
                     p["hy_d"], 1, nbatch=nbatch, seq=seq, seg=seg, conv_yin=False)

    x = _out_proj((o_gla, o_ret, y_s5, o_hy), s5_spec, p["s5_glu_w"], p["s5_glu_b"], x, mod, layer, row_of(tm), p["w_out"],
                  p["ln1_g"], p["ln1_b"], tm, alpha)
    x = _mlp(x, mod, layer, row_of(tm_dense), p["w_up"], p["w_down"], p["ln2_g"], p["ln2_b"], tm_dense, MLP_TF, alpha)
    return x, (s_gla, s_ret, hf_re, hf_im)


def kernel(x_prompt, x_sample, state_gla, state_ret, state_s5_re, state_s5_im, c, c_ctx, ada_w, ada_b, w_in, gla_w_gate, gla_b_gate, gla_norm_w, ret_decay_exp, s5_a_re, s5_a_im, s5_log_step, s5_b_re, s5_b_im, s5_c_re, s5_c_im, s5_d, s5_glu_w, s5_glu_b, hy_conv_w, hy_conv_b, hy_f_w1, hy_f_b1, hy_f_w2, hy_f_b2, hy_f_freq, hy_f_w3, hy_decay, hy_d, w_out, ln1_g, ln1_b, w_up, w_down, ln2_g, ln2_b):
    bp, lp, d = x_prompt.shape
    bs, ls, _ = x_sample.shape
    depth = w_in.shape[0]
    alpha = (2 * depth) ** 0.25
    ngroup = s5_a_re.shape[2]
    ntile = ngroup // S5_GPT

    nrow = -(-(1 + bs) // SUBLANES) * SUBLANES
    c_rows = jnp.concatenate([c_ctx[None, :], c, jnp.zeros((nrow - 1 - bs, d), f32)], axis=0)
    mod = _ada(c_rows, ada_w, ada_b).reshape(depth, nrow, 6, d)

    tr_state = lambda s: jnp.swapaxes(s, -1, -2)
    s5_state = lambda s: jnp.transpose(s.reshape(bs, depth, 2, ntile, S5_SW), (1, 2, 3, 0, 4))
    lat_state = dict(gla=tr_state(state_gla), ret=tr_state(state_ret),
                     s5_re=s5_state(state_s5_re), s5_im=s5_state(state_s5_im))

    tm_ctx = min(lp, DENSE_TM)
    tm_lat = min(ls, DENSE_TM)
    yp = x_prompt.reshape(bp * lp, d)
    ys = x_sample.reshape(bs * ls, d)
    outs = []
    for l in range(depth):
        lre, lim, bbre, bbim = _s5_prep(s5_a_re[l], s5_a_im[l], s5_log_step[l], s5_b_re[l], s5_b_im[l])
        wb, cre, cim = _s5_block_weights(bbre, bbim, s5_c_re[l], s5_c_im[l])
        lam = jnp.stack([lre.reshape(2, ntile, S5_SW), lim.reshape(2, ntile, S5_SW)], axis=2)
        wg, bg, nw = _gla_gate_params(gla_w_gate[l], gla_b_gate[l], gla_norm_w[l])
        w1p = jnp.pad(hy_f_w1[l], ((0, LANES - hy_f_w1.shape[1]), (0, 0)))
        p = dict(
            w_in=_perm_w_in(w_in[l]),
            gla_wg=wg, gla_bg=bg, gla_nw=nw,
            ret_dexp=jnp.broadcast_to(ret_decay_exp[l][:, None, :, None], (2, SUBLANES, HEADS, LANES)).reshape(2, SUBLANES, GRP),
            s5_wb=wb, s5_lam=lam, s5_cre=cre, s5_cim=cim, s5_d=s5_d[l][None, :],
            s5_glu_w=s5_glu_w[l].astype(bf16), s5_glu_b=s5_glu_b[l][None, :],
            hy_cw=hy_conv_w[l], hy_cb=hy_conv_b[l][None, :],
            hy_w1=w1p, hy_b1=hy_f_b1[l][None, :], hy_w2=hy_f_w2[l], hy_b2=hy_f_b2[l][None, :],
            hy_freq=hy_f_freq[l][None, :], hy_w3=hy_f_w3[l], hy_decay=hy_decay[l][None, :],
            hy_d=hy_d[l][:, None, :],
            w_out=w_out[l].astype(bf16), ln1_g=ln1_g[l][None, :], ln1_b=ln1_b[l][None, :],
            w_up=w_up[l].astype(bf16), w_down=w_down[l].astype(bf16),
            ln2_g=ln2_g[l][None, :], ln2_b=ln2_b[l][None, :],
        )
        yp, st = _layer_group(yp, mod, l, p, None, nbatch=bp, seq=lp, rows=1, tm=tm_ctx,
                              row_of=lambda tile: (lambda i: 0), want_state=True, alpha=alpha)
        outs.append(st)
        ys, _ = _layer_group(ys, mod, l, p, lat_state, nbatch=bs, seq=ls, rows=ls // GRID_W, tm=tm_lat,
                             row_of=lambda tile: (lambda i: 1 + i // (ls // tile)), want_state=False, alpha=alpha)

    new_gla = jnp.stack([jnp.swapaxes(o[0], -1, -2) for o in outs], axis=1)
    new_ret = jnp.stack([jnp.swapaxes(o[1], -1, -2) for o in outs], axis=1)
    unpack = lambda h: jnp.transpose(h, (2, 0, 1, 3)).reshape(bp, 2, ngroup, S5_STATE)
    new_re = jnp.stack([unpack(o[2]) for o in outs], axis=1)
    new_im = jnp.stack([unpack(o[3]) for o in outs], axis=1)
    return (yp.reshape(bp, lp, d), ys.reshape(bs, ls, d), new_gla, new_ret, new_re, new_im)
```

```python
import functools
import math

import numpy as np
import jax
import jax.numpy as jnp
from jax import lax
from jax.experimental import pallas as pl
from jax.experimental.pallas import tpu as pltpu

f32 = jnp.float32
bf16 = jnp.bfloat16

GRID_W = 64
HEADS = 4
DK = 64
DV = 128
GLA_RANK = 16
GLA_TAU = 16.0
S5_GROUP = 16
S5_STATE = 64
HY_BANDS = 16
HY_SHORT = 3
CHUNK = 64
RET_CHUNK = 128
LN_EPS = 1e-5
NORM_EPS = 1e-6

LANES = 128
SUBLANES = 8
VMEM_LIMIT = 52 * 1024 * 1024

GRP = HEADS * DV
OFF_GLA = 0
OFF_RET = 3 * GRP
OFF_S5 = 6 * GRP
OFF_HY = 7 * GRP
OFF_LR = 10 * GRP
NP_IN = 5376
TN_IN = 1792
DENSE_TM = 512
MLP_TF = 1024


def _cparams(sem, vmem=VMEM_LIMIT):
    return pltpu.CompilerParams(dimension_semantics=sem, vmem_limit_bytes=vmem)


def _ln(x):
    mu = jnp.mean(x, axis=-1, keepdims=True)
    xc = x - mu
    var = jnp.mean(xc * xc, axis=-1, keepdims=True)
    return xc * lax.rsqrt(var + LN_EPS)


def _silu(x):
    return x * jax.nn.sigmoid(x)


def _bdot(a, b):
    return jnp.dot(a.astype(bf16), b.astype(bf16), preferred_element_type=f32)


def _bdot_nt(a, b):
    return lax.dot_general(a.astype(bf16), b.astype(bf16), (((1,), (1,)), ((), ())), preferred_element_type=f32)


def _bdot_tn(a, b):
    return lax.dot_general(a.astype(bf16), b.astype(bf16), (((0,), (0,)), ((), ())), preferred_element_type=f32)


def _dot_exact_lhs(a_bf, x):
    hi = x.astype(bf16)
    lo = (x - hi.astype(f32)).astype(bf16)
    d = lambda b: jnp.dot(a_bf, b, preferred_element_type=f32)
    return d(hi) + d(lo)


def _dot3(a, b):
    ah = a.astype(bf16)
    al = (a - ah.astype(f32)).astype(bf16)
    bh = b.astype(bf16)
    bl = (b - bh.astype(f32)).astype(bf16)
    d = lambda p, q: jnp.dot(p, q, preferred_element_type=f32)
    return d(ah, bh) + (d(al, bh) + d(ah, bl))


def _ada_kernel(c_ref, w_ref, b_ref, o_ref):
    s = _silu(c_ref[...])
    o_ref[...] = _bdot(s, w_ref[...]) + b_ref[...]


def _ada(c_rows, ada_w, ada_b):
    depth, d, n = ada_w.shape
    r = c_rows.shape[0]
    tn = 1024
    return pl.pallas_call(
        _ada_kernel,
        out_shape=jax.ShapeDtypeStruct((depth, r, n), f32),
        grid=(depth, n // tn),
        in_specs=[pl.BlockSpec((r, d), lambda l, j: (0, 0)),
                  pl.BlockSpec((None, d, tn), lambda l, j: (l, 0, j)),
                  pl.BlockSpec((None, 1, tn), lambda l, j: (l, 0, j))],
        out_specs=pl.BlockSpec((None, r, tn), lambda l, j: (l, 0, j)),
        compiler_params=_cparams(("parallel", "parallel")),
        name="ada",
    )(c_rows, ada_w, ada_b.reshape(depth, 1, n))


def _in_kernel(x_ref, mod_ref, w_ref, o_ref, u_ref, hy_ref, h_ref, *, nsub, sub):
    j = pl.program_id(1)

    @pl.when(j == 0)
    def _():
        y = _ln(x_ref[...])
        h_ref[...] = (y * (1.0 + mod_ref[1:2, :]) + mod_ref[0:1, :]).astype(bf16)

    cols = pl.ds(pl.multiple_of(j * TN_IN, LANES), TN_IN)
    o_ref[...] = jnp.dot(h_ref[...], w_ref[:, cols], preferred_element_type=f32)

    @pl.when(j == OFF_S5 // TN_IN)
    def _():
        c0 = OFF_S5 % TN_IN
        for k in range(nsub):
            u_ref[:, k * GRP:(k + 1) * GRP] = o_ref[k * sub:(k + 1) * sub, c0:c0 + GRP]

    @pl.when(j == OFF_HY // TN_IN)
    def _():
        c0 = OFF_HY % TN_IN
        hy_ref[...] = o_ref[:, c0:c0 + 3 * GRP]


def _in_proj(x, mod, layer, row_fn, w_perm, tm, seq):
    t, d = x.shape
    assert OFF_S5 // TN_IN == (OFF_S5 + GRP - 1) // TN_IN
    assert OFF_HY // TN_IN == (OFF_HY + 3 * GRP - 1) // TN_IN
    sub = min(tm, seq)
    nsub = tm // sub
    tps = seq // sub
    return pl.pallas_call(
        functools.partial(_in_kernel, nsub=nsub, sub=sub),
        out_shape=[jax.ShapeDtypeStruct((t, NP_IN), f32), jax.ShapeDtypeStruct((seq, (t // seq) * GRP), f32),
                   jax.ShapeDtypeStruct((t, 3 * GRP), f32)],
        grid=(t // tm, NP_IN // TN_IN),
        in_specs=[pl.BlockSpec((tm, d), lambda i, j: (i, 0)),
                  pl.BlockSpec((None, None, 6, d), lambda i, j: (layer, row_fn(i), 0, 0)),
                  pl.BlockSpec((d, NP_IN), lambda i, j: (0, 0), pipeline_mode=pl.Buffered(1))],
        out_specs=[pl.BlockSpec((tm, TN_IN), lambda i, j: (i, j)),
                   pl.BlockSpec((sub, nsub * GRP), lambda i, j: (i % tps, i // tps)),
                   pl.BlockSpec((tm, 3 * GRP), lambda i, j: (i, 0))],
        scratch_shapes=[pltpu.VMEM((tm, d), bf16)],
        compiler_params=_cparams(("parallel", "arbitrary")),
        name="in_proj",
    )(x, mod, w_perm)


LA_TAIL = 256


def _linattn_kernel(*refs, mode, seq, has_s0, want_state):
    it = iter(refs)
    qk_ref, v_ref, g_ref = next(it), next(it), next(it)
    if mode == "gla":
        lr_ref, wg_ref, bg_ref, nw_ref = next(it), next(it), next(it), next(it)
    else:
        dexp_ref = next(it)
    s0_ref = next(it) if has_s0 else None
    o_ref = next(it)
    sf_ref = next(it) if want_state else None
    ob_ref, st_ref = next(it), next(it)
    if mode == "gla":
        qkd_ref, ku_ref, dec_ref = next(it), next(it), next(it)

    c = CHUNK if mode == "gla" else RET_CHUNK
    nchunk = seq // c
    scale = DK ** -0.5
    row = lax.broadcasted_iota(jnp.int32, (c, c), 0)
    col = lax.broadcasted_iota(jnp.int32, (c, c), 1)
    is_k = (lax.broadcasted_iota(jnp.int32, (1, GRP), 1) & DK) != 0
    hq = lambda a, h: a[:, h * LANES:h * LANES + DK]
    hk = lambda a, h: a[:, h * LANES + DK:(h + 1) * LANES]
    hv = lambda h: slice(h * DV, (h + 1) * DV)

    if has_s0:
        st_ref[...] = s0_ref[...]
    else:
        st_ref[...] = jnp.zeros_like(st_ref)

    per_dir = []
    for d in (0, 1):
        causal = (row >= col) if d == 0 else (row <= col)
        if mode == "gla":
            tri = jnp.where(causal, 1.0, 0.0).astype(bf16)
            sgn = jnp.where(is_k, -1.0, 1.0)
            qsc = jnp.where(is_k, 1.0, scale)
            per_dir.append((causal, tri, sgn, qsc))
        else:
            lgr = jnp.log1p(-jnp.exp2(-dexp_ref[d]))[0:1, :]
            rowg = lax.broadcasted_iota(jnp.int32, (c, GRP), 0)
            pos = rowg if d == 0 else (c - 1 - rowg)
            pw = jnp.where(is_k, c - 1 - pos, pos + 1).astype(f32)
            wqk = jnp.exp(lgr * pw) * jnp.where(is_k, scale, 1.0)
            dist = ((row - col) if d == 0 else (col - row)).astype(f32)
            masks = [jnp.where(causal, jnp.exp(jnp.broadcast_to(lgr[:, h * LANES:h * LANES + c], (c, c)) * dist), 0.0) * scale
                     for h in range(HEADS)]
            dec = jnp.exp(lgr * float(c))
            per_dir.append((wqk, masks, dec))

    if mode == "gla":
        def pre(n, carry):
            rows = pl.ds(pl.multiple_of(n * c, c), c)
            qk = qk_ref[rows, :]
            lrc = lr_ref[rows, :]
            logits = [_bdot(lrc, wg_ref[d]) + bg_ref[d] for d in (0, 1)]
            la = [(jnp.minimum(x, 0.0) - jnp.log(1.0 + jnp.exp(-jnp.abs(x)))) * (1.0 / GLA_TAU) for x in logits]
            b = [_dot_exact_lhs(per_dir[d][1], la[d]) for d in (0, 1)]
            for d in (0, 1):
                _, _, sgn, qsc = per_dir[d]
                last = c - 1 if d == 0 else 0
                btot = b[d][last:last + 1, :]
                qkd_ref[d, rows, :] = (qk * (jnp.exp(b[d] * sgn) * qsc)).astype(bf16)
                ku_ref[d, rows, :] = (qk * jnp.exp(btot - b[d])).astype(bf16)
                dec_ref[d, n] = jnp.exp(btot)
            return carry

        lax.fori_loop(0, nchunk, pre, 0, unroll=2)

    def body(i, carry):
        rows, qka, qko, ku, dec = [], [], [], [], []
        for d in (0, 1):
            n = i if d == 0 else nchunk - 1 - i
            rows.append(pl.ds(pl.multiple_of(n * c, c), c))
            if mode == "gla":
                qka.append(qkd_ref[d, rows[d], :])
                qko.append(qka[d])
                ku.append(ku_ref[d, rows[d], :])
                dec.append(dec_ref[d, n])
            else:
                wqk, _, dcy = per_dir[d]
                qka.append(qk_ref[rows[d], :])
                qko.append(qka[d] * wqk)
                ku.append(qko[d])
                dec.append(dcy)
        chains = [(d, h) for d in (0, 1) for h in range(HEADS)]
        v = {(d, h): v_ref[rows[d], hv(h)] for d, h in chains}
        st = {(d, h): st_ref[d, h] for d, h in chains}
        att = {(d, h): _bdot_nt(hq(qka[d], h), hk(qka[d], h)) for d, h in chains}
        oi = {(d, h): _bdot_nt(hq(qko[d], h), st[d, h]) for d, h in chains}
        up = {(d, h): _bdot_tn(v[d, h], hk(ku[d], h)) for d, h in chains}
        for d, h in chains:
            if mode == "gla":
                a = jnp.where(per_dir[d][0], att[d, h], 0.0)
            else:
                a = att[d, h] * per_dir[d][1][h]
            o = _bdot(a, v[d, h]) + oi[d, h]
            st_ref[d, h] = hq(dec[d], h) * st[d, h] + up[d, h]
            if d == 0:
                o_ref[rows[d], hv(h)] = o
            else:
                ob_ref[rows[d], hv(h)] = o
        return carry

    lax.fori_loop(0, nchunk, body, 0, unroll=4 if mode == "gla" else 2)
    if want_state:
        sf_ref[...] = st_ref[...]

    tl = min(LA_TAIL, seq)

    def tail(i, carry):
        rows = pl.ds(pl.multiple_of(i * tl, tl), tl)
        for h in range(HEADS):
            o = o_ref[rows, hv(h)] + ob_ref[rows, hv(h)]
            if mode == "gla":
                o = o * lax.rsqrt(jnp.mean(o * o, axis=-1, keepdims=True) + NORM_EPS) * nw_ref[:, hv(h)]
            else:
                o = _ln(o)
            o_ref[rows, hv(h)] = o * _silu(g_ref[rows, hv(h)])
        return carry

    lax.fori_loop(0, seq // tl, tail, 0)


def _linattn(proj, col0, extra, s0t, layer, *, mode, nbatch, seq, want_state):
    cb = col0 // GRP
    full = lambda a: pl.BlockSpec(a.shape, lambda b: (0,) * a.ndim)
    in_specs = [pl.BlockSpec((seq, GRP), lambda b: (b, cb)),
                pl.BlockSpec((seq, GRP), lambda b: (b, cb + 1)),
                pl.BlockSpec((seq, GRP), lambda b: (b, cb + 2))]
    args = [proj, proj, proj]
    scratch = [pltpu.VMEM((seq, GRP), f32), pltpu.VMEM((2, HEADS, DV, DK), f32)]
    if mode == "gla":
        scratch += [pltpu.VMEM((2, seq, GRP), bf16), pltpu.VMEM((2, seq, GRP), bf16),
                    pltpu.VMEM((2, seq // CHUNK, 1, GRP), f32)]
    if mode == "gla":
        in_specs += [pl.BlockSpec((seq, LANES), lambda b: (b, OFF_LR // LANES))] + [full(a) for a in extra]
        args += [proj, *extra]
    else:
        in_specs += [full(a) for a in extra]
        args += list(extra)
    has_s0 = s0t is not None
    if has_s0:
        in_specs += [pl.BlockSpec((None, None, 2, HEADS, DV, DK), lambda b: (b, layer, 0, 0, 0, 0))]
        args += [s0t]
    out_shape = [jax.ShapeDtypeStruct((nbatch * seq, GRP), f32)]
    out_specs = [pl.BlockSpec((seq, GRP), lambda b: (b, 0))]
    if want_state:
        out_shape += [jax.ShapeDtypeStruct((nbatch, 2, HEADS, DV, DK), f32)]
        out_specs += [pl.BlockSpec((None, 2, HEADS, DV, DK), lambda b: (b, 0, 0, 0, 0))]
    res = pl.pallas_call(
        functools.partial(_linattn_kernel, mode=mode, seq=seq, has_s0=has_s0, want_state=want_state),
        out_shape=out_shape,
        grid=(nbatch,),
        in_specs=in_specs,
        out_specs=out_specs,
        scratch_shapes=scratch,
        compiler_params=_cparams(("parallel",)),
        name="linattn_" + mode,
    )(*args)
    return (res[0], res[1]) if want_state else (res[0], None)


def _s5_prep_kernel(are_ref, aim_ref, lstep_ref, bre_ref, bim_ref, lre_ref, lim_ref, bbre_ref, bbim_ref):
    ar, ai = are_ref[...], aim_ref[...]
    st = jnp.exp(lstep_ref[...])
    mag = jnp.exp(ar * st)
    lr = mag * jnp.cos(ai * st)
    li = mag * jnp.sin(ai * st)
    nr, ni = lr - 1.0, li
    den = ar * ar + ai * ai
    kr = (nr * ar + ni * ai) / den
    ki = (ni * ar - nr * ai) / den
    br, bi = bre_ref[...], bim_ref[...]
    lre_ref[...] = lr
    lim_ref[...] = li
    bbre_ref[...] = kr * br - ki * bi
    bbim_ref[...] = kr * bi + ki * br


def _s5_prep(a_re, a_im, log_step, b_re, b_im):
    _, g, p = a_re.shape
    hch = b_re.shape[-1]
    rows = 2 * g * hch
    bc = lambda a: jnp.broadcast_to(a[:, :, None, :], (2, g, hch, p)).reshape(rows, p)
    tr = lambda b: jnp.transpose(b, (0, 1, 3, 2)).reshape(rows, p)
    ls = jnp.broadcast_to(log_step[:, :, None, None], (2, g, hch, p)).reshape(rows, p)
    spec = pl.BlockSpec((rows, p), lambda: (0, 0))
    lre, lim, bbre, bbim = pl.pallas_call(
        _s5_prep_kernel,
        out_shape=[jax.ShapeDtypeStruct((rows, p), f32)] * 4,
        in_specs=[spec] * 5,
        out_specs=[spec] * 4,
        name="s5_prep",
    )(bc(a_re), bc(a_im), ls, tr(b_re), tr(b_im))
    r4 = lambda a: a.reshape(2, g, hch, p)
    return r4(lre)[:, :, 0, :], r4(lim)[:, :, 0, :], r4(bbre), r4(bbim)


S5_TC = 64
S5_GPT = LANES // S5_GROUP
S5_SW = S5_GPT * S5_STATE


S5_TB = 256


def _s5_scan_kernel(*refs, tb, nb, has_h0, want_state):
    it = iter(refs)
    uf_ref, ub_ref, wb_ref, lam_ref, cre_ref, cim_ref, dsk_ref = (next(it) for _ in range(7))
    h0re_ref = next(it) if has_h0 else None
    h0im_ref = next(it) if has_h0 else None
    yf_ref, yb_ref = next(it), next(it)
    hfre_ref = next(it) if want_state else None
    hfim_ref = next(it) if want_state else None
    bu_ref, hs_ref, st_ref = next(it), next(it), next(it)

    s = pl.program_id(1)
    tc = S5_TC
    nchunk = tb // tc
    sw = S5_SW
    u_refs, y_refs = (uf_ref, ub_ref), (yf_ref, yb_ref)

    @pl.when(s == 0)
    def _():
        if has_h0:
            for d in (0, 1):
                st_ref[2 * d] = h0re_ref[d]
                st_ref[2 * d + 1] = h0im_ref[d]
        else:
            st_ref[...] = jnp.zeros_like(st_ref)

    lam = [(jnp.broadcast_to(lam_ref[d, 0:1, :], (nb, sw)), jnp.broadcast_to(lam_ref[d, 1:2, :], (nb, sw)))
           for d in (0, 1)]

    def chunk(i, carry):
        h = [list(carry[0:2]), list(carry[2:4])]
        t0 = [pl.multiple_of(i * tc, tc), pl.multiple_of((nchunk - 1 - i) * tc, tc)]
        ub = [u_refs[d][pl.ds(t0[d], tc)] for d in (0, 1)]
        for d in (0, 1):
            bu_ref[d] = _bdot(ub[d].reshape(tc * nb, LANES), wb_ref[d])
        for k in range(tc):
            for d in (0, 1):
                t = k if d == 0 else tc - 1 - k
                rs = slice(t * nb, (t + 1) * nb)
                lam_r, lam_i = lam[d]
                hre, him = h[d]
                nre = lam_r * hre - lam_i * him + bu_ref[d, rs, 0:sw]
                nim = lam_r * him + lam_i * hre + bu_ref[d, rs, sw:2 * sw]
                h[d] = [nre, nim]
                hs_ref[d, rs, 0:sw] = nre
                hs_ref[d, rs, sw:2 * sw] = nim
        for d in (0, 1):
            y = _bdot(hs_ref[d, :, 0:sw], cre_ref[d]) - _bdot(hs_ref[d, :, sw:2 * sw], cim_ref[d])
            y = y.reshape(tc, nb, LANES)
            if d == 0:
                y = y + ub[0] * dsk_ref[...]
            y_refs[d][pl.ds(t0[d], tc)] = y
        return (*h[0], *h[1])

    hfin = lax.fori_loop(0, nchunk, chunk, tuple(st_ref[q] for q in range(4)))
    for q in range(4):
        st_ref[q] = hfin[q]

    if want_state:
        @pl.when(s == pl.num_programs(1) - 1)
        def _():
            for d in (0, 1):
                hfre_ref[d] = hfin[2 * d]
                hfim_ref[d] = hfin[2 * d + 1]


def _s5_scan(u_t, wb, lam, cre, cim, dsk, h0, *, want_state):
    seq, nb, ch = u_t.shape
    ntile = ch // LANES
    sw = S5_SW
    tb = min(S5_TB, seq)
    nblk = seq // tb
    fwd = lambda j, s: (s, 0, j)
    bwd = lambda j, s: (nblk - 1 - s, 0, j)
    par = lambda shape: pl.BlockSpec(shape, lambda j, s: (0, j, 0, 0))
    in_specs = [pl.BlockSpec((tb, nb, LANES), fwd), pl.BlockSpec((tb, nb, LANES), bwd),
                par((2, None, LANES, 2 * sw)), par((2, None, 2, sw)),
                par((2, None, sw, LANES)), par((2, None, sw, LANES)),
                pl.BlockSpec((1, LANES), lambda j, s: (0, j))]
    args = [u_t, u_t, wb, lam, cre, cim, dsk]
    has_h0 = h0 is not None
    st_spec = par((2, None, nb, sw))
    if has_h0:
        in_specs += [st_spec, st_spec]
        args += list(h0)
    out_shape = [jax.ShapeDtypeStruct((seq, nb, ch), f32)] * 2
    out_specs = [pl.BlockSpec((tb, nb, LANES), fwd), pl.BlockSpec((tb, nb, LANES), bwd)]
    if want_state:
        out_shape += [jax.ShapeDtypeStruct((2, ntile, nb, sw), f32)] * 2
        out_specs += [st_spec, st_spec]
    res = pl.pallas_call(
        functools.partial(_s5_scan_kernel, tb=tb, nb=nb, has_h0=has_h0, want_state=want_state),
        out_shape=out_shape,
        grid=(ntile, nblk),
        in_specs=in_specs,
        out_specs=out_specs,
        scratch_shapes=[pltpu.VMEM((2, S5_TC * nb, 2 * sw), f32), pltpu.VMEM((2, S5_TC * nb, 2 * sw), f32),
                        pltpu.VMEM((4, nb, sw), f32)],
        compiler_params=_cparams(("parallel", "arbitrary")),
        name="s5_scan",
    )(*args)
    return (res[0], res[1], res[2], res[3]) if want_state else (res[0], res[1], None, None)


HY_TK = 256
HY_CT = 256


def _pk_mul(xr, xi, a, b, d):
    return xr * a - xi * b, xr * b + xi * d


def _hy_filter_kernel(fte_ref, fto_ref, w1_ref, b1_ref, w2_ref, b2_ref, fq_ref, w3f_ref, w3b_ref, dcf_ref, dcb_ref,
                      fre_ref, fim_ref, cph_ref, sph_ref,
                      fea_ref, feb_ref, foa_ref, fob_ref, ga_ref, gb_ref, ny_ref, xs_ref, he_ref, ho_ref, *, lh, tk):
    j = pl.program_id(1)
    ct = HY_CT

    @pl.when(jnp.logical_and(j == 0, pl.program_id(0) == 0))
    def _():
        fq = fq_ref[...]
        for feat_ref, h_ref in ((fte_ref, he_ref), (fto_ref, ho_ref)):
            h = jnp.sin(fq * (_dot3(feat_ref[...], w1_ref[...]) + b1_ref[...]))
            h_ref[...] = jnp.sin(fq * (_dot3(h, w2_ref[...]) + b2_ref[...]))

    @pl.when(j == 0)
    def _():
        he, te = he_ref[...], fte_ref[:, 0:1]
        ho, to = ho_ref[...], fto_ref[:, 0:1]
        raw = lambda h, t, w3_ref, dc_ref: _dot3(h, w3_ref[...]) * jnp.exp(-t * jnp.abs(dc_ref[...]))
        ffe, ffo = raw(he, te, w3f_ref, dcf_ref), raw(ho, to, w3f_ref, dcf_ref)
        fbe, fbo = raw(he, te, w3b_ref, dcb_ref), raw(ho, to, w3b_ref, dcb_ref)
        sf = jnp.sum(jnp.abs(ffe), axis=0, keepdims=True) + jnp.sum(jnp.abs(ffo), axis=0, keepdims=True)
        sb = jnp.sum(jnp.abs(fbe), axis=0, keepdims=True) + jnp.sum(jnp.abs(fbo), axis=0, keepdims=True)
        rowi = lax.broadcasted_iota(jnp.int32, (lh, ct), 0)
        xs_ref[:, 0 * ct:1 * ct] = (ffe / sf).astype(bf16)
        xs_ref[:, 1 * ct:2 * ct] = jnp.where(rowi == 0, 0.0, fbe / sb).astype(bf16)
        xs_ref[:, 2 * ct:3 * ct] = (ffo / sf).astype(bf16)
        xs_ref[:, 3 * ct:4 * ct] = (fbo / sb).astype(bf16)

    xs = xs_ref[...]
    r = jnp.dot(fre_ref[...], xs, preferred_element_type=f32)
    i = jnp.dot(fim_ref[...], xs, preferred_element_type=f32)
    r1, r2, r3, r4 = (r[:, q * ct:(q + 1) * ct] for q in range(4))
    i1, i2, i3, i4 = (i[:, q * ct:(q + 1) * ct] for q in range(4))
    c, s = cph_ref[...], sph_ref[...]
    fe_re, fe_im = r1 + r2, i1 - i2
    fo_re = r3 + c * r4 + s * i4
    fo_im = i3 + s * r4 - c * i4
    g_re = c * fo_re + s * fo_im
    g_im = c * fo_im - s * fo_re
    scl = 1.0 / lh
    first = jnp.logical_and(lax.broadcasted_iota(jnp.int32, (tk, ct), 0) == 0, j == 0)
    for a_ref, b_ref, re, im in ((fea_ref, feb_ref, fe_re, fe_im), (foa_ref, fob_ref, fo_re, fo_im),
                                 (ga_ref, gb_ref, g_re, g_im)):
        a_ref[...] = jnp.where(first, 0.5 * scl, scl) * re
        b_ref[...] = jnp.where(first, 0.0, scl * im)

    @pl.when(j == 0)
    def _():
        fo_ny = i3[0:1, :] - i4[0:1, :]
        ny_ref[...] = jnp.zeros_like(ny_ref)
        ny_ref[0:1, :] = (0.5 * scl) * (i1[0:1, :] + i2[0:1, :])
        ny_ref[1:2, :] = (0.5 * scl) * fo_ny
        ny_ref[2:3, :] = (-0.5 * scl) * fo_ny


def _hy_filter(feat_e, feat_o, w1p, b1, w2, b2, fq, w3, dc, fre, fim, cph, sph):
    lh = feat_e.shape[0]
    ncol = w3.shape[1] // 2
    nct = ncol // HY_CT
    tk = min(HY_TK, lh)
    full = lambda a: pl.BlockSpec(a.shape, lambda c, j: (0,) * a.ndim)
    out_spec = pl.BlockSpec((tk, HY_CT), lambda c, j: (j, c))
    return pl.pallas_call(
        functools.partial(_hy_filter_kernel, lh=lh, tk=tk),
        out_shape=[jax.ShapeDtypeStruct((lh, ncol), f32)] * 6 + [jax.ShapeDtypeStruct((SUBLANES, ncol), f32)],
        grid=(nct, lh // tk),
        in_specs=[full(feat_e), full(feat_o), full(w1p), full(b1), full(w2), full(b2), full(fq),
                  pl.BlockSpec((w3.shape[0], HY_CT), lambda c, j: (0, c)),
                  pl.BlockSpec((w3.shape[0], HY_CT), lambda c, j: (0, nct + c)),
                  pl.BlockSpec((1, HY_CT), lambda c, j: (0, c)),
                  pl.BlockSpec((1, HY_CT), lambda c, j: (0, nct + c)),
                  pl.BlockSpec((tk, lh), lambda c, j: (j, 0)),
                  pl.BlockSpec((tk, lh), lambda c, j: (j, 0)),
                  pl.BlockSpec((tk, HY_CT), lambda c, j: (j, 0)),
                  pl.BlockSpec((tk, HY_CT), lambda c, j: (j, 0))],
        out_specs=[out_spec] * 6 + [pl.BlockSpec((SUBLANES, HY_CT), lambda c, j: (0, c))],
        scratch_shapes=[pltpu.VMEM((lh, 4 * HY_CT), bf16), pltpu.VMEM((lh, w2.shape[1]), f32),
                        pltpu.VMEM((lh, w2.shape[1]), f32)],
        compiler_params=_cparams(("arbitrary", "arbitrary")),
        name="hy_filter",
    )(feat_e, feat_o, w1p, b1, w2, b2, fq, w3, w3, dc, dc, fre, fim, cph, sph)


def _short_conv(x, w_ref, b_ref, seg):
    rows = x.shape[0]
    r = lax.broadcasted_iota(jnp.int32, x.shape, 0) & (seg - 1)
    xp = jnp.where(r == 0, 0.0, pltpu.roll(x, 1, 0))
    xn = jnp.where(r == seg - 1, 0.0, pltpu.roll(x, rows - 1, 0))
    return xp * w_ref[0:1, :] + x * w_ref[1:2, :] + xn * w_ref[2:3, :] + b_ref[...]


HY_PB = 256


def _longconv_kernel(yin_ref, gate_ref, cwy_ref, cby_ref, cwg_ref, cbg_ref, fre_ref, fim_ref, iim_ref,
                     fea_ref, feb_ref, foa_ref, fob_ref, ga_ref, gb_ref, ny_ref, dsk_ref,
                     o_ref, y32_ref, ybf_ref, acc_ref, *, seg, conv_yin, tk):
    j = pl.program_id(1)
    g = GRP
    seq = yin_ref.shape[0]
    pb = min(HY_PB, seq)
    hb = pb // 2
    pr = lax.broadcasted_iota(jnp.int32, (pb, pb), 0)
    pc = lax.broadcasted_iota(jnp.int32, (pb, pb), 1)

    @pl.when(j == 0)
    def _():
        y = yin_ref[...]
        if conv_yin:
            y = _short_conv(y, cwy_ref, cby_ref, seg)
        y32_ref[...] = y
        split = jnp.where(pc == jnp.where(pr < hb, 2 * pr, 2 * (pr - hb) + 1), 1.0, 0.0).astype(bf16)
        for k in range(seq // pb):
            t = jnp.dot(split, y[k * pb:(k + 1) * pb, :].astype(bf16), preferred_element_type=f32).astype(bf16)
            ybf_ref[k * hb:(k + 1) * hb, 0:g] = t[0:hb, :]
            ybf_ref[k * hb:(k + 1) * hb, g:2 * g] = t[hb:pb, :]
        acc_ref[...] = jnp.zeros_like(acc_ref)

    freq = pl.ds(pl.multiple_of(j * tk, tk), tk)
    yb = ybf_ref[...]
    r = jnp.dot(fre_ref[freq, :], yb, preferred_element_type=f32)
    i = jnp.dot(fim_ref[freq, :], yb, preferred_element_type=f32)
    first = jnp.logical_and(lax.broadcasted_iota(jnp.int32, (tk, g), 0) == 0, j == 0)
    fe_a, fo_a, g_a = fea_ref[...], foa_ref[...], ga_ref[...]
    f_e = (fe_a, feb_ref[...], jnp.where(first, ny_ref[0:1, :], fe_a))
    f_o = (fo_a, fob_ref[...], jnp.where(first, ny_ref[1:2, :], fo_a))
    f_g = (g_a, gb_ref[...], jnp.where(first, ny_ref[2:3, :], g_a))
    er, ei, orr, oi = r[:, 0:g], i[:, 0:g], r[:, g:2 * g], i[:, g:2 * g]
    p1, p2 = _pk_mul(er, ei, *f_e), _pk_mul(orr, oi, *f_g)
    p3, p4 = _pk_mul(er, ei, *f_o), _pk_mul(orr, oi, *f_e)
    zr = jnp.concatenate([p1[0] + p2[0], p3[0] + p4[0]], axis=1).astype(bf16)
    zi = jnp.concatenate([p1[1] + p2[1], p3[1] + p4[1]], axis=1).astype(bf16)
    acc_ref[...] += (jnp.dot(fre_ref[:, freq], zr, preferred_element_type=f32)
                     + jnp.dot(iim_ref[:, freq], zi, preferred_element_type=f32))

    @pl.when(j == pl.num_programs(1) - 1)
    def _():
        gate = _short_conv(gate_ref[...], cwg_ref, cbg_ref, seg)
        d = dsk_ref[...]
        merge = jnp.where(pr == jnp.where(pc < hb, 2 * pc, 2 * (pc - hb) + 1), 1.0, 0.0).astype(bf16)
        for k in range(seq // pb):
            stacked = jnp.concatenate([acc_ref[k * hb:(k + 1) * hb, 0:g], acc_ref[k * hb:(k + 1) * hb, g:2 * g]], axis=0)
            conv = _dot_exact_lhs(merge, stacked)
            rows = slice(k * pb, (k + 1) * pb)
            o_ref[rows, :] = gate[rows, :] * (conv + d * y32_ref[rows, :])


def _longconv(yin, yin_cb, gate_arr, gate_cb, cw, cb, part_y, part_g, fre, fim, iim, filt, dsk, order,
              *, nbatch, seq, seg, conv_yin):
    lh = seq // 2
    tk = min(HY_TK, lh)
    once = pl.Buffered(1)
    cw_spec = lambda p: pl.BlockSpec((HY_SHORT, GRP), lambda b, j: (0, p))
    cb_spec = lambda p: pl.BlockSpec((1, GRP), lambda b, j: (0, p))
    dft_spec = pl.BlockSpec((lh, lh), lambda b, j: (0, 0), pipeline_mode=once)
    filt_spec = pl.BlockSpec((tk, GRP), lambda b, j: (j, order))
    return pl.pallas_call(
        functools.partial(_longconv_kernel, seg=seg, conv_yin=conv_yin, tk=tk),
        out_shape=jax.ShapeDtypeStruct((nbatch * seq, GRP), f32),
        grid=(nbatch, lh // tk),
        in_specs=[pl.BlockSpec((seq, GRP), lambda b, j: (b, yin_cb)),
                  pl.BlockSpec((seq, GRP), lambda b, j: (b, gate_cb)),
                  cw_spec(part_y), cb_spec(part_y), cw_spec(part_g), cb_spec(part_g),
                  dft_spec, dft_spec, dft_spec,
                  filt_spec, filt_spec, filt_spec, filt_spec, filt_spec, filt_spec,
                  pl.BlockSpec((SUBLANES, GRP), lambda b, j: (0, order)),
                  pl.BlockSpec((None, 1, GRP), lambda b, j: (order, 0, 0))],
        out_specs=pl.BlockSpec((seq, GRP), lambda b, j: (b, 0)),
        scratch_shapes=[pltpu.VMEM((seq, GRP), f32), pltpu.VMEM((lh, 2 * GRP), bf16), pltpu.VMEM((lh, 2 * GRP), f32)],
        compiler_params=_cparams(("parallel", "arbitrary")),
        name="hy_longconv",
    )(yin, gate_arr, cw, cb, cw, cb, fre, fim, iim, *filt, dsk)


def _out_kernel(m0_ref, m1_ref, y2f_ref, y2b_ref, m3_ref, gw_ref, gb_ref, x_ref, mod_ref, w_ref, g_ref, b_ref, o_ref, *, alpha):
    y = y2f_ref[...] + y2b_ref[...]
    z = y * (0.5 * (1.0 + jnp.tanh(math.sqrt(2.0 / math.pi) * (y + 0.044715 * (y * y * y)))))
    m2 = z * jax.nn.sigmoid(_bdot(z, gw_ref[...]) + gb_ref[...])
    acc = _bdot(m0_ref[...], w_ref[0 * GRP:1 * GRP, :])
    acc += _bdot(m1_ref[...], w_ref[1 * GRP:2 * GRP, :])
    acc += _bdot(m2, w_ref[2 * GRP:3 * GRP, :])
    acc += _bdot(m3_ref[...], w_ref[3 * GRP:4 * GRP, :])
    r = alpha * x_ref[...] + mod_ref[2:3, :] * acc
    o_ref[...] = _ln(r) * g_ref[...] + b_ref[...]


def _out_proj(mixes, s5_spec, glu_w, glu_b, x, mod, layer, row_fn, w_out, g, b, tm, alpha):
    t, d = x.shape
    mspec = pl.BlockSpec((tm, GRP), lambda i: (i, 0))
    const = lambda a: pl.BlockSpec(a.shape, lambda i: (0,) * a.ndim)
    return pl.pallas_call(
        functools.partial(_out_kernel, alpha=alpha),
        out_shape=jax.ShapeDtypeStruct((t, d), f32),
        grid=(t // tm,),
        in_specs=[mspec, mspec, s5_spec, s5_spec, mspec, const(glu_w), const(glu_b),
                  pl.BlockSpec((tm, d), lambda i: (i, 0)),
                  pl.BlockSpec((None, None, 6, d), lambda i: (layer, row_fn(i), 0, 0)),
                  pl.BlockSpec((d, d), lambda i: (0, 0), pipeline_mode=pl.Buffered(1)),
                  pl.BlockSpec((1, d), lambda i: (0, 0)),
                  pl.BlockSpec((1, d), lambda i: (0, 0))],
        out_specs=pl.BlockSpec((tm, d), lambda i: (i, 0)),
        compiler_params=_cparams(("parallel",)),
        name="out_proj",
    )(*mixes, glu_w, glu_b, x, mod, w_out, g, b)


def _mlp_kernel(x_ref, mod_ref, wu_ref, wd_ref, g_ref, b_ref, o_ref, h_ref, acc_ref, *, alpha):
    j = pl.program_id(1)

    @pl.when(j == 0)
    def _():
        y = _ln(x_ref[...])
        h_ref[...] = (y * (1.0 + mod_ref[4:5, :]) + mod_ref[3:4, :]).astype(bf16)
        acc_ref[...] = jnp.zeros_like(acc_ref)

    u = jnp.maximum(jnp.dot(h_ref[...], wu_ref[...], preferred_element_type=f32), 0.0)
    acc_ref[...] += jnp.dot((u * u).astype(bf16), wd_ref[...], preferred_element_type=f32)

    @pl.when(j == pl.num_programs(1) - 1)
    def _():
        r = alpha * x_ref[...] + mod_ref[5:6, :] * acc_ref[...]
        o_ref[...] = _ln(r) * g_ref[...] + b_ref[...]


def _mlp(x, mod, layer, row_fn, w_up, w_down, g, b, tm, tf, alpha):
    t, d = x.shape
    dff = w_up.shape[1]
    return pl.pallas_call(
        functools.partial(_mlp_kernel, alpha=alpha),
        out_shape=jax.ShapeDtypeStruct((t, d), f32),
        grid=(t // tm, dff // tf),
        in_specs=[pl.BlockSpec((tm, d), lambda i, j: (i, 0)),
                  pl.BlockSpec((None, None, 6, d), lambda i, j: (layer, row_fn(i), 0, 0)),
                  pl.BlockSpec((d, tf), lambda i, j: (0, j)),
                  pl.BlockSpec((tf, d), lambda i, j: (j, 0)),
                  pl.BlockSpec((1, d), lambda i, j: (0, 0)),
                  pl.BlockSpec((1, d), lambda i, j: (0, 0))],
        out_specs=pl.BlockSpec((tm, d), lambda i, j: (i, 0)),
        scratch_shapes=[pltpu.VMEM((tm, d), bf16), pltpu.VMEM((tm, d), f32)],
        compiler_params=_cparams(("parallel", "arbitrary")),
        name="mlp",
    )(x, mod, w_up, w_down, g, b)


def _perm_w_in(w):
    d = w.shape[0]
    hk, hv = HEADS * DK, HEADS * DV
    o = 0
    parts = {}
    for name, width in (("gq", hk), ("gk", hk), ("gv", hv), ("gg", GRP), ("glr", 2 * GLA_RANK),
                        ("rq", hk), ("rk", hk), ("rv", hv), ("rg", GRP), ("su", GRP), ("hy", 3 * GRP)):
        parts[name] = w[:, o:o + width]
        o += width
    qk = lambda q, k: jnp.concatenate([q.reshape(d, HEADS, DK), k.reshape(d, HEADS, DK)], axis=-1).reshape(d, 2 * hk)
    cols = [qk(parts["gq"], parts["gk"]), parts["gv"], parts["gg"],
            qk(parts["rq"], parts["rk"]), parts["rv"], parts["rg"],
            parts["su"], parts["hy"], parts["glr"]]
    wp = jnp.concatenate(cols, axis=1)
    return jnp.pad(wp, ((0, 0), (0, NP_IN - wp.shape[1]))).astype(bf16)


def _gla_gate_params(w_gate, b_gate, norm_w):
    r = w_gate.shape[1]
    dup = lambda a: jnp.concatenate([a, a], axis=-1).reshape(*a.shape[:-2], 2 * HEADS * DK)
    wg = dup(w_gate.reshape(2, r, HEADS, DK))
    full = jnp.zeros((2, LANES, 2 * HEADS * DK), f32)
    for d in range(2):
        full = full.at[d, d * r:(d + 1) * r, :].set(wg[d])
    return full.astype(bf16), dup(b_gate.reshape(2, 1, HEADS, DK)), jnp.tile(norm_w[None, :], (1, HEADS))


def _s5_block_weights(bbre, bbim, c_re, c_im):
    _, g, hch, p = bbre.shape
    ntile = g // S5_GPT
    eye = jnp.eye(S5_GPT, dtype=f32)

    def in_map(b):
        b = b.reshape(2, ntile, S5_GPT, hch, p)
        return jnp.einsum("dtghp,gk->dtghkp", b, eye).reshape(2, ntile, S5_GPT * hch, S5_GPT * p)

    def out_map(c):
        c = c.reshape(2, ntile, S5_GPT, hch, p)
        return jnp.einsum("dtghp,gk->dtgpkh", c, eye).reshape(2, ntile, S5_GPT * p, S5_GPT * hch)

    wb = jnp.concatenate([in_map(bbre), in_map(bbim)], axis=-1).astype(bf16)
    return wb, out_map(c_re).astype(bf16), out_map(c_im).astype(bf16)


@functools.lru_cache(maxsize=None)
def _dft_tables(seq):
    n = 2 * seq
    k = np.arange(seq, dtype=np.int64)[:, None]
    s1 = np.arange(seq // 64, dtype=np.int64)[None, :]
    s2 = np.arange(64, dtype=np.int64)[None, :]
    a = 2.0 * np.pi * ((k * s1 * 64) % n) / n
    b = 2.0 * np.pi * ((k * s2) % n) / n
    return tuple(np.asarray(t, np.float32) for t in (np.cos(a), np.sin(a), np.cos(b), np.sin(b)))


def _dft_mats(seq):
    ca, sa, cb, sb = (jnp.asarray(t) for t in _dft_tables(seq))
    cosm = (ca[:, :, None] * cb[:, None, :] - sa[:, :, None] * sb[:, None, :]).reshape(seq, seq)
    sinm = (sa[:, :, None] * cb[:, None, :] + ca[:, :, None] * sb[:, None, :]).reshape(seq, seq)
    sign = jnp.where((jnp.arange(seq) & 1) == 0, 1.0, -1.0).astype(f32)
    ri = jnp.arange(seq)[:, None]
    ci = jnp.arange(seq)[None, :]
    fim = jnp.where(ri == 0, sign[None, :], -sinm)
    iim = jnp.where(ci == 0, sign[:, None], -sinm)
    return cosm.astype(bf16), fim.astype(bf16), iim.astype(bf16)


@functools.lru_cache(maxsize=None)
def _hy_feats(seq):
    t = np.linspace(0.0, 1.0, seq)[:, None]
    w = 2.0 * math.pi * np.arange(seq, dtype=np.float64)[:, None] / seq
    fr = np.linspace(1e-4, HY_BANDS - 1.0, HY_BANDS)[None, :]
    feats = np.concatenate([t, np.cos(fr * w), -np.sin(fr * w)], axis=-1).astype(np.float32)
    return np.pad(feats, ((0, 0), (0, LANES - feats.shape[1])))


@functools.lru_cache(maxsize=None)
def _hy_twiddles(lh):
    ang = np.pi * np.arange(lh, dtype=np.float64)[:, None] / lh
    return np.cos(ang).astype(np.float32), np.sin(ang).astype(np.float32)


def _layer_group(x, mod, layer, p, state, *, nbatch, seq, rows, tm, row_of, want_state, alpha):
    tm_dense = min(DENSE_TM, nbatch * seq)
    proj, u_t, hy = _in_proj(x, mod, layer, row_of(tm_dense), p["w_in"], tm_dense, seq)

    o_gla, s_gla = _linattn(proj, OFF_GLA, (p["gla_wg"], p["gla_bg"], p["gla_nw"]),
                            None if state is None else state["gla"], layer,
                            mode="gla", nbatch=nbatch, seq=seq, want_state=want_state)
    o_ret, s_ret = _linattn(proj, OFF_RET, (p["ret_dexp"],),
                            None if state is None else state["ret"], layer,
                            mode="ret", nbatch=nbatch, seq=seq, want_state=want_state)

    u_t = u_t.reshape(seq, nbatch, GRP)
    h0 =None if state is None else (state["s5_re"][layer], state["s5_im"][layer])
    y_f, y_b, hf_re, hf_im = _s5_scan(u_t, p["s5_wb"], p["s5_lam"], p["s5_cre"], p["s5_cim"], p["s5_d"], h0,
                                      want_state=want_state)
    y_f, y_b = y_f.reshape(seq, nbatch * GRP), y_b.reshape(seq, nbatch * GRP)
    tpb = seq // tm
    s5_spec = pl.BlockSpec((tm, GRP), lambda i: (i % tpb, i // tpb))

    lh = seq // 2
    fre, fim, iim = _dft_mats(lh)
    feats = _hy_feats(seq)
    cph, sph = (jnp.broadcast_to(jnp.asarray(t), (lh, HY_CT)) for t in _hy_twiddles(lh))
    filt = _hy_filter(jnp.asarray(feats[0::2]), jnp.asarray(feats[1::2]), p["hy_w1"], p["hy_b1"], p["hy_w2"],
                      p["hy_b2"], p["hy_freq"], p["hy_w3"], p["hy_decay"], fre, fim, cph, sph)
    seg = seq // rows
    y1 = _longconv(hy, 0, hy, 1, p["hy_cw"], p["hy_cb"], 0, 1, fre, fim, iim, filt,
                   p["hy_d"], 0, nbatch=nbatch, seq=seq, seg=seg, conv_yin=True)
    o_hy = _longconv(y1, 0, hy, 2, p["hy_cw"], p["hy_cb"], 0, 2, fre, fim, iim, filt,
                     p["hy_d"], 1, nbatch=nbatch, seq=seq, seg=seg, conv_yin=False)

    x = _out_proj((o_gla, o_ret, y_f, y_b, o_hy), s5_spec, p["s5_glu_w"], p["s5_glu_b"], x, mod, layer, row_of(tm), p["w_out"],
                  p["ln1_g"], p["ln1_b"], tm, alpha)
    x = _mlp(x, mod, layer, row_of(tm_dense), p["w_up"], p["w_down"], p["ln2_g"], p["ln2_b"], tm_dense, MLP_TF, alpha)
    return x, (s_gla, s_ret, hf_re, hf_im)


def kernel(x_prompt, x_sample, state_gla, state_ret, state_s5_re, state_s5_im, c, c_ctx, ada_w, ada_b, w_in, gla_w_gate, gla_b_gate, gla_norm_w, ret_decay_exp, s5_a_re, s5_a_im, s5_log_step, s5_b_re, s5_b_im, s5_c_re, s5_c_im, s5_d, s5_glu_w, s5_glu_b, hy_conv_w, hy_conv_b, hy_f_w1, hy_f_b1, hy_f_w2, hy_f_b2, hy_f_freq, hy_f_w3, hy_decay, hy_d, w_out, ln1_g, ln1_b, w_up, w_down, ln2_g, ln2_b):
    bp, lp, d = x_prompt.shape
    bs, ls, _ = x_sample.shape
    depth = w_in.shape[0]
    alpha = (2 * depth) ** 0.25
    ngroup = s5_a_re.shape[2]
    ntile = ngroup // S5_GPT

    nrow = -(-(1 + bs) // SUBLANES) * SUBLANES
    c_rows = jnp.concatenate([c_ctx[None, :], c, jnp.zeros((nrow - 1 - bs, d), f32)], axis=0)
    mod = _ada(c_rows, ada_w, ada_b).reshape(depth, nrow, 6, d)

    tr_state = lambda s: jnp.swapaxes(s, -1, -2)
    s5_state = lambda s: jnp.transpose(s.reshape(bs, depth, 2, ntile, S5_SW), (1, 2, 3, 0, 4))
    lat_state = dict(gla=tr_state(state_gla), ret=tr_state(state_ret),
                     s5_re=s5_state(state_s5_re), s5_im=s5_state(state_s5_im))

    tm_ctx = min(lp, DENSE_TM)
    tm_lat = min(ls, DENSE_TM)
    yp = x_prompt.reshape(bp * lp, d)
    ys = x_sample.reshape(bs * ls, d)
    outs = []
    for l in range(depth):
        lre, lim, bbre, bbim = _s5_prep(s5_a_re[l], s5_a_im[l], s5_log_step[l], s5_b_re[l], s5_b_im[l])
        wb, cre, cim = _s5_block_weights(bbre, bbim, s5_c_re[l], s5_c_im[l])
        lam = jnp.stack([lre.reshape(2, ntile, S5_SW), lim.reshape(2, ntile, S5_SW)], axis=2)
        wg, bg, nw = _gla_gate_params(gla_w_gate[l], gla_b_gate[l], gla_norm_w[l])
        w1p = jnp.pad(hy_f_w1[l], ((0, LANES - hy_f_w1.shape[1]), (0, 0)))
        p = dict(
            w_in=_perm_w_in(w_in[l]),
            gla_wg=wg, gla_bg=bg, gla_nw=nw,
            ret_dexp=jnp.broadcast_to(ret_decay_exp[l][:, None, :, None], (2, SUBLANES, HEADS, LANES)).reshape(2, SUBLANES, GRP),
            s5_wb=wb, s5_lam=lam, s5_cre=cre, s5_cim=cim, s5_d=s5_d[l][None, :],
            s5_glu_w=s5_glu_w[l].astype(bf16), s5_glu_b=s5_glu_b[l][None, :],
            hy_cw=hy_conv_w[l], hy_cb=hy_conv_b[l][None, :],
            hy_w1=w1p, hy_b1=hy_f_b1[l][None, :], hy_w2=hy_f_w2[l], hy_b2=hy_f_b2[l][None, :],
            hy_freq=hy_f_freq[l][None, :], hy_w3=hy_f_w3[l], hy_decay=hy_decay[l][None, :],
            hy_d=hy_d[l][:, None, :],
            w_out=w_out[l].astype(bf16), ln1_g=ln1_g[l][None, :], ln1_b=ln1_b[l][None, :],
            w_up=w_up[l].astype(bf16), w_down=w_down[l].astype(bf16),
            ln2_g=ln2_g[l][None, :], ln2_b=ln2_b[l][None, :],
        )
        yp, st = _layer_group(yp, mod, l, p, None, nbatch=bp, seq=lp, rows=1, tm=tm_ctx,
                              row_of=lambda tile: (lambda i: 0), want_state=True, alpha=alpha)
        outs.append(st)
        ys, _ = _layer_group(ys, mod, l, p, lat_state, nbatch=bs, seq=ls, rows=ls // GRID_W, tm=tm_lat,
                             row_of=lambda tile: (lambda i: 1 + i // (ls // tile)), want_state=False, alpha=alpha)

    new_gla = jnp.stack([jnp.swapaxes(o[0], -1, -2) for o in outs], axis=1)
    new_ret = jnp.stack([jnp.swapaxes(o[1], -1, -2) for o in outs], axis=1)
    unpack = lambda h: jnp.transpose(h, (2, 0, 1, 3)).reshape(bp, 2, ngroup, S5_STATE)
    new_re = jnp.stack([unpack(o[2]) for o in outs], axis=1)
    new_im = jnp.stack([unpack(o[3]) for o in outs], axis=1)
    return (yp.reshape(bp, lp, d), ys.reshape(bs, ls, d), new_gla, new_ret, new_re, new_im)
```

```python
import functools
import math

import numpy as np
import jax
import jax.numpy as jnp
from jax import lax
from jax.experimental import pallas as pl
from jax.experimental.pallas import tpu as pltpu

f32 = jnp.float32
bf16 = jnp.bfloat16

GRID_W = 64
HEADS = 4
DK = 64
DV = 128
GLA_RANK = 16
GLA_TAU = 16.0
S5_GROUP = 16
S5_STATE = 64
HY_BANDS = 16
HY_SHORT = 3
CHUNK = 64
RET_CHUNK = 128
LN_EPS = 1e-5
NORM_EPS = 1e-6

LANES = 128
SUBLANES = 8
VMEM_LIMIT = 52 * 1024 * 1024

GRP = HEADS * DV
OFF_GLA = 0
OFF_RET = 3 * GRP
OFF_S5 = 6 * GRP
OFF_HY = 7 * GRP
OFF_LR = 10 * GRP
NP_IN = 5376
TN_IN = 1792
DENSE_TM = 512
MLP_TF = 1024
OUT_SPLIT = 2


def _cparams(sem, vmem=VMEM_LIMIT):
    return pltpu.CompilerParams(dimension_semantics=sem, vmem_limit_bytes=vmem)


def _ln(x):
    mu = jnp.mean(x, axis=-1, keepdims=True)
    xc = x - mu
    var = jnp.mean(xc * xc, axis=-1, keepdims=True)
    return xc * lax.rsqrt(var + LN_EPS)


def _silu(x):
    return x * jax.nn.sigmoid(x)


def _bdot(a, b):
    return jnp.dot(a.astype(bf16), b.astype(bf16), preferred_element_type=f32)


def _bdot_nt(a, b):
    return lax.dot_general(a.astype(bf16), b.astype(bf16), (((1,), (1,)), ((), ())), preferred_element_type=f32)


def _bdot_tn(a, b):
    return lax.dot_general(a.astype(bf16), b.astype(bf16), (((0,), (0,)), ((), ())), preferred_element_type=f32)


def _dot_exact_lhs(a_bf, x):
    hi = x.astype(bf16)
    lo = (x - hi.astype(f32)).astype(bf16)
    d = lambda b: jnp.dot(a_bf, b, preferred_element_type=f32)
    return d(hi) + d(lo)


def _dot3(a, b):
    ah = a.astype(bf16)
    al = (a - ah.astype(f32)).astype(bf16)
    bh = b.astype(bf16)
    bl = (b - bh.astype(f32)).astype(bf16)
    d = lambda p, q: jnp.dot(p, q, preferred_element_type=f32)
    return d(ah, bh) + (d(al, bh) + d(ah, bl))


def _ada_kernel(c_ref, w_ref, b_ref, o_ref):
    s = _silu(c_ref[...])
    o_ref[...] = _bdot(s, w_ref[...]) + b_ref[...]


def _ada(c_rows, ada_w, ada_b):
    depth, d, n = ada_w.shape
    r = c_rows.shape[0]
    tn = 1024
    return pl.pallas_call(
        _ada_kernel,
        out_shape=jax.ShapeDtypeStruct((depth, r, n), f32),
        grid=(depth, n // tn),
        in_specs=[pl.BlockSpec((r, d), lambda l, j: (0, 0)),
                  pl.BlockSpec((None, d, tn), lambda l, j: (l, 0, j)),
                  pl.BlockSpec((None, 1, tn), lambda l, j: (l, 0, j))],
        out_specs=pl.BlockSpec((None, r, tn), lambda l, j: (l, 0, j)),
        compiler_params=_cparams(("parallel", "parallel")),
        name="ada",
    )(c_rows, ada_w, ada_b.reshape(depth, 1, n))


def _in_kernel(x_ref, mod_ref, w_ref, o_ref, u_ref, hy_ref, h_ref, *, nsub, sub):
    j = pl.program_id(1)

    @pl.when(j == 0)
    def _():
        y = _ln(x_ref[...])
        h_ref[...] = (y * (1.0 + mod_ref[1:2, :]) + mod_ref[0:1, :]).astype(bf16)

    cols = pl.ds(pl.multiple_of(j * TN_IN, LANES), TN_IN)
    o_ref[...] = jnp.dot(h_ref[...], w_ref[:, cols], preferred_element_type=f32)

    @pl.when(j == OFF_S5 // TN_IN)
    def _():
        c0 = OFF_S5 % TN_IN
        for k in range(nsub):
            u_ref[:, k * GRP:(k + 1) * GRP] = o_ref[k * sub:(k + 1) * sub, c0:c0 + GRP]

    @pl.when(j == OFF_HY // TN_IN)
    def _():
        c0 = OFF_HY % TN_IN
        hy_ref[...] = o_ref[:, c0:c0 + 3 * GRP]


def _in_proj(x, mod, layer, row_fn, w_perm, tm, seq):
    t, d = x.shape
    assert OFF_S5 // TN_IN == (OFF_S5 + GRP - 1) // TN_IN
    assert OFF_HY // TN_IN == (OFF_HY + 3 * GRP - 1) // TN_IN
    sub = min(tm, seq)
    nsub = tm // sub
    tps = seq // sub
    return pl.pallas_call(
        functools.partial(_in_kernel, nsub=nsub, sub=sub),
        out_shape=[jax.ShapeDtypeStruct((t, NP_IN), f32), jax.ShapeDtypeStruct((seq, (t // seq) * GRP), f32),
                   jax.ShapeDtypeStruct((t, 3 * GRP), f32)],
        grid=(t // tm, NP_IN // TN_IN),
        in_specs=[pl.BlockSpec((tm, d), lambda i, j: (i, 0)),
                  pl.BlockSpec((None, None, 6, d), lambda i, j: (layer, row_fn(i), 0, 0)),
                  pl.BlockSpec((d, NP_IN), lambda i, j: (0, 0), pipeline_mode=pl.Buffered(1))],
        out_specs=[pl.BlockSpec((tm, TN_IN), lambda i, j: (i, j)),
                   pl.BlockSpec((sub, nsub * GRP), lambda i, j: (i % tps, i // tps)),
                   pl.BlockSpec((tm, 3 * GRP), lambda i, j: (i, 0))],
        scratch_shapes=[pltpu.VMEM((tm, d), bf16)],
        compiler_params=_cparams(("parallel", "arbitrary")),
        name="in_proj",
    )(x, mod, w_perm)


LA_TAIL = 256


def _linattn_kernel(*refs, mode, seq, has_s0, want_state):
    it = iter(refs)
    qk_ref, v_ref, g_ref = next(it), next(it), next(it)
    if mode == "gla":
        lr_ref, wg_ref, bg_ref, nw_ref = next(it), next(it), next(it), next(it)
    else:
        dexp_ref = next(it)
    s0_ref = next(it) if has_s0 else None
    o_ref = next(it)
    sf_ref = next(it) if want_state else None
    ob_ref, st_ref = next(it), next(it)
    if mode == "gla":
        qkd_ref, ku_ref, dec_ref = next(it), next(it), next(it)

    c = CHUNK if mode == "gla" else RET_CHUNK
    nchunk = seq // c
    scale = DK ** -0.5
    row = lax.broadcasted_iota(jnp.int32, (c, c), 0)
    col = lax.broadcasted_iota(jnp.int32, (c, c), 1)
    is_k = (lax.broadcasted_iota(jnp.int32, (1, GRP), 1) & DK) != 0
    hq = lambda a, h: a[:, h * LANES:h * LANES + DK]
    hk = lambda a, h: a[:, h * LANES + DK:(h + 1) * LANES]
    hv = lambda h: slice(h * DV, (h + 1) * DV)

    if has_s0:
        st_ref[...] = s0_ref[...]
    else:
        st_ref[...] = jnp.zeros_like(st_ref)

    per_dir = []
    for d in (0, 1):
        causal = (row >= col) if d == 0 else (row <= col)
        if mode == "gla":
            tri = jnp.where(causal, 1.0, 0.0).astype(bf16)
            sgn = jnp.where(is_k, -1.0, 1.0)
            qsc = jnp.where(is_k, 1.0, scale)
            per_dir.append((causal, tri, sgn, qsc))
        else:
            lgr = jnp.log1p(-jnp.exp2(-dexp_ref[d]))[0:1, :]
            rowg = lax.broadcasted_iota(jnp.int32, (c, GRP), 0)
            pos = rowg if d == 0 else (c - 1 - rowg)
            pw = jnp.where(is_k, c - 1 - pos, pos + 1).astype(f32)
            wqk = jnp.exp(lgr * pw) * jnp.where(is_k, scale, 1.0)
            dist = ((row - col) if d == 0 else (col - row)).astype(f32)
            masks = [jnp.where(causal, jnp.exp(jnp.broadcast_to(lgr[:, h * LANES:h * LANES + c], (c, c)) * dist), 0.0) * scale
                     for h in range(HEADS)]
            dec = jnp.exp(lgr * float(c))
            per_dir.append((wqk, masks, dec))

    if mode == "gla":
        def pre(n, carry):
            rows = pl.ds(pl.multiple_of(n * c, c), c)
            qk = qk_ref[rows, :]
            lrc = lr_ref[rows, :]
            logits = [_bdot(lrc, wg_ref[d]) + bg_ref[d] for d in (0, 1)]
            la = [(jnp.minimum(x, 0.0) - jnp.log(1.0 + jnp.exp(-jnp.abs(x)))) * (1.0 / GLA_TAU) for x in logits]
            b = [_dot_exact_lhs(per_dir[d][1], la[d]) for d in (0, 1)]
            for d in (0, 1):
                _, _, sgn, qsc = per_dir[d]
                last = c - 1 if d == 0 else 0
                btot = b[d][last:last + 1, :]
                qkd_ref[d, rows, :] = (qk * (jnp.exp(b[d] * sgn) * qsc)).astype(bf16)
                ku_ref[d, rows, :] = (qk * jnp.exp(btot - b[d])).astype(bf16)
                dec_ref[d, n] = jnp.exp(btot)
            return carry

        lax.fori_loop(0, nchunk, pre, 0, unroll=2)

    def body(i, carry):
        rows, qka, qko, ku, dec = [], [], [], [], []
        for d in (0, 1):
            n = i if d == 0 else nchunk - 1 - i
            rows.append(pl.ds(pl.multiple_of(n * c, c), c))
            if mode == "gla":
                qka.append(qkd_ref[d, rows[d], :])
                qko.append(qka[d])
                ku.append(ku_ref[d, rows[d], :])
                dec.append(dec_ref[d, n])
            else:
                wqk, _, dcy = per_dir[d]
                qka.append(qk_ref[rows[d], :])
                qko.append(qka[d] * wqk)
                ku.append(qko[d])
                dec.append(dcy)
        chains = [(d, h) for d in (0, 1) for h in range(HEADS)]
        v = {(d, h): v_ref[rows[d], hv(h)] for d, h in chains}
        st = {(d, h): st_ref[d, h] for d, h in chains}
        att = {(d, h): _bdot_nt(hq(qka[d], h), hk(qka[d], h)) for d, h in chains}
        oi = {(d, h): _bdot_nt(hq(qko[d], h), st[d, h]) for d, h in chains}
        up = {(d, h): _bdot_tn(v[d, h], hk(ku[d], h)) for d, h in chains}
        for d, h in chains:
            if mode == "gla":
                a = jnp.where(per_dir[d][0], att[d, h], 0.0)
            else:
                a = att[d, h] * per_dir[d][1][h]
            o = _bdot(a, v[d, h]) + oi[d, h]
            st_ref[d, h] = hq(dec[d], h) * st[d, h] + up[d, h]
            if d == 0:
                o_ref[rows[d], hv(h)] = o
            else:
                ob_ref[rows[d], hv(h)] = o
        return carry

    lax.fori_loop(0, nchunk, body, 0, unroll=4 if mode == "gla" else 2)
    if want_state:
        sf_ref[...] = st_ref[...]

    tl = min(LA_TAIL, seq)

    def tail(i, carry):
        rows = pl.ds(pl.multiple_of(i * tl, tl), tl)
        for h in range(HEADS):
            o = o_ref[rows, hv(h)] + ob_ref[rows, hv(h)]
            if mode == "gla":
                o = o * lax.rsqrt(jnp.mean(o * o, axis=-1, keepdims=True) + NORM_EPS) * nw_ref[:, hv(h)]
            else:
                o = _ln(o)
            o_ref[rows, hv(h)] = o * _silu(g_ref[rows, hv(h)])
        return carry

    lax.fori_loop(0, seq // tl, tail, 0)


def _linattn(proj, col0, extra, s0t, layer, *, mode, nbatch, seq, want_state):
    cb = col0 // GRP
    full = lambda a: pl.BlockSpec(a.shape, lambda b: (0,) * a.ndim)
    in_specs = [pl.BlockSpec((seq, GRP), lambda b: (b, cb)),
                pl.BlockSpec((seq, GRP), lambda b: (b, cb + 1)),
                pl.BlockSpec((seq, GRP), lambda b: (b, cb + 2))]
    args = [proj, proj, proj]
    scratch = [pltpu.VMEM((seq, GRP), f32), pltpu.VMEM((2, HEADS, DV, DK), f32)]
    if mode == "gla":
        scratch += [pltpu.VMEM((2, seq, GRP), bf16), pltpu.VMEM((2, seq, GRP), bf16),
                    pltpu.VMEM((2, seq // CHUNK, 1, GRP), f32)]
    if mode == "gla":
        in_specs += [pl.BlockSpec((seq, LANES), lambda b: (b, OFF_LR // LANES))] + [full(a) for a in extra]
        args += [proj, *extra]
    else:
        in_specs += [full(a) for a in extra]
        args += list(extra)
    has_s0 = s0t is not None
    if has_s0:
        in_specs += [pl.BlockSpec((None, None, 2, HEADS, DV, DK), lambda b: (b, layer, 0, 0, 0, 0))]
        args += [s0t]
    out_shape = [jax.ShapeDtypeStruct((nbatch * seq, GRP), f32)]
    out_specs = [pl.BlockSpec((seq, GRP), lambda b: (b, 0))]
    if want_state:
        out_shape += [jax.ShapeDtypeStruct((nbatch, 2, HEADS, DV, DK), f32)]
        out_specs += [pl.BlockSpec((None, 2, HEADS, DV, DK), lambda b: (b, 0, 0, 0, 0))]
    res = pl.pallas_call(
        functools.partial(_linattn_kernel, mode=mode, seq=seq, has_s0=has_s0, want_state=want_state),
        out_shape=out_shape,
        grid=(nbatch,),
        in_specs=in_specs,
        out_specs=out_specs,
        scratch_shapes=scratch,
        compiler_params=_cparams(("parallel",)),
        name="linattn_" + mode,
    )(*args)
    return (res[0], res[1]) if want_state else (res[0], None)


def _s5_prep_kernel(are_ref, aim_ref, lstep_ref, bre_ref, bim_ref, lre_ref, lim_ref, bbre_ref, bbim_ref):
    ar, ai = are_ref[...], aim_ref[...]
    st = jnp.exp(lstep_ref[...])
    mag = jnp.exp(ar * st)
    lr = mag * jnp.cos(ai * st)
    li = mag * jnp.sin(ai * st)
    nr, ni = lr - 1.0, li
    den = ar * ar + ai * ai
    kr = (nr * ar + ni * ai) / den
    ki = (ni * ar - nr * ai) / den
    br, bi = bre_ref[...], bim_ref[...]
    lre_ref[...] = lr
    lim_ref[...] = li
    bbre_ref[...] = kr * br - ki * bi
    bbim_ref[...] = kr * bi + ki * br


def _s5_prep(a_re, a_im, log_step, b_re, b_im):
    _, g, p = a_re.shape
    hch = b_re.shape[-1]
    rows = 2 * g * hch
    bc = lambda a: jnp.broadcast_to(a[:, :, None, :], (2, g, hch, p)).reshape(rows, p)
    tr = lambda b: jnp.transpose(b, (0, 1, 3, 2)).reshape(rows, p)
    ls = jnp.broadcast_to(log_step[:, :, None, None], (2, g, hch, p)).reshape(rows, p)
    spec = pl.BlockSpec((rows, p), lambda: (0, 0))
    lre, lim, bbre, bbim = pl.pallas_call(
        _s5_prep_kernel,
        out_shape=[jax.ShapeDtypeStruct((rows, p), f32)] * 4,
        in_specs=[spec] * 5,
        out_specs=[spec] * 4,
        name="s5_prep",
    )(bc(a_re), bc(a_im), ls, tr(b_re), tr(b_im))
    r4 = lambda a: a.reshape(2, g, hch, p)
    return r4(lre)[:, :, 0, :], r4(lim)[:, :, 0, :], r4(bbre), r4(bbim)


S5_TC = 64
S5_GPT = LANES // S5_GROUP
S5_SW = S5_GPT * S5_STATE


S5_TB = 256


def _s5_scan_kernel(*refs, tb, nb, has_h0, want_state):
    it = iter(refs)
    uf_ref, ub_ref, wb_ref, lam_ref, cre_ref, cim_ref, dsk_ref = (next(it) for _ in range(7))
    h0re_ref = next(it) if has_h0 else None
    h0im_ref = next(it) if has_h0 else None
    yf_ref, yb_ref = next(it), next(it)
    hfre_ref = next(it) if want_state else None
    hfim_ref = next(it) if want_state else None
    bu_ref, hs_ref, st_ref = next(it), next(it), next(it)

    s = pl.program_id(1)
    tc = S5_TC
    nchunk = tb // tc
    sw = S5_SW
    u_refs, y_refs = (uf_ref, ub_ref), (yf_ref, yb_ref)

    @pl.when(s == 0)
    def _():
        if has_h0:
            for d in (0, 1):
                st_ref[2 * d] = h0re_ref[d]
                st_ref[2 * d + 1] = h0im_ref[d]
        else:
            st_ref[...] = jnp.zeros_like(st_ref)

    lam = [(jnp.broadcast_to(lam_ref[d, 0:1, :], (nb, sw)), jnp.broadcast_to(lam_ref[d, 1:2, :], (nb, sw)))
           for d in (0, 1)]

    def chunk_pair(ip, carry):
        h = [list(carry[0:2]), list(carry[2:4])]
        for par in (0, 1):
            c = 2 * ip + par
            t0 = [pl.multiple_of(c * tc, tc), pl.multiple_of((nchunk - 1 - c) * tc, tc)]
            ub = [u_refs[d][pl.ds(t0[d], tc)] for d in (0, 1)]
            for d in (0, 1):
                bu_ref[par, d] = _bdot(ub[d].reshape(tc * nb, LANES), wb_ref[d])
            for k in range(tc):
                for d in (0, 1):
                    t = k if d == 0 else tc - 1 - k
                    rs = slice(t * nb, (t + 1) * nb)
                    lam_r, lam_i = lam[d]
                    hre, him = h[d]
                    nre = lam_r * hre - lam_i * him + bu_ref[par, d, rs, 0:sw]
                    nim = lam_r * him + lam_i * hre + bu_ref[par, d, rs, sw:2 * sw]
                    h[d] = [nre, nim]
                    hs_ref[par, d, rs, 0:sw] = nre
                    hs_ref[par, d, rs, sw:2 * sw] = nim
            for d in (0, 1):
                y = _bdot(hs_ref[par, d, :, 0:sw], cre_ref[d]) - _bdot(hs_ref[par, d, :, sw:2 * sw], cim_ref[d])
                y = y.reshape(tc, nb, LANES)
                if d == 0:
                    y = y + ub[0] * dsk_ref[...]
                y_refs[d][pl.ds(t0[d], tc)] = y
        return (*h[0], *h[1])

    hfin = lax.fori_loop(0, nchunk // 2, chunk_pair, tuple(st_ref[q] for q in range(4)))
    for q in range(4):
        st_ref[q] = hfin[q]

    if want_state:
        @pl.when(s == pl.num_programs(1) - 1)
        def _():
            for d in (0, 1):
                hfre_ref[d] = hfin[2 * d]
                hfim_ref[d] = hfin[2 * d + 1]


def _s5_scan(u_t, wb, lam, cre, cim, dsk, h0, *, want_state):
    seq, nb, ch = u_t.shape
    ntile = ch // LANES
    sw = S5_SW
    tb = min(S5_TB, seq)
    nblk = seq // tb
    fwd = lambda j, s: (s, 0, j)
    bwd = lambda j, s: (nblk - 1 - s, 0, j)
    par = lambda shape: pl.BlockSpec(shape, lambda j, s: (0, j, 0, 0))
    in_specs = [pl.BlockSpec((tb, nb, LANES), fwd), pl.BlockSpec((tb, nb, LANES), bwd),
                par((2, None, LANES, 2 * sw)), par((2, None, 2, sw)),
                par((2, None, sw, LANES)), par((2, None, sw, LANES)),
                pl.BlockSpec((1, LANES), lambda j, s: (0, j))]
    args = [u_t, u_t, wb, lam, cre, cim, dsk]
    has_h0 = h0 is not None
    st_spec = par((2, None, nb, sw))
    if has_h0:
        in_specs += [st_spec, st_spec]
        args += list(h0)
    out_shape = [jax.ShapeDtypeStruct((seq, nb, ch), f32)] * 2
    out_specs = [pl.BlockSpec((tb, nb, LANES), fwd), pl.BlockSpec((tb, nb, LANES), bwd)]
    if want_state:
        out_shape += [jax.ShapeDtypeStruct((2, ntile, nb, sw), f32)] * 2
        out_specs += [st_spec, st_spec]
    res = pl.pallas_call(
        functools.partial(_s5_scan_kernel, tb=tb, nb=nb, has_h0=has_h0, want_state=want_state),
        out_shape=out_shape,
        grid=(ntile, nblk),
        in_specs=in_specs,
        out_specs=out_specs,
        scratch_shapes=[pltpu.VMEM((2, 2, S5_TC * nb, 2 * sw), f32), pltpu.VMEM((2, 2, S5_TC * nb, 2 * sw), f32),
                        pltpu.VMEM((4, nb, sw), f32)],
        compiler_params=_cparams(("parallel", "arbitrary")),
        name="s5_scan",
    )(*args)
    return (res[0], res[1], res[2], res[3]) if want_state else (res[0], res[1], None, None)


HY_TK = 256
HY_CT = 256


def _pk_mul(xr, xi, a, b, d):
    return xr * a - xi * b, xr * b + xi * d


def _hy_filter_kernel(fte_ref, fto_ref, w1_ref, b1_ref, w2_ref, b2_ref, fq_ref, w3f_ref, w3b_ref, dcf_ref, dcb_ref,
                      fre_ref, fim_ref, cph_ref, sph_ref,
                      fea_ref, feb_ref, foa_ref, fob_ref, ga_ref, gb_ref, ny_ref, xs_ref, he_ref, ho_ref, *, lh, tk):
    j = pl.program_id(1)
    ct = HY_CT

    @pl.when(jnp.logical_and(j == 0, pl.program_id(0) == 0))
    def _():
        fq = fq_ref[...]
        for feat_ref, h_ref in ((fte_ref, he_ref), (fto_ref, ho_ref)):
            h = jnp.sin(fq * (_dot3(feat_ref[...], w1_ref[...]) + b1_ref[...]))
            h_ref[...] = jnp.sin(fq * (_dot3(h, w2_ref[...]) + b2_ref[...]))

    @pl.when(j == 0)
    def _():
        he, te = he_ref[...], fte_ref[:, 0:1]
        ho, to = ho_ref[...], fto_ref[:, 0:1]
        raw = lambda h, t, w3_ref, dc_ref: _dot3(h, w3_ref[...]) * jnp.exp(-t * jnp.abs(dc_ref[...]))
        ffe, ffo = raw(he, te, w3f_ref, dcf_ref), raw(ho, to, w3f_ref, dcf_ref)
        fbe, fbo = raw(he, te, w3b_ref, dcb_ref), raw(ho, to, w3b_ref, dcb_ref)
        sf = jnp.sum(jnp.abs(ffe), axis=0, keepdims=True) + jnp.sum(jnp.abs(ffo), axis=0, keepdims=True)
        sb = jnp.sum(jnp.abs(fbe), axis=0, keepdims=True) + jnp.sum(jnp.abs(fbo), axis=0, keepdims=True)
        rowi = lax.broadcasted_iota(jnp.int32, (lh, ct), 0)
        xs_ref[:, 0 * ct:1 * ct] = (ffe / sf).astype(bf16)
        xs_ref[:, 1 * ct:2 * ct] = jnp.where(rowi == 0, 0.0, fbe / sb).astype(bf16)
        xs_ref[:, 2 * ct:3 * ct] = (ffo / sf).astype(bf16)
        xs_ref[:, 3 * ct:4 * ct] = (fbo / sb).astype(bf16)

    xs = xs_ref[...]
    r = jnp.dot(fre_ref[...], xs, preferred_element_type=f32)
    i = jnp.dot(fim_ref[...], xs, preferred_element_type=f32)
    r1, r2, r3, r4 = (r[:, q * ct:(q + 1) * ct] for q in range(4))
    i1, i2, i3, i4 = (i[:, q * ct:(q + 1) * ct] for q in range(4))
    c, s = cph_ref[...], sph_ref[...]
    fe_re, fe_im = r1 + r2, i1 - i2
    fo_re = r3 + c * r4 + s * i4
    fo_im = i3 + s * r4 - c * i4
    g_re = c * fo_re + s * fo_im
    g_im = c * fo_im - s * fo_re
    scl = 1.0 / lh
    first = jnp.logical_and(lax.broadcasted_iota(jnp.int32, (tk, ct), 0) == 0, j == 0)
    for a_ref, b_ref, re, im in ((fea_ref, feb_ref, fe_re, fe_im), (foa_ref, fob_ref, fo_re, fo_im),
                                 (ga_ref, gb_ref, g_re, g_im)):
        a_ref[...] = jnp.where(first, 0.5 * scl, scl) * re
        b_ref[...] = jnp.where(first, 0.0, scl * im)

    @pl.when(j == 0)
    def _():
        fo_ny = i3[0:1, :] - i4[0:1, :]
        ny_ref[...] = jnp.zeros_like(ny_ref)
        ny_ref[0:1, :] = (0.5 * scl) * (i1[0:1, :] + i2[0:1, :])
        ny_ref[1:2, :] = (0.5 * scl) * fo_ny
        ny_ref[2:3, :] = (-0.5 * scl) * fo_ny


def _hy_filter(feat_e, feat_o, w1p, b1, w2, b2, fq, w3, dc, fre, fim, cph, sph):
    lh = feat_e.shape[0]
    ncol = w3.shape[1] // 2
    nct = ncol // HY_CT
    tk = min(HY_TK, lh)
    full = lambda a: pl.BlockSpec(a.shape, lambda c, j: (0,) * a.ndim)
    out_spec = pl.BlockSpec((tk, HY_CT), lambda c, j: (j, c))
    return pl.pallas_call(
        functools.partial(_hy_filter_kernel, lh=lh, tk=tk),
        out_shape=[jax.ShapeDtypeStruct((lh, ncol), f32)] * 6 + [jax.ShapeDtypeStruct((SUBLANES, ncol), f32)],
        grid=(nct, lh // tk),
        in_specs=[full(feat_e), full(feat_o), full(w1p), full(b1), full(w2), full(b2), full(fq),
                  pl.BlockSpec((w3.shape[0], HY_CT), lambda c, j: (0, c)),
                  pl.BlockSpec((w3.shape[0], HY_CT), lambda c, j: (0, nct + c)),
                  pl.BlockSpec((1, HY_CT), lambda c, j: (0, c)),
                  pl.BlockSpec((1, HY_CT), lambda c, j: (0, nct + c)),
                  pl.BlockSpec((tk, lh), lambda c, j: (j, 0)),
                  pl.BlockSpec((tk, lh), lambda c, j: (j, 0)),
                  pl.BlockSpec((tk, HY_CT), lambda c, j: (j, 0)),
                  pl.BlockSpec((tk, HY_CT), lambda c, j: (j, 0))],
        out_specs=[out_spec] * 6 + [pl.BlockSpec((SUBLANES, HY_CT), lambda c, j: (0, c))],
        scratch_shapes=[pltpu.VMEM((lh, 4 * HY_CT), bf16), pltpu.VMEM((lh, w2.shape[1]), f32),
                        pltpu.VMEM((lh, w2.shape[1]), f32)],
        compiler_params=_cparams(("arbitrary", "arbitrary")),
        name="hy_filter",
    )(feat_e, feat_o, w1p, b1, w2, b2, fq, w3, w3, dc, dc, fre, fim, cph, sph)


def _short_conv(x, w_ref, b_ref, seg):
    rows = x.shape[0]
    r = lax.broadcasted_iota(jnp.int32, x.shape, 0) & (seg - 1)
    xp = jnp.where(r == 0, 0.0, pltpu.roll(x, 1, 0))
    xn = jnp.where(r == seg - 1, 0.0, pltpu.roll(x, rows - 1, 0))
    return xp * w_ref[0:1, :] + x * w_ref[1:2, :] + xn * w_ref[2:3, :] + b_ref[...]


HY_PB = 256


def _longconv_kernel(yin_ref, gate_ref, cwy_ref, cby_ref, cwg_ref, cbg_ref, fre_ref, fim_ref, iim_ref,
                     fea_ref, feb_ref, foa_ref, fob_ref, ga_ref, gb_ref, ny_ref, dsk_ref,
                     o_ref, y32_ref, ybf_ref, acc_ref, *, seg, conv_yin, tk):
    j = pl.program_id(1)
    g = GRP
    seq = yin_ref.shape[0]
    pb = min(HY_PB, seq)
    hb = pb // 2
    pr = lax.broadcasted_iota(jnp.int32, (pb, pb), 0)
    pc = lax.broadcasted_iota(jnp.int32, (pb, pb), 1)

    @pl.when(j == 0)
    def _():
        y = yin_ref[...]
        if conv_yin:
            y = _short_conv(y, cwy_ref, cby_ref, seg)
        y32_ref[...] = y
        split = jnp.where(pc == jnp.where(pr < hb, 2 * pr, 2 * (pr - hb) + 1), 1.0, 0.0).astype(bf16)
        for k in range(seq // pb):
            t = jnp.dot(split, y[k * pb:(k + 1) * pb, :].astype(bf16), preferred_element_type=f32).astype(bf16)
            ybf_ref[k * hb:(k + 1) * hb, 0:g] = t[0:hb, :]
            ybf_ref[k * hb:(k + 1) * hb, g:2 * g] = t[hb:pb, :]
        acc_ref[...] = jnp.zeros_like(acc_ref)

    freq = pl.ds(pl.multiple_of(j * tk, tk), tk)
    yb = ybf_ref[...]
    r = jnp.dot(fre_ref[freq, :], yb, preferred_element_type=f32)
    i = jnp.dot(fim_ref[freq, :], yb, preferred_element_type=f32)
    first = jnp.logical_and(lax.broadcasted_iota(jnp.int32, (tk, g), 0) == 0, j == 0)
    fe_a, fo_a, g_a = fea_ref[...], foa_ref[...], ga_ref[...]
    f_e = (fe_a, feb_ref[...], jnp.where(first, ny_ref[0:1, :], fe_a))
    f_o = (fo_a, fob_ref[...], jnp.where(first, ny_ref[1:2, :], fo_a))
    f_g = (g_a, gb_ref[...], jnp.where(first, ny_ref[2:3, :], g_a))
    er, ei, orr, oi = r[:, 0:g], i[:, 0:g], r[:, g:2 * g], i[:, g:2 * g]
    p1, p2 = _pk_mul(er, ei, *f_e), _pk_mul(orr, oi, *f_g)
    p3, p4 = _pk_mul(er, ei, *f_o), _pk_mul(orr, oi, *f_e)
    zr = jnp.concatenate([p1[0] + p2[0], p3[0] + p4[0]], axis=1).astype(bf16)
    zi = jnp.concatenate([p1[1] + p2[1], p3[1] + p4[1]], axis=1).astype(bf16)
    inv = jnp.concatenate([fre_ref[:, freq], iim_ref[:, freq]], axis=1)
    acc_ref[...] += jnp.dot(inv, jnp.concatenate([zr, zi], axis=0), preferred_element_type=f32)

    @pl.when(j == pl.num_programs(1) - 1)
    def _():
        gate = _short_conv(gate_ref[...], cwg_ref, cbg_ref, seg)
        d = dsk_ref[...]
        merge = jnp.where(pr == jnp.where(pc < hb, 2 * pc, 2 * (pc - hb) + 1), 1.0, 0.0).astype(bf16)
        for k in range(seq // pb):
            stacked = jnp.concatenate([acc_ref[k * hb:(k + 1) * hb, 0:g], acc_ref[k * hb:(k + 1) * hb, g:2 * g]], axis=0)
            conv = _dot_exact_lhs(merge, stacked)
            rows = slice(k * pb, (k + 1) * pb)
            o_ref[rows, :] = gate[rows, :] * (conv + d * y32_ref[rows, :])


def _longconv(yin, yin_cb, gate_arr, gate_cb, cw, cb, part_y, part_g, fre, fim, iim, filt, dsk, order,
              *, nbatch, seq, seg, conv_yin):
    lh = seq // 2
    tk = min(HY_TK, lh)
    once = pl.Buffered(1)
    cw_spec = lambda p: pl.BlockSpec((HY_SHORT, GRP), lambda b, j: (0, p))
    cb_spec = lambda p: pl.BlockSpec((1, GRP), lambda b, j: (0, p))
    dft_spec = pl.BlockSpec((lh, lh), lambda b, j: (0, 0), pipeline_mode=once)
    filt_spec = pl.BlockSpec((tk, GRP), lambda b, j: (j, order))
    return pl.pallas_call(
        functools.partial(_longconv_kernel, seg=seg, conv_yin=conv_yin, tk=tk),
        out_shape=jax.ShapeDtypeStruct((nbatch * seq, GRP), f32),
        grid=(nbatch, lh // tk),
        in_specs=[pl.BlockSpec((seq, GRP), lambda b, j: (b, yin_cb)),
                  pl.BlockSpec((seq, GRP), lambda b, j: (b, gate_cb)),
                  cw_spec(part_y), cb_spec(part_y), cw_spec(part_g), cb_spec(part_g),
                  dft_spec, dft_spec, dft_spec,
                  filt_spec, filt_spec, filt_spec, filt_spec, filt_spec, filt_spec,
                  pl.BlockSpec((SUBLANES, GRP), lambda b, j: (0, order)),
                  pl.BlockSpec((None, 1, GRP), lambda b, j: (order, 0, 0))],
        out_specs=pl.BlockSpec((seq, GRP), lambda b, j: (b, 0)),
        scratch_shapes=[pltpu.VMEM((seq, GRP), f32), pltpu.VMEM((lh, 2 * GRP), bf16), pltpu.VMEM((lh, 2 * GRP), f32)],
        compiler_params=_cparams(("parallel", "arbitrary")),
        name="hy_longconv",
    )(yin, gate_arr, cw, cb, cw, cb, fre, fim, iim, *filt, dsk)


def _out_kernel(m0_ref, m1_ref, y2f_ref, y2b_ref, m3_ref, gw_ref, gb_ref, x_ref, mod_ref, w_ref, g_ref, b_ref, o_ref, *, alpha):
    tm = x_ref.shape[0]
    nh = OUT_SPLIT if tm % (OUT_SPLIT * SUBLANES) == 0 else 1
    for k in range(nh):
        rows = slice(k * (tm // nh), (k + 1) * (tm // nh))
        y = y2f_ref[rows, :] + y2b_ref[rows, :]
        z = y * (0.5 * (1.0 + jnp.tanh(math.sqrt(2.0 / math.pi) * (y + 0.044715 * (y * y * y)))))
        m2 = z * jax.nn.sigmoid(_bdot(z, gw_ref[...]) + gb_ref[...])
        acc = _bdot(m0_ref[rows, :], w_ref[0 * GRP:1 * GRP, :])
        acc += _bdot(m1_ref[rows, :], w_ref[1 * GRP:2 * GRP, :])
        acc += _bdot(m2, w_ref[2 * GRP:3 * GRP, :])
        acc += _bdot(m3_ref[rows, :], w_ref[3 * GRP:4 * GRP, :])
        r = alpha * x_ref[rows, :] + mod_ref[2:3, :] * acc
        o_ref[rows, :] = _ln(r) * g_ref[...] + b_ref[...]


def _out_proj(mixes, s5_spec, glu_w, glu_b, x, mod, layer, row_fn, w_out, g, b, tm, alpha):
    t, d = x.shape
    mspec = pl.BlockSpec((tm, GRP), lambda i: (i, 0))
    const = lambda a: pl.BlockSpec(a.shape, lambda i: (0,) * a.ndim)
    return pl.pallas_call(
        functools.partial(_out_kernel, alpha=alpha),
        out_shape=jax.ShapeDtypeStruct((t, d), f32),
        grid=(t // tm,),
        in_specs=[mspec, mspec, s5_spec, s5_spec, mspec, const(glu_w), const(glu_b),
                  pl.BlockSpec((tm, d), lambda i: (i, 0)),
                  pl.BlockSpec((None, None, 6, d), lambda i: (layer, row_fn(i), 0, 0)),
                  pl.BlockSpec((d, d), lambda i: (0, 0), pipeline_mode=pl.Buffered(1)),
                  pl.BlockSpec((1, d), lambda i: (0, 0)),
                  pl.BlockSpec((1, d), lambda i: (0, 0))],
        out_specs=pl.BlockSpec((tm, d), lambda i: (i, 0)),
        compiler_params=_cparams(("parallel",)),
        name="out_proj",
    )(*mixes, glu_w, glu_b, x, mod, w_out, g, b)


def _mlp_kernel(x_ref, mod_ref, wu_ref, wd_ref, g_ref, b_ref, o_ref, h_ref, acc_ref, *, alpha):
    j = pl.program_id(1)

    @pl.when(j == 0)
    def _():
        y = _ln(x_ref[...])
        h_ref[...] = (y * (1.0 + mod_ref[4:5, :]) + mod_ref[3:4, :]).astype(bf16)
        acc_ref[...] = jnp.zeros_like(acc_ref)

    u = jnp.maximum(jnp.dot(h_ref[...], wu_ref[...], preferred_element_type=f32), 0.0)
    acc_ref[...] += jnp.dot((u * u).astype(bf16), wd_ref[...], preferred_element_type=f32)

    @pl.when(j == pl.num_programs(1) - 1)
    def _():
        r = alpha * x_ref[...] + mod_ref[5:6, :] * acc_ref[...]
        o_ref[...] = _ln(r) * g_ref[...] + b_ref[...]


def _mlp(x, mod, layer, row_fn, w_up, w_down, g, b, tm, tf, alpha):
    t, d = x.shape
    dff = w_up.shape[1]
    return pl.pallas_call(
        functools.partial(_mlp_kernel, alpha=alpha),
        out_shape=jax.ShapeDtypeStruct((t, d), f32),
        grid=(t // tm, dff // tf),
        in_specs=[pl.BlockSpec((tm, d), lambda i, j: (i, 0)),
                  pl.BlockSpec((None, None, 6, d), lambda i, j: (layer, row_fn(i), 0, 0)),
                  pl.BlockSpec((d, tf), lambda i, j: (0, j)),
                  pl.BlockSpec((tf, d), lambda i, j: (j, 0)),
                  pl.BlockSpec((1, d), lambda i, j: (0, 0)),
                  pl.BlockSpec((1, d), lambda i, j: (0, 0))],
        out_specs=pl.BlockSpec((tm, d), lambda i, j: (i, 0)),
        scratch_shapes=[pltpu.VMEM((tm, d), bf16), pltpu.VMEM((tm, d), f32)],
        compiler_params=_cparams(("parallel", "arbitrary")),
        name="mlp",
    )(x, mod, w_up, w_down, g, b)


def _perm_w_in(w):
    d = w.shape[0]
    hk, hv = HEADS * DK, HEADS * DV
    o = 0
    parts = {}
    for name, width in (("gq", hk), ("gk", hk), ("gv", hv), ("gg", GRP), ("glr", 2 * GLA_RANK),
                        ("rq", hk), ("rk", hk), ("rv", hv), ("rg", GRP), ("su", GRP), ("hy", 3 * GRP)):
        parts[name] = w[:, o:o + width]
        o += width
    qk = lambda q, k: jnp.concatenate([q.reshape(d, HEADS, DK), k.reshape(d, HEADS, DK)], axis=-1).reshape(d, 2 * hk)
    cols = [qk(parts["gq"], parts["gk"]), parts["gv"], parts["gg"],
            qk(parts["rq"], parts["rk"]), parts["rv"], parts["rg"],
            parts["su"], parts["hy"], parts["glr"]]
    wp = jnp.concatenate(cols, axis=1)
    return jnp.pad(wp, ((0, 0), (0, NP_IN - wp.shape[1]))).astype(bf16)


def _gla_gate_params(w_gate, b_gate, norm_w):
    r = w_gate.shape[1]
    dup = lambda a: jnp.concatenate([a, a], axis=-1).reshape(*a.shape[:-2], 2 * HEADS * DK)
    wg = dup(w_gate.reshape(2, r, HEADS, DK))
    full = jnp.zeros((2, LANES, 2 * HEADS * DK), f32)
    for d in range(2):
        full = full.at[d, d * r:(d + 1) * r, :].set(wg[d])
    return full.astype(bf16), dup(b_gate.reshape(2, 1, HEADS, DK)), jnp.tile(norm_w[None, :], (1, HEADS))


def _s5_block_weights(bbre, bbim, c_re, c_im):
    _, g, hch, p = bbre.shape
    ntile = g // S5_GPT
    eye = jnp.eye(S5_GPT, dtype=f32)

    def in_map(b):
        b = b.reshape(2, ntile, S5_GPT, hch, p)
        return jnp.einsum("dtghp,gk->dtghkp", b, eye).reshape(2, ntile, S5_GPT * hch, S5_GPT * p)

    def out_map(c):
        c = c.reshape(2, ntile, S5_GPT, hch, p)
        return jnp.einsum("dtghp,gk->dtgpkh", c, eye).reshape(2, ntile, S5_GPT * p, S5_GPT * hch)

    wb = jnp.concatenate([in_map(bbre), in_map(bbim)], axis=-1).astype(bf16)
    return wb, out_map(c_re).astype(bf16), out_map(c_im).astype(bf16)


@functools.lru_cache(maxsize=None)
def _dft_tables(seq):
    n = 2 * seq
    k = np.arange(seq, dtype=np.int64)[:, None]
    s1 = np.arange(seq // 64, dtype=np.int64)[None, :]
    s2 = np.arange(64, dtype=np.int64)[None, :]
    a = 2.0 * np.pi * ((k * s1 * 64) % n) / n
    b = 2.0 * np.pi * ((k * s2) % n) / n
    return tuple(np.asarray(t, np.float32) for t in (np.cos(a), np.sin(a), np.cos(b), np.sin(b)))


def _dft_mats(seq):
    ca, sa, cb, sb = (jnp.asarray(t) for t in _dft_tables(seq))
    cosm = (ca[:, :, None] * cb[:, None, :] - sa[:, :, None] * sb[:, None, :]).reshape(seq, seq)
    sinm = (sa[:, :, None] * cb[:, None, :] + ca[:, :, None] * sb[:, None, :]).reshape(seq, seq)
    sign = jnp.where((jnp.arange(seq) & 1) == 0, 1.0, -1.0).astype(f32)
    ri = jnp.arange(seq)[:, None]
    ci = jnp.arange(seq)[None, :]
    fim = jnp.where(ri == 0, sign[None, :], -sinm)
    iim = jnp.where(ci == 0, sign[:, None], -sinm)
    return cosm.astype(bf16), fim.astype(bf16), iim.astype(bf16)


@functools.lru_cache(maxsize=None)
def _hy_feats(seq):
    t = np.linspace(0.0, 1.0, seq)[:, None]
    w = 2.0 * math.pi * np.arange(seq, dtype=np.float64)[:, None] / seq
    fr = np.linspace(1e-4, HY_BANDS - 1.0, HY_BANDS)[None, :]
    feats = np.concatenate([t, np.cos(fr * w), -np.sin(fr * w)], axis=-1).astype(np.float32)
    return np.pad(feats, ((0, 0), (0, LANES - feats.shape[1])))


@functools.lru_cache(maxsize=None)
def _hy_twiddles(lh):
    ang = np.pi * np.arange(lh, dtype=np.float64)[:, None] / lh
    return np.cos(ang).astype(np.float32), np.sin(ang).astype(np.float32)


def _layer_group(x, mod, layer, p, state, *, nbatch, seq, rows, tm, row_of, want_state, alpha):
    tm_dense = min(DENSE_TM, nbatch * seq)
    proj, u_t, hy = _in_proj(x, mod, layer, row_of(tm_dense), p["w_in"], tm_dense, seq)

    o_gla, s_gla = _linattn(proj, OFF_GLA, (p["gla_wg"], p["gla_bg"], p["gla_nw"]),
                            None if state is None else state["gla"], layer,
                            mode="gla", nbatch=nbatch, seq=seq, want_state=want_state)
    o_ret, s_ret = _linattn(proj, OFF_RET, (p["ret_dexp"],),
                            None if state is None else state["ret"], layer,
                            mode="ret", nbatch=nbatch, seq=seq, want_state=want_state)

    u_t = u_t.reshape(seq, nbatch, GRP)
    h0 =None if state is None else (state["s5_re"][layer], state["s5_im"][layer])
    y_f, y_b, hf_re, hf_im = _s5_scan(u_t, p["s5_wb"], p["s5_lam"], p["s5_cre"], p["s5_cim"], p["s5_d"], h0,
                                      want_state=want_state)
    y_f, y_b = y_f.reshape(seq, nbatch * GRP), y_b.reshape(seq, nbatch * GRP)
    tpb = seq // tm
    s5_spec = pl.BlockSpec((tm, GRP), lambda i: (i % tpb, i // tpb))

    lh = seq // 2
    fre, fim, iim = _dft_mats(lh)
    feats = _hy_feats(seq)
    cph, sph = (jnp.broadcast_to(jnp.asarray(t), (lh, HY_CT)) for t in _hy_twiddles(lh))
    filt = _hy_filter(jnp.asarray(feats[0::2]), jnp.asarray(feats[1::2]), p["hy_w1"], p["hy_b1"], p["hy_w2"],
                      p["hy_b2"], p["hy_freq"], p["hy_w3"], p["hy_decay"], fre, fim, cph, sph)
    seg = seq // rows
    y1 = _longconv(hy, 0, hy, 1, p["hy_cw"], p["hy_cb"], 0, 1, fre, fim, iim, filt,
                   p["hy_d"], 0, nbatch=nbatch, seq=seq, seg=seg, conv_yin=True)
    o_hy = _longconv(y1, 0, hy, 2, p["hy_cw"], p["hy_cb"], 0, 2, fre, fim, iim, filt,
                     p["hy_d"], 1, nbatch=nbatch, seq=seq, seg=seg, conv_yin=False)

    x = _out_proj((o_gla, o_ret, y_f, y_b, o_hy), s5_spec, p["s5_glu_w"], p["s5_glu_b"], x, mod, layer, row_of(tm), p["w_out"],
                  p["ln1_g"], p["ln1_b"], tm, alpha)
    x = _mlp(x, mod, layer, row_of(tm_dense), p["w_up"], p["w_down"], p["ln2_g"], p["ln2_b"], tm_dense, MLP_TF, alpha)
    return x, (s_gla, s_ret, hf_re, hf_im)


def kernel(x_prompt, x_sample, state_gla, state_ret, state_s5_re, state_s5_im, c, c_ctx, ada_w, ada_b, w_in, gla_w_gate, gla_b_gate, gla_norm_w, ret_decay_exp, s5_a_re, s5_a_im, s5_log_step, s5_b_re, s5_b_im, s5_c_re, s5_c_im, s5_d, s5_glu_w, s5_glu_b, hy_conv_w, hy_conv_b, hy_f_w1, hy_f_b1, hy_f_w2, hy_f_b2, hy_f_freq, hy_f_w3, hy_decay, hy_d, w_out, ln1_g, ln1_b, w_up, w_down, ln2_g, ln2_b):
    bp, lp, d = x_prompt.shape
    bs, ls, _ = x_sample.shape
    depth = w_in.shape[0]
    alpha = (2 * depth) ** 0.25
    ngroup = s5_a_re.shape[2]
    ntile = ngroup // S5_GPT

    nrow = -(-(1 + bs) // SUBLANES) * SUBLANES
    c_rows = jnp.concatenate([c_ctx[None, :], c, jnp.zeros((nrow - 1 - bs, d), f32)], axis=0)
    mod = _ada(c_rows, ada_w, ada_b).reshape(depth, nrow, 6, d)

    tr_state = lambda s: jnp.swapaxes(s, -1, -2)
    s5_state = lambda s: jnp.transpose(s.reshape(bs, depth, 2, ntile, S5_SW), (1, 2, 3, 0, 4))
    lat_state = dict(gla=tr_state(state_gla), ret=tr_state(state_ret),
                     s5_re=s5_state(state_s5_re), s5_im=s5_state(state_s5_im))

    tm_ctx = min(lp, DENSE_TM)
    tm_lat = min(ls, DENSE_TM)
    yp = x_prompt.reshape(bp * lp, d)
    ys = x_sample.reshape(bs * ls, d)
    outs = []
    for l in range(depth):
        lre, lim, bbre, bbim = _s5_prep(s5_a_re[l], s5_a_im[l], s5_log_step[l], s5_b_re[l], s5_b_im[l])
        wb, cre, cim = _s5_block_weights(bbre, bbim, s5_c_re[l], s5_c_im[l])
        lam = jnp.stack([lre.reshape(2, ntile, S5_SW), lim.reshape(2, ntile, S5_SW)], axis=2)
        wg, bg, nw = _gla_gate_params(gla_w_gate[l], gla_b_gate[l], gla_norm_w[l])
        w1p = jnp.pad(hy_f_w1[l], ((0, LANES - hy_f_w1.shape[1]), (0, 0)))
        p = dict(
            w_in=_perm_w_in(w_in[l]),
            gla_wg=wg, gla_bg=bg, gla_nw=nw,
            ret_dexp=jnp.broadcast_to(ret_decay_exp[l][:, None, :, None], (2, SUBLANES, HEADS, LANES)).reshape(2, SUBLANES, GRP),
            s5_wb=wb, s5_lam=lam, s5_cre=cre, s5_cim=cim, s5_d=s5_d[l][None, :],
            s5_glu_w=s5_glu_w[l].astype(bf16), s5_glu_b=s5_glu_b[l][None, :],
            hy_cw=hy_conv_w[l], hy_cb=hy_conv_b[l][None, :],
            hy_w1=w1p, hy_b1=hy_f_b1[l][None, :], hy_w2=hy_f_w2[l], hy_b2=hy_f_b2[l][None, :],
            hy_freq=hy_f_freq[l][None, :], hy_w3=hy_f_w3[l], hy_decay=hy_decay[l][None, :],
            hy_d=hy_d[l][:, None, :],
            w_out=w_out[l].astype(bf16), ln1_g=ln1_g[l][None, :], ln1_b=ln1_b[l][None, :],
            w_up=w_up[l].astype(bf16), w_down=w_down[l].astype(bf16),
            ln2_g=ln2_g[l][None, :], ln2_b=ln2_b[l][None, :],
        )
        yp, st = _layer_group(yp, mod, l, p, None, nbatch=bp, seq=lp, rows=1, tm=tm_ctx,
                              row_of=lambda tile: (lambda i: 0), want_state=True, alpha=alpha)
        outs.append(st)
        ys, _ = _layer_group(ys, mod, l, p, lat_state, nbatch=bs, seq=ls, rows=ls // GRID_W, tm=tm_lat,
                             row_of=lambda tile: (lambda i: 1 + i // (ls // tile)), want_state=False, alpha=alpha)

    new_gla = jnp.stack([jnp.swapaxes(o[0], -1, -2) for o in outs], axis=1)
    new_ret = jnp.stack([jnp.swapaxes(o[1], -1, -2) for o in outs], axis=1)
    unpack = lambda h: jnp.transpose(h, (2, 0, 1, 3)).reshape(bp, 2, ngroup, S5_STATE)
    new_re = jnp.stack([unpack(o[2]) for o in outs], axis=1)
    new_im = jnp.stack([unpack(o[3]) for o in outs], axis=1)
    return (yp.reshape(bp, lp, d), ys.reshape(bs, ls, d), new_gla, new_ret, new_re, new_im)
```

```python
import functools
import math

import numpy as np
import jax
import jax.numpy as jnp
from jax import lax
from jax.experimental import pallas as pl
from jax.experimental.pallas import tpu as pltpu

f32 = jnp.float32
bf16 = jnp.bfloat16

GRID_W = 64
HEADS = 4
DK = 64
DV = 128
GLA_RANK = 16
GLA_TAU = 16.0
S5_GROUP = 16
S5_STATE = 64
HY_BANDS = 16
HY_SHORT = 3
CHUNK = 64
RET_CHUNK = 128
LN_EPS = 1e-5
NORM_EPS = 1e-6

LANES = 128
SUBLANES = 8
VMEM_LIMIT = 52 * 1024 * 1024
IN_VMEM_LIMIT = 57 * 1024 * 1024

GRP = HEADS * DV
OFF_GLA = 0
OFF_RET = 3 * GRP
OFF_S5 = 6 * GRP
OFF_HY = 7 * GRP
OFF_LR = 10 * GRP
NP_IN = 5376
TN_IN = 1792
DENSE_TM = 512
MLP_TF = 1024
OUT_SPLIT = 2


def _cparams(sem, vmem=VMEM_LIMIT):
    return pltpu.CompilerParams(dimension_semantics=sem, vmem_limit_bytes=vmem)


def _ln(x):
    mu = jnp.mean(x, axis=-1, keepdims=True)
    xc = x - mu
    var = jnp.mean(xc * xc, axis=-1, keepdims=True)
    return xc * lax.rsqrt(var + LN_EPS)


def _silu(x):
    return x * jax.nn.sigmoid(x)


def _bdot(a, b):
    return jnp.dot(a.astype(bf16), b.astype(bf16), preferred_element_type=f32)


def _bdot_nt(a, b):
    return lax.dot_general(a.astype(bf16), b.astype(bf16), (((1,), (1,)), ((), ())), preferred_element_type=f32)


def _bdot_tn(a, b):
    return lax.dot_general(a.astype(bf16), b.astype(bf16), (((0,), (0,)), ((), ())), preferred_element_type=f32)


def _dot_exact_lhs(a_bf, x):
    hi = x.astype(bf16)
    lo = (x - hi.astype(f32)).astype(bf16)
    d = lambda b: jnp.dot(a_bf, b, preferred_element_type=f32)
    return d(hi) + d(lo)


def _dot3(a, b):
    ah = a.astype(bf16)
    al = (a - ah.astype(f32)).astype(bf16)
    bh = b.astype(bf16)
    bl = (b - bh.astype(f32)).astype(bf16)
    d = lambda p, q: jnp.dot(p, q, preferred_element_type=f32)
    return d(ah, bh) + (d(al, bh) + d(ah, bl))


def _ada_kernel(c_ref, w_ref, b_ref, o_ref):
    s = _silu(c_ref[...])
    o_ref[...] = _bdot(s, w_ref[...]) + b_ref[...]


def _ada(c_rows, ada_w, ada_b):
    depth, d, n = ada_w.shape
    r = c_rows.shape[0]
    tn = 1024
    return pl.pallas_call(
        _ada_kernel,
        out_shape=jax.ShapeDtypeStruct((depth, r, n), f32),
        grid=(depth, n // tn),
        in_specs=[pl.BlockSpec((r, d), lambda l, j: (0, 0)),
                  pl.BlockSpec((None, d, tn), lambda l, j: (l, 0, j)),
                  pl.BlockSpec((None, 1, tn), lambda l, j: (l, 0, j))],
        out_specs=pl.BlockSpec((None, r, tn), lambda l, j: (l, 0, j)),
        compiler_params=_cparams(("parallel", "parallel")),
        name="ada",
    )(c_rows, ada_w, ada_b.reshape(depth, 1, n))


def _in_kernel(x_ref, mod_ref, w_ref, o_ref, u_ref, tail_ref, h_ref, *, nsub, sub):
    j = pl.program_id(1)
    last = pl.num_programs(1) - 1

    @pl.when(j == 0)
    def _():
        y = _ln(x_ref[...])
        h_ref[...] = (y * (1.0 + mod_ref[1:2, :]) + mod_ref[0:1, :]).astype(bf16)

    cols = pl.ds(pl.multiple_of(j * TN_IN, LANES), TN_IN)
    acc = jnp.dot(h_ref[...], w_ref[:, cols], preferred_element_type=f32)

    @pl.when(j < last)
    def _():
        o_ref[...] = acc

    @pl.when(j == OFF_S5 // TN_IN)
    def _():
        c0 = OFF_S5 % TN_IN
        for k in range(nsub):
            u_ref[:, k * GRP:(k + 1) * GRP] = acc[k * sub:(k + 1) * sub, c0:c0 + GRP]

    @pl.when(j == last)
    def _():
        tail_ref[...] = acc


def _in_proj(x, mod, layer, row_fn, w_perm, tm, seq):
    t, d = x.shape
    nj = NP_IN // TN_IN
    assert OFF_S5 // TN_IN == (OFF_S5 + GRP - 1) // TN_IN < nj - 1
    assert OFF_HY == (nj - 1) * TN_IN and OFF_LR + LANES <= NP_IN
    sub = min(tm, seq)
    nsub = tm // sub
    tps = seq // sub
    return pl.pallas_call(
        functools.partial(_in_kernel, nsub=nsub, sub=sub),
        out_shape=[jax.ShapeDtypeStruct((t, OFF_HY), f32), jax.ShapeDtypeStruct((seq, (t // seq) * GRP), f32),
                   jax.ShapeDtypeStruct((t, TN_IN), f32)],
        grid=(t // tm, nj),
        in_specs=[pl.BlockSpec((tm, d), lambda i, j: (i, 0)),
                  pl.BlockSpec((None, None, 6, d), lambda i, j: (layer, row_fn(i), 0, 0)),
                  pl.BlockSpec((d, NP_IN), lambda i, j: (0, 0), pipeline_mode=pl.Buffered(1))],
        out_specs=[pl.BlockSpec((tm, TN_IN), lambda i, j: (i, jnp.minimum(j, nj - 2))),
                   pl.BlockSpec((sub, nsub * GRP), lambda i, j: (i % tps, i // tps)),
                   pl.BlockSpec((tm, TN_IN), lambda i, j: (i, 0))],
        scratch_shapes=[pltpu.VMEM((tm, d), bf16)],
        compiler_params=_cparams(("parallel", "arbitrary"), vmem=IN_VMEM_LIMIT),
        name="in_proj",
    )(x, mod, w_perm)


LA_TAIL = 256


def _linattn_kernel(*refs, mode, seq, has_s0, want_state):
    it = iter(refs)
    qk_ref, v_ref, g_ref = next(it), next(it), next(it)
    if mode == "gla":
        lr_ref, wg_ref, bg_ref, nw_ref = next(it), next(it), next(it), next(it)
    else:
        dexp_ref = next(it)
    s0_ref = next(it) if has_s0 else None
    o_ref = next(it)
    sf_ref = next(it) if want_state else None
    ob_ref, st_ref = next(it), next(it)
    if mode == "gla":
        qkd_ref, ku_ref, dec_ref = next(it), next(it), next(it)

    c = CHUNK if mode == "gla" else RET_CHUNK
    nchunk = seq // c
    scale = DK ** -0.5
    row = lax.broadcasted_iota(jnp.int32, (c, c), 0)
    col = lax.broadcasted_iota(jnp.int32, (c, c), 1)
    is_k = (lax.broadcasted_iota(jnp.int32, (1, GRP), 1) & DK) != 0
    hq = lambda a, h: a[:, h * LANES:h * LANES + DK]
    hk = lambda a, h: a[:, h * LANES + DK:(h + 1) * LANES]
    hv = lambda h: slice(h * DV, (h + 1) * DV)

    if has_s0:
        st_ref[...] = s0_ref[...]
    else:
        st_ref[...] = jnp.zeros_like(st_ref)

    per_dir = []
    for d in (0, 1):
        causal = (row >= col) if d == 0 else (row <= col)
        if mode == "gla":
            tri = jnp.where(causal, 1.0, 0.0).astype(bf16)
            sgn = jnp.where(is_k, -1.0, 1.0)
            qsc = jnp.where(is_k, 1.0, scale)
            per_dir.append((causal, tri, sgn, qsc))
        else:
            lgr = jnp.log1p(-jnp.exp2(-dexp_ref[d]))[0:1, :]
            rowg = lax.broadcasted_iota(jnp.int32, (c, GRP), 0)
            pos = rowg if d == 0 else (c - 1 - rowg)
            pw = jnp.where(is_k, c - 1 - pos, pos + 1).astype(f32)
            wqk = jnp.exp(lgr * pw) * jnp.where(is_k, scale, 1.0)
            dist = ((row - col) if d == 0 else (col - row)).astype(f32)
            masks = [jnp.where(causal, jnp.exp(jnp.broadcast_to(lgr[:, h * LANES:h * LANES + c], (c, c)) * dist), 0.0) * scale
                     for h in range(HEADS)]
            dec = jnp.exp(lgr * float(c))
            per_dir.append((wqk, masks, dec))

    if mode == "gla":
        def pre(n, carry):
            rows = pl.ds(pl.multiple_of(n * c, c), c)
            qk = qk_ref[rows, :]
            lrc = lr_ref[rows, :]
            logits = [_bdot(lrc, wg_ref[d]) + bg_ref[d] for d in (0, 1)]
            la = [(jnp.minimum(x, 0.0) - jnp.log(1.0 + jnp.exp(-jnp.abs(x)))) * (1.0 / GLA_TAU) for x in logits]
            b = [_dot_exact_lhs(per_dir[d][1], la[d]) for d in (0, 1)]
            for d in (0, 1):
                _, _, sgn, qsc = per_dir[d]
                last = c - 1 if d == 0 else 0
                btot = b[d][last:last + 1, :]
                qkd_ref[d, rows, :] = (qk * (jnp.exp(b[d] * sgn) * qsc)).astype(bf16)
                ku_ref[d, rows, :] = (qk * jnp.exp(btot - b[d])).astype(bf16)
                dec_ref[d, n] = jnp.exp(btot)
            return carry

        lax.fori_loop(0, nchunk, pre, 0, unroll=2)

    def body(i, carry):
        rows, qka, qko, ku, dec = [], [], [], [], []
        for d in (0, 1):
            n = i if d == 0 else nchunk - 1 - i
            rows.append(pl.ds(pl.multiple_of(n * c, c), c))
            if mode == "gla":
                qka.append(qkd_ref[d, rows[d], :])
                qko.append(qka[d])
                ku.append(ku_ref[d, rows[d], :])
                dec.append(dec_ref[d, n])
            else:
                wqk, _, dcy = per_dir[d]
                qka.append(qk_ref[rows[d], :])
                qko.append(qka[d] * wqk)
                ku.append(qko[d])
                dec.append(dcy)
        chains = [(d, h) for d in (0, 1) for h in range(HEADS)]
        v = {(d, h): v_ref[rows[d], hv(h)] for d, h in chains}
        st = {(d, h): st_ref[d, h] for d, h in chains}
        att = {(d, h): _bdot_nt(hq(qka[d], h), hk(qka[d], h)) for d, h in chains}
        oi = {(d, h): _bdot_nt(hq(qko[d], h), st[d, h]) for d, h in chains}
        up = {(d, h): _bdot_tn(v[d, h], hk(ku[d], h)) for d, h in chains}
        for d, h in chains:
            if mode == "gla":
                a = jnp.where(per_dir[d][0], att[d, h], 0.0)
            else:
                a = att[d, h] * per_dir[d][1][h]
            o = _bdot(a, v[d, h]) + oi[d, h]
            st_ref[d, h] = hq(dec[d], h) * st[d, h] + up[d, h]
            if d == 0:
                o_ref[rows[d], hv(h)] = o
            else:
                ob_ref[rows[d], hv(h)] = o
        return carry

    lax.fori_loop(0, nchunk, body, 0, unroll=4 if mode == "gla" else 2)
    if want_state:
        sf_ref[...] = st_ref[...]

    tl = min(LA_TAIL, seq)

    def tail(i, carry):
        rows = pl.ds(pl.multiple_of(i * tl, tl), tl)
        for h in range(HEADS):
            o = o_ref[rows, hv(h)] + ob_ref[rows, hv(h)]
            if mode == "gla":
                o = o * lax.rsqrt(jnp.mean(o * o, axis=-1, keepdims=True) + NORM_EPS) * nw_ref[:, hv(h)]
            else:
                o = _ln(o)
            o_ref[rows, hv(h)] = o * _silu(g_ref[rows, hv(h)])
        return carry

    lax.fori_loop(0, seq // tl, tail, 0)


def _linattn(proj, col0, lr_src, extra, s0t, layer, *, mode, nbatch, seq, want_state):
    cb = col0 // GRP
    full = lambda a: pl.BlockSpec(a.shape, lambda b: (0,) * a.ndim)
    in_specs = [pl.BlockSpec((seq, GRP), lambda b: (b, cb)),
                pl.BlockSpec((seq, GRP), lambda b: (b, cb + 1)),
                pl.BlockSpec((seq, GRP), lambda b: (b, cb + 2))]
    args = [proj, proj, proj]
    scratch = [pltpu.VMEM((seq, GRP), f32), pltpu.VMEM((2, HEADS, DV, DK), f32)]
    if mode == "gla":
        scratch += [pltpu.VMEM((2, seq, GRP), bf16), pltpu.VMEM((2, seq, GRP), bf16),
                    pltpu.VMEM((2, seq // CHUNK, 1, GRP), f32)]
    if mode == "gla":
        in_specs += [pl.BlockSpec((seq, LANES), lambda b: (b, (OFF_LR - OFF_HY) // LANES))] + [full(a) for a in extra]
        args += [lr_src, *extra]
    else:
        in_specs += [full(a) for a in extra]
        args += list(extra)
    has_s0 = s0t is not None
    if has_s0:
        in_specs += [pl.BlockSpec((None, None, 2, HEADS, DV, DK), lambda b: (b, layer, 0, 0, 0, 0))]
        args += [s0t]
    out_shape = [jax.ShapeDtypeStruct((nbatch * seq, GRP), f32)]
    out_specs = [pl.BlockSpec((seq, GRP), lambda b: (b, 0))]
    if want_state:
        out_shape += [jax.ShapeDtypeStruct((nbatch, 2, HEADS, DV, DK), f32)]
        out_specs += [pl.BlockSpec((None, 2, HEADS, DV, DK), lambda b: (b, 0, 0, 0, 0))]
    res = pl.pallas_call(
        functools.partial(_linattn_kernel, mode=mode, seq=seq, has_s0=has_s0, want_state=want_state),
        out_shape=out_shape,
        grid=(nbatch,),
        in_specs=in_specs,
        out_specs=out_specs,
        scratch_shapes=scratch,
        compiler_params=_cparams(("parallel",)),
        name="linattn_" + mode,
    )(*args)
    return (res[0], res[1]) if want_state else (res[0], None)


def _s5_prep_kernel(are_ref, aim_ref, lstep_ref, bre_ref, bim_ref, lre_ref, lim_ref, bbre_ref, bbim_ref):
    ar, ai = are_ref[...], aim_ref[...]
    st = jnp.exp(lstep_ref[...])
    mag = jnp.exp(ar * st)
    lr = mag * jnp.cos(ai * st)
    li = mag * jnp.sin(ai * st)
    nr, ni = lr - 1.0, li
    den = ar * ar + ai * ai
    kr = (nr * ar + ni * ai) / den
    ki = (ni * ar - nr * ai) / den
    br, bi = bre_ref[...], bim_ref[...]
    lre_ref[...] = lr
    lim_ref[...] = li
    bbre_ref[...] = kr * br - ki * bi
    bbim_ref[...] = kr * bi + ki * br


def _s5_prep(a_re, a_im, log_step, b_re, b_im):
    _, g, p = a_re.shape
    hch = b_re.shape[-1]
    rows = 2 * g * hch
    bc = lambda a: jnp.broadcast_to(a[:, :, None, :], (2, g, hch, p)).reshape(rows, p)
    tr = lambda b: jnp.transpose(b, (0, 1, 3, 2)).reshape(rows, p)
    ls = jnp.broadcast_to(log_step[:, :, None, None], (2, g, hch, p)).reshape(rows, p)
    spec = pl.BlockSpec((rows, p), lambda: (0, 0))
    lre, lim, bbre, bbim = pl.pallas_call(
        _s5_prep_kernel,
        out_shape=[jax.ShapeDtypeStruct((rows, p), f32)] * 4,
        in_specs=[spec] * 5,
        out_specs=[spec] * 4,
        name="s5_prep",
    )(bc(a_re), bc(a_im), ls, tr(b_re), tr(b_im))
    r4 = lambda a: a.reshape(2, g, hch, p)
    return r4(lre)[:, :, 0, :], r4(lim)[:, :, 0, :], r4(bbre), r4(bbim)


S5_TC = 64
S5_GPT = LANES // S5_GROUP
S5_SW = S5_GPT * S5_STATE


S5_TB = 256


def _s5_scan_kernel(*refs, tb, nb, has_h0, want_state):
    it = iter(refs)
    uf_ref, ub_ref, wb_ref, lam_ref, cre_ref, cim_ref, dsk_ref = (next(it) for _ in range(7))
    h0re_ref = next(it) if has_h0 else None
    h0im_ref = next(it) if has_h0 else None
    yf_ref, yb_ref = next(it), next(it)
    hfre_ref = next(it) if want_state else None
    hfim_ref = next(it) if want_state else None
    bu_ref, hs_ref, st_ref = next(it), next(it), next(it)

    s = pl.program_id(1)
    tc = S5_TC
    nchunk = tb // tc
    sw = S5_SW
    u_refs, y_refs = (uf_ref, ub_ref), (yf_ref, yb_ref)

    @pl.when(s == 0)
    def _():
        if has_h0:
            for d in (0, 1):
                st_ref[2 * d] = h0re_ref[d]
                st_ref[2 * d + 1] = h0im_ref[d]
        else:
            st_ref[...] = jnp.zeros_like(st_ref)

    lam = [(jnp.broadcast_to(lam_ref[d, 0:1, :], (nb, sw)), jnp.broadcast_to(lam_ref[d, 1:2, :], (nb, sw)))
           for d in (0, 1)]

    def chunk_pair(ip, carry):
        h = [list(carry[0:2]), list(carry[2:4])]
        for par in (0, 1):
            c = 2 * ip + par
            t0 = [pl.multiple_of(c * tc, tc), pl.multiple_of((nchunk - 1 - c) * tc, tc)]
            ub = [u_refs[d][pl.ds(t0[d], tc)] for d in (0, 1)]
            for d in (0, 1):
                bu_ref[par, d] = _bdot(ub[d].reshape(tc * nb, LANES), wb_ref[d])
            for k in range(tc):
                for d in (0, 1):
                    t = k if d == 0 else tc - 1 - k
                    rs = slice(t * nb, (t + 1) * nb)
                    lam_r, lam_i = lam[d]
                    hre, him = h[d]
                    nre = lam_r * hre - lam_i * him + bu_ref[par, d, rs, 0:sw]
                    nim = lam_r * him + lam_i * hre + bu_ref[par, d, rs, sw:2 * sw]
                    h[d] = [nre, nim]
                    hs_ref[par, d, rs, 0:sw] = nre
                    hs_ref[par, d, rs, sw:2 * sw] = nim
            for d in (0, 1):
                y = _bdot(hs_ref[par, d, :, 0:sw], cre_ref[d]) - _bdot(hs_ref[par, d, :, sw:2 * sw], cim_ref[d])
                y = y.reshape(tc, nb, LANES)
                if d == 0:
                    y = y + ub[0] * dsk_ref[...]
                y_refs[d][pl.ds(t0[d], tc)] = y
        return (*h[0], *h[1])

    hfin = lax.fori_loop(0, nchunk // 2, chunk_pair, tuple(st_ref[q] for q in range(4)))
    for q in range(4):
        st_ref[q] = hfin[q]

    if want_state:
        @pl.when(s == pl.num_programs(1) - 1)
        def _():
            for d in (0, 1):
                hfre_ref[d] = hfin[2 * d]
                hfim_ref[d] = hfin[2 * d + 1]


def _s5_scan(u_t, wb, lam, cre, cim, dsk, h0, *, want_state):
    seq, nb, ch = u_t.shape
    ntile = ch // LANES
    sw = S5_SW
    tb = min(S5_TB, seq)
    nblk = seq // tb
    fwd = lambda j, s: (s, 0, j)
    bwd = lambda j, s: (nblk - 1 - s, 0, j)
    par = lambda shape: pl.BlockSpec(shape, lambda j, s: (0, j, 0, 0))
    in_specs = [pl.BlockSpec((tb, nb, LANES), fwd), pl.BlockSpec((tb, nb, LANES), bwd),
                par((2, None, LANES, 2 * sw)), par((2, None, 2, sw)),
                par((2, None, sw, LANES)), par((2, None, sw, LANES)),
                pl.BlockSpec((1, LANES), lambda j, s: (0, j))]
    args = [u_t, u_t, wb, lam, cre, cim, dsk]
    has_h0 = h0 is not None
    st_spec = par((2, None, nb, sw))
    if has_h0:
        in_specs += [st_spec, st_spec]
        args += list(h0)
    out_shape = [jax.ShapeDtypeStruct((seq, nb, ch), f32)] * 2
    out_specs = [pl.BlockSpec((tb, nb, LANES), fwd), pl.BlockSpec((tb, nb, LANES), bwd)]
    if want_state:
        out_shape += [jax.ShapeDtypeStruct((2, ntile, nb, sw), f32)] * 2
        out_specs += [st_spec, st_spec]
    res = pl.pallas_call(
        functools.partial(_s5_scan_kernel, tb=tb, nb=nb, has_h0=has_h0, want_state=want_state),
        out_shape=out_shape,
        grid=(ntile, nblk),
        in_specs=in_specs,
        out_specs=out_specs,
        scratch_shapes=[pltpu.VMEM((2, 2, S5_TC * nb, 2 * sw), f32), pltpu.VMEM((2, 2, S5_TC * nb, 2 * sw), f32),
                        pltpu.VMEM((4, nb, sw), f32)],
        compiler_params=_cparams(("parallel", "arbitrary")),
        name="s5_scan",
    )(*args)
    return (res[0], res[1], res[2], res[3]) if want_state else (res[0], res[1], None, None)


HY_TK = 256
HY_CT = 256


def _pk_mul(xr, xi, a, b, d):
    return xr * a - xi * b, xr * b + xi * d


def _hy_filter_kernel(fte_ref, fto_ref, w1_ref, b1_ref, w2_ref, b2_ref, fq_ref, w3f_ref, w3b_ref, dcf_ref, dcb_ref,
                      fre_ref, fim_ref, cph_ref, sph_ref,
                      fea_ref, feb_ref, foa_ref, fob_ref, ga_ref, gb_ref, ny_ref, xs_ref, he_ref, ho_ref, *, lh, tk):
    j = pl.program_id(1)
    ct = HY_CT

    @pl.when(jnp.logical_and(j == 0, pl.program_id(0) == 0))
    def _():
        fq = fq_ref[...]
        for feat_ref, h_ref in ((fte_ref, he_ref), (fto_ref, ho_ref)):
            h = jnp.sin(fq * (_dot3(feat_ref[...], w1_ref[...]) + b1_ref[...]))
            h_ref[...] = jnp.sin(fq * (_dot3(h, w2_ref[...]) + b2_ref[...]))

    @pl.when(j == 0)
    def _():
        he, te = he_ref[...], fte_ref[:, 0:1]
        ho, to = ho_ref[...], fto_ref[:, 0:1]
        raw = lambda h, t, w3_ref, dc_ref: _dot3(h, w3_ref[...]) * jnp.exp(-t * jnp.abs(dc_ref[...]))
        ffe, ffo = raw(he, te, w3f_ref, dcf_ref), raw(ho, to, w3f_ref, dcf_ref)
        fbe, fbo = raw(he, te, w3b_ref, dcb_ref), raw(ho, to, w3b_ref, dcb_ref)
        sf = jnp.sum(jnp.abs(ffe), axis=0, keepdims=True) + jnp.sum(jnp.abs(ffo), axis=0, keepdims=True)
        sb = jnp.sum(jnp.abs(fbe), axis=0, keepdims=True) + jnp.sum(jnp.abs(fbo), axis=0, keepdims=True)
        rowi = lax.broadcasted_iota(jnp.int32, (lh, ct), 0)
        xs_ref[:, 0 * ct:1 * ct] = (ffe / sf).astype(bf16)
        xs_ref[:, 1 * ct:2 * ct] = jnp.where(rowi == 0, 0.0, fbe / sb).astype(bf16)
        xs_ref[:, 2 * ct:3 * ct] = (ffo / sf).astype(bf16)
        xs_ref[:, 3 * ct:4 * ct] = (fbo / sb).astype(bf16)

    xs = xs_ref[...]
    r = jnp.dot(fre_ref[...], xs, preferred_element_type=f32)
    i = jnp.dot(fim_ref[...], xs, preferred_element_type=f32)
    r1, r2, r3, r4 = (r[:, q * ct:(q + 1) * ct] for q in range(4))
    i1, i2, i3, i4 = (i[:, q * ct:(q + 1) * ct] for q in range(4))
    c, s = cph_ref[...], sph_ref[...]
    fe_re, fe_im = r1 + r2, i1 - i2
    fo_re = r3 + c * r4 + s * i4
    fo_im = i3 + s * r4 - c * i4
    g_re = c * fo_re + s * fo_im
    g_im = c * fo_im - s * fo_re
    scl = 1.0 / lh
    first = jnp.logical_and(lax.broadcasted_iota(jnp.int32, (tk, ct), 0) == 0, j == 0)
    for a_ref, b_ref, re, im in ((fea_ref, feb_ref, fe_re, fe_im), (foa_ref, fob_ref, fo_re, fo_im),
                                 (ga_ref, gb_ref, g_re, g_im)):
        a_ref[...] = jnp.where(first, 0.5 * scl, scl) * re
        b_ref[...] = jnp.where(first, 0.0, scl * im)

    @pl.when(j == 0)
    def _():
        fo_ny = i3[0:1, :] - i4[0:1, :]
        ny_ref[...] = jnp.zeros_like(ny_ref)
        ny_ref[0:1, :] = (0.5 * scl) * (i1[0:1, :] + i2[0:1, :])
        ny_ref[1:2, :] = (0.5 * scl) * fo_ny
        ny_ref[2:3, :] = (-0.5 * scl) * fo_ny


def _hy_filter(feat_e, feat_o, w1p, b1, w2, b2, fq, w3, dc, fre, fim, cph, sph):
    lh = feat_e.shape[0]
    ncol = w3.shape[1] // 2
    nct = ncol // HY_CT
    tk = min(HY_TK, lh)
    full = lambda a: pl.BlockSpec(a.shape, lambda c, j: (0,) * a.ndim)
    out_spec = pl.BlockSpec((tk, HY_CT), lambda c, j: (j, c))
    return pl.pallas_call(
        functools.partial(_hy_filter_kernel, lh=lh, tk=tk),
        out_shape=[jax.ShapeDtypeStruct((lh, ncol), f32)] * 6 + [jax.ShapeDtypeStruct((SUBLANES, ncol), f32)],
        grid=(nct, lh // tk),
        in_specs=[full(feat_e), full(feat_o), full(w1p), full(b1), full(w2), full(b2), full(fq),
                  pl.BlockSpec((w3.shape[0], HY_CT), lambda c, j: (0, c)),
                  pl.BlockSpec((w3.shape[0], HY_CT), lambda c, j: (0, nct + c)),
                  pl.BlockSpec((1, HY_CT), lambda c, j: (0, c)),
                  pl.BlockSpec((1, HY_CT), lambda c, j: (0, nct + c)),
                  pl.BlockSpec((tk, lh), lambda c, j: (j, 0)),
                  pl.BlockSpec((tk, lh), lambda c, j: (j, 0)),
                  pl.BlockSpec((tk, HY_CT), lambda c, j: (j, 0)),
                  pl.BlockSpec((tk, HY_CT), lambda c, j: (j, 0))],
        out_specs=[out_spec] * 6 + [pl.BlockSpec((SUBLANES, HY_CT), lambda c, j: (0, c))],
        scratch_shapes=[pltpu.VMEM((lh, 4 * HY_CT), bf16), pltpu.VMEM((lh, w2.shape[1]), f32),
                        pltpu.VMEM((lh, w2.shape[1]), f32)],
        compiler_params=_cparams(("arbitrary", "arbitrary")),
        name="hy_filter",
    )(feat_e, feat_o, w1p, b1, w2, b2, fq, w3, w3, dc, dc, fre, fim, cph, sph)


def _short_conv(x, w_ref, b_ref, seg):
    rows = x.shape[0]
    r = lax.broadcasted_iota(jnp.int32, x.shape, 0) & (seg - 1)
    xp = jnp.where(r == 0, 0.0, pltpu.roll(x, 1, 0))
    xn = jnp.where(r == seg - 1, 0.0, pltpu.roll(x, rows - 1, 0))
    return xp * w_ref[0:1, :] + x * w_ref[1:2, :] + xn * w_ref[2:3, :] + b_ref[...]


HY_PB = 256


def _longconv_kernel(yin_ref, gate_ref, cwy_ref, cby_ref, cwg_ref, cbg_ref, fre_ref, fim_ref, iim_ref,
                     fea_ref, feb_ref, foa_ref, fob_ref, ga_ref, gb_ref, ny_ref, dsk_ref,
                     o_ref, y32_ref, ybf_ref, acc_ref, *, seg, conv_yin, tk):
    j = pl.program_id(1)
    g = GRP
    seq = yin_ref.shape[0]
    pb = min(HY_PB, seq)
    hb = pb // 2
    pr = lax.broadcasted_iota(jnp.int32, (pb, pb), 0)
    pc = lax.broadcasted_iota(jnp.int32, (pb, pb), 1)

    @pl.when(j == 0)
    def _():
        y = yin_ref[...]
        if conv_yin:
            y = _short_conv(y, cwy_ref, cby_ref, seg)
        y32_ref[...] = y
        split = jnp.where(pc == jnp.where(pr < hb, 2 * pr, 2 * (pr - hb) + 1), 1.0, 0.0).astype(bf16)
        for k in range(seq // pb):
            t = jnp.dot(split, y[k * pb:(k + 1) * pb, :].astype(bf16), preferred_element_type=f32).astype(bf16)
            ybf_ref[k * hb:(k + 1) * hb, 0:g] = t[0:hb, :]
            ybf_ref[k * hb:(k + 1) * hb, g:2 * g] = t[hb:pb, :]
        acc_ref[...] = jnp.zeros_like(acc_ref)

    freq = pl.ds(pl.multiple_of(j * tk, tk), tk)
    yb = ybf_ref[...]
    r = jnp.dot(fre_ref[freq, :], yb, preferred_element_type=f32)
    i = jnp.dot(fim_ref[freq, :], yb, preferred_element_type=f32)
    first = jnp.logical_and(lax.broadcasted_iota(jnp.int32, (tk, g), 0) == 0, j == 0)
    fe_a, fo_a, g_a = fea_ref[...], foa_ref[...], ga_ref[...]
    f_e = (fe_a, feb_ref[...], jnp.where(first, ny_ref[0:1, :], fe_a))
    f_o = (fo_a, fob_ref[...], jnp.where(first, ny_ref[1:2, :], fo_a))
    f_g = (g_a, gb_ref[...], jnp.where(first, ny_ref[2:3, :], g_a))
    er, ei, orr, oi = r[:, 0:g], i[:, 0:g], r[:, g:2 * g], i[:, g:2 * g]
    p1, p2 = _pk_mul(er, ei, *f_e), _pk_mul(orr, oi, *f_g)
    p3, p4 = _pk_mul(er, ei, *f_o), _pk_mul(orr, oi, *f_e)
    zr = jnp.concatenate([p1[0] + p2[0], p3[0] + p4[0]], axis=1).astype(bf16)
    zi = jnp.concatenate([p1[1] + p2[1], p3[1] + p4[1]], axis=1).astype(bf16)
    inv = jnp.concatenate([fre_ref[:, freq], iim_ref[:, freq]], axis=1)
    acc_ref[...] += jnp.dot(inv, jnp.concatenate([zr, zi], axis=0), preferred_element_type=f32)

    @pl.when(j == pl.num_programs(1) - 1)
    def _():
        gate = _short_conv(gate_ref[...], cwg_ref, cbg_ref, seg)
        d = dsk_ref[...]
        merge = jnp.where(pr == jnp.where(pc < hb, 2 * pc, 2 * (pc - hb) + 1), 1.0, 0.0).astype(bf16)
        for k in range(seq // pb):
            stacked = jnp.concatenate([acc_ref[k * hb:(k + 1) * hb, 0:g], acc_ref[k * hb:(k + 1) * hb, g:2 * g]], axis=0)
            conv = _dot_exact_lhs(merge, stacked)
            rows = slice(k * pb, (k + 1) * pb)
            o_ref[rows, :] = gate[rows, :] * (conv + d * y32_ref[rows, :])


def _longconv(yin, yin_cb, gate_arr, gate_cb, cw, cb, part_y, part_g, fre, fim, iim, filt, dsk, order,
              *, nbatch, seq, seg, conv_yin):
    lh = seq // 2
    tk = min(HY_TK, lh)
    once = pl.Buffered(1)
    cw_spec = lambda p: pl.BlockSpec((HY_SHORT, GRP), lambda b, j: (0, p))
    cb_spec = lambda p: pl.BlockSpec((1, GRP), lambda b, j: (0, p))
    dft_spec = pl.BlockSpec((lh, lh), lambda b, j: (0, 0), pipeline_mode=once)
    filt_spec = pl.BlockSpec((tk, GRP), lambda b, j: (j, order))
    return pl.pallas_call(
        functools.partial(_longconv_kernel, seg=seg, conv_yin=conv_yin, tk=tk),
        out_shape=jax.ShapeDtypeStruct((nbatch * seq, GRP), f32),
        grid=(nbatch, lh // tk),
        in_specs=[pl.BlockSpec((seq, GRP), lambda b, j: (b, yin_cb)),
                  pl.BlockSpec((seq, GRP), lambda b, j: (b, gate_cb)),
                  cw_spec(part_y), cb_spec(part_y), cw_spec(part_g), cb_spec(part_g),
                  dft_spec, dft_spec, dft_spec,
                  filt_spec, filt_spec, filt_spec, filt_spec, filt_spec, filt_spec,
                  pl.BlockSpec((SUBLANES, GRP), lambda b, j: (0, order)),
                  pl.BlockSpec((None, 1, GRP), lambda b, j: (order, 0, 0))],
        out_specs=pl.BlockSpec((seq, GRP), lambda b, j: (b, 0)),
        scratch_shapes=[pltpu.VMEM((seq, GRP), f32), pltpu.VMEM((lh, 2 * GRP), bf16), pltpu.VMEM((lh, 2 * GRP), f32)],
        compiler_params=_cparams(("parallel", "arbitrary")),
        name="hy_longconv",
    )(yin, gate_arr, cw, cb, cw, cb, fre, fim, iim, *filt, dsk)


def _out_kernel(m0_ref, m1_ref, y2f_ref, y2b_ref, m3_ref, gw_ref, gb_ref, x_ref, mod_ref, w_ref, g_ref, b_ref, o_ref, *, alpha):
    tm = x_ref.shape[0]
    nh = OUT_SPLIT if tm % (OUT_SPLIT * SUBLANES) == 0 else 1
    for k in range(nh):
        rows = slice(k * (tm // nh), (k + 1) * (tm // nh))
        y = y2f_ref[rows, :] + y2b_ref[rows, :]
        z = y * (0.5 * (1.0 + jnp.tanh(math.sqrt(2.0 / math.pi) * (y + 0.044715 * (y * y * y)))))
        m2 = z * jax.nn.sigmoid(_bdot(z, gw_ref[...]) + gb_ref[...])
        acc = _bdot(m0_ref[rows, :], w_ref[0 * GRP:1 * GRP, :])
        acc += _bdot(m1_ref[rows, :], w_ref[1 * GRP:2 * GRP, :])
        acc += _bdot(m2, w_ref[2 * GRP:3 * GRP, :])
        acc += _bdot(m3_ref[rows, :], w_ref[3 * GRP:4 * GRP, :])
        r = alpha * x_ref[rows, :] + mod_ref[2:3, :] * acc
        o_ref[rows, :] = _ln(r) * g_ref[...] + b_ref[...]


def _out_proj(mixes, s5_spec, glu_w, glu_b, x, mod, layer, row_fn, w_out, g, b, tm, alpha):
    t, d = x.shape
    mspec = pl.BlockSpec((tm, GRP), lambda i: (i, 0))
    const = lambda a: pl.BlockSpec(a.shape, lambda i: (0,) * a.ndim)
    return pl.pallas_call(
        functools.partial(_out_kernel, alpha=alpha),
        out_shape=jax.ShapeDtypeStruct((t, d), f32),
        grid=(t // tm,),
        in_specs=[mspec, mspec, s5_spec, s5_spec, mspec, const(glu_w), const(glu_b),
                  pl.BlockSpec((tm, d), lambda i: (i, 0)),
                  pl.BlockSpec((None, None, 6, d), lambda i: (layer, row_fn(i), 0, 0)),
                  pl.BlockSpec((d, d), lambda i: (0, 0), pipeline_mode=pl.Buffered(1)),
                  pl.BlockSpec((1, d), lambda i: (0, 0)),
                  pl.BlockSpec((1, d), lambda i: (0, 0))],
        out_specs=pl.BlockSpec((tm, d), lambda i: (i, 0)),
        compiler_params=_cparams(("parallel",)),
        name="out_proj",
    )(*mixes, glu_w, glu_b, x, mod, w_out, g, b)


def _mlp_kernel(x_ref, mod_ref, wu_ref, wd_ref, g_ref, b_ref, o_ref, h_ref, acc_ref, *, alpha):
    j = pl.program_id(1)

    @pl.when(j == 0)
    def _():
        y = _ln(x_ref[...])
        h_ref[...] = (y * (1.0 + mod_ref[4:5, :]) + mod_ref[3:4, :]).astype(bf16)
        acc_ref[...] = jnp.zeros_like(acc_ref)

    u = jnp.maximum(jnp.dot(h_ref[...], wu_ref[...], preferred_element_type=f32), 0.0)
    acc_ref[...] += jnp.dot((u * u).astype(bf16), wd_ref[...], preferred_element_type=f32)

    @pl.when(j == pl.num_programs(1) - 1)
    def _():
        r = alpha * x_ref[...] + mod_ref[5:6, :] * acc_ref[...]
        o_ref[...] = _ln(r) * g_ref[...] + b_ref[...]


def _mlp(x, mod, layer, row_fn, w_up, w_down, g, b, tm, tf, alpha):
    t, d = x.shape
    dff = w_up.shape[1]
    return pl.pallas_call(
        functools.partial(_mlp_kernel, alpha=alpha),
        out_shape=jax.ShapeDtypeStruct((t, d), f32),
        grid=(t // tm, dff // tf),
        in_specs=[pl.BlockSpec((tm, d), lambda i, j: (i, 0)),
                  pl.BlockSpec((None, None, 6, d), lambda i, j: (layer, row_fn(i), 0, 0)),
                  pl.BlockSpec((d, tf), lambda i, j: (0, j)),
                  pl.BlockSpec((tf, d), lambda i, j: (j, 0)),
                  pl.BlockSpec((1, d), lambda i, j: (0, 0)),
                  pl.BlockSpec((1, d), lambda i, j: (0, 0))],
        out_specs=pl.BlockSpec((tm, d), lambda i, j: (i, 0)),
        scratch_shapes=[pltpu.VMEM((tm, d), bf16), pltpu.VMEM((tm, d), f32)],
        compiler_params=_cparams(("parallel", "arbitrary")),
        name="mlp",
    )(x, mod, w_up, w_down, g, b)


def _perm_w_in(w):
    d = w.shape[0]
    hk, hv = HEADS * DK, HEADS * DV
    o = 0
    parts = {}
    for name, width in (("gq", hk), ("gk", hk), ("gv", hv), ("gg", GRP), ("glr", 2 * GLA_RANK),
                        ("rq", hk), ("rk", hk), ("rv", hv), ("rg", GRP), ("su", GRP), ("hy", 3 * GRP)):
        parts[name] = w[:, o:o + width]
        o += width
    qk = lambda q, k: jnp.concatenate([q.reshape(d, HEADS, DK), k.reshape(d, HEADS, DK)], axis=-1).reshape(d, 2 * hk)
    cols = [qk(parts["gq"], parts["gk"]), parts["gv"], parts["gg"],
            qk(parts["rq"], parts["rk"]), parts["rv"], parts["rg"],
            parts["su"], parts["hy"], parts["glr"]]
    wp = jnp.concatenate(cols, axis=1)
    return jnp.pad(wp, ((0, 0), (0, NP_IN - wp.shape[1]))).astype(bf16)


def _gla_gate_params(w_gate, b_gate, norm_w):
    r = w_gate.shape[1]
    dup = lambda a: jnp.concatenate([a, a], axis=-1).reshape(*a.shape[:-2], 2 * HEADS * DK)
    wg = dup(w_gate.reshape(2, r, HEADS, DK))
    full = jnp.zeros((2, LANES, 2 * HEADS * DK), f32)
    for d in range(2):
        full = full.at[d, d * r:(d + 1) * r, :].set(wg[d])
    return full.astype(bf16), dup(b_gate.reshape(2, 1, HEADS, DK)), jnp.tile(norm_w[None, :], (1, HEADS))


def _s5_block_weights(bbre, bbim, c_re, c_im):
    _, g, hch, p = bbre.shape
    ntile = g // S5_GPT
    eye = jnp.eye(S5_GPT, dtype=f32)

    def in_map(b):
        b = b.reshape(2, ntile, S5_GPT, hch, p)
        return jnp.einsum("dtghp,gk->dtghkp", b, eye).reshape(2, ntile, S5_GPT * hch, S5_GPT * p)

    def out_map(c):
        c = c.reshape(2, ntile, S5_GPT, hch, p)
        return jnp.einsum("dtghp,gk->dtgpkh", c, eye).reshape(2, ntile, S5_GPT * p, S5_GPT * hch)

    wb = jnp.concatenate([in_map(bbre), in_map(bbim)], axis=-1).astype(bf16)
    return wb, out_map(c_re).astype(bf16), out_map(c_im).astype(bf16)


@functools.lru_cache(maxsize=None)
def _dft_tables(seq):
    n = 2 * seq
    k = np.arange(seq, dtype=np.int64)[:, None]
    s1 = np.arange(seq // 64, dtype=np.int64)[None, :]
    s2 = np.arange(64, dtype=np.int64)[None, :]
    a = 2.0 * np.pi * ((k * s1 * 64) % n) / n
    b = 2.0 * np.pi * ((k * s2) % n) / n
    return tuple(np.asarray(t, np.float32) for t in (np.cos(a), np.sin(a), np.cos(b), np.sin(b)))


def _dft_mats(seq):
    ca, sa, cb, sb = (jnp.asarray(t) for t in _dft_tables(seq))
    cosm = (ca[:, :, None] * cb[:, None, :] - sa[:, :, None] * sb[:, None, :]).reshape(seq, seq)
    sinm = (sa[:, :, None] * cb[:, None, :] + ca[:, :, None] * sb[:, None, :]).reshape(seq, seq)
    sign = jnp.where((jnp.arange(seq) & 1) == 0, 1.0, -1.0).astype(f32)
    ri = jnp.arange(seq)[:, None]
    ci = jnp.arange(seq)[None, :]
    fim = jnp.where(ri == 0, sign[None, :], -sinm)
    iim = jnp.where(ci == 0, sign[:, None], -sinm)
    return cosm.astype(bf16), fim.astype(bf16), iim.astype(bf16)


@functools.lru_cache(maxsize=None)
def _hy_feats(seq):
    t = np.linspace(0.0, 1.0, seq)[:, None]
    w = 2.0 * math.pi * np.arange(seq, dtype=np.float64)[:, None] / seq
    fr = np.linspace(1e-4, HY_BANDS - 1.0, HY_BANDS)[None, :]
    feats = np.concatenate([t, np.cos(fr * w), -np.sin(fr * w)], axis=-1).astype(np.float32)
    return np.pad(feats, ((0, 0), (0, LANES - feats.shape[1])))


@functools.lru_cache(maxsize=None)
def _hy_twiddles(lh):
    ang = np.pi * np.arange(lh, dtype=np.float64)[:, None] / lh
    return np.cos(ang).astype(np.float32), np.sin(ang).astype(np.float32)


def _layer_group(x, mod, layer, p, state, *, nbatch, seq, rows, tm, row_of, want_state, alpha):
    tm_dense = min(DENSE_TM, nbatch * seq)
    proj, u_t, hy = _in_proj(x, mod, layer, row_of(tm_dense), p["w_in"], tm_dense, seq)

    o_gla, s_gla = _linattn(proj, OFF_GLA, hy, (p["gla_wg"], p["gla_bg"], p["gla_nw"]),
                            None if state is None else state["gla"], layer,
                            mode="gla", nbatch=nbatch, seq=seq, want_state=want_state)
    o_ret, s_ret = _linattn(proj, OFF_RET, None, (p["ret_dexp"],),
                            None if state is None else state["ret"], layer,
                            mode="ret", nbatch=nbatch, seq=seq, want_state=want_state)

    u_t = u_t.reshape(seq, nbatch, GRP)
    h0 =None if state is None else (state["s5_re"][layer], state["s5_im"][layer])
    y_f, y_b, hf_re, hf_im = _s5_scan(u_t, p["s5_wb"], p["s5_lam"], p["s5_cre"], p["s5_cim"], p["s5_d"], h0,
                                      want_state=want_state)
    y_f, y_b = y_f.reshape(seq, nbatch * GRP), y_b.reshape(seq, nbatch * GRP)
    tpb = seq // tm
    s5_spec = pl.BlockSpec((tm, GRP), lambda i: (i % tpb, i // tpb))

    lh = seq // 2
    fre, fim, iim = _dft_mats(lh)
    feats = _hy_feats(seq)
    cph, sph = (jnp.broadcast_to(jnp.asarray(t), (lh, HY_CT)) for t in _hy_twiddles(lh))
    filt = _hy_filter(jnp.asarray(feats[0::2]), jnp.asarray(feats[1::2]), p["hy_w1"], p["hy_b1"], p["hy_w2"],
                      p["hy_b2"], p["hy_freq"], p["hy_w3"], p["hy_decay"], fre, fim, cph, sph)
    seg = seq // rows
    y1 = _longconv(hy, 0, hy, 1, p["hy_cw"], p["hy_cb"], 0, 1, fre, fim, iim, filt,
                   p["hy_d"], 0, nbatch=nbatch, seq=seq, seg=seg, conv_yin=True)
    o_hy = _longconv(y1, 0, hy, 2, p["hy_cw"], p["hy_cb"], 0, 2, fre, fim, iim, filt,
                     p["hy_d"], 1, nbatch=nbatch, seq=seq, seg=seg, conv_yin=False)

    x = _out_proj((o_gla, o_ret, y_f, y_b, o_hy), s5_spec, p["s5_glu_w"], p["s5_glu_b"], x, mod, layer, row_of(tm), p["w_out"],
                  p["ln1_g"], p["ln1_b"], tm, alpha)
    x = _mlp(x, mod, layer, row_of(tm_dense), p["w_up"], p["w_down"], p["ln2_g"], p["ln2_b"], tm_dense, MLP_TF, alpha)
    return x, (s_gla, s_ret, hf_re, hf_im)


def kernel(x_prompt, x_sample, state_gla, state_ret, state_s5_re, state_s5_im, c, c_ctx, ada_w, ada_b, w_in, gla_w_gate, gla_b_gate, gla_norm_w, ret_decay_exp, s5_a_re, s5_a_im, s5_log_step, s5_b_re, s5_b_im, s5_c_re, s5_c_im, s5_d, s5_glu_w, s5_glu_b, hy_conv_w, hy_conv_b, hy_f_w1, hy_f_b1, hy_f_w2, hy_f_b2, hy_f_freq, hy_f_w3, hy_decay, hy_d, w_out, ln1_g, ln1_b, w_up, w_down, ln2_g, ln2_b):
    bp, lp, d = x_prompt.shape
    bs, ls, _ = x_sample.shape
    depth = w_in.shape[0]
    alpha = (2 * depth) ** 0.25
    ngroup = s5_a_re.shape[2]
    ntile = ngroup // S5_GPT

    nrow = -(-(1 + bs) // SUBLANES) * SUBLANES
    c_rows = jnp.concatenate([c_ctx[None, :], c, jnp.zeros((nrow - 1 - bs, d), f32)], axis=0)
    mod = _ada(c_rows, ada_w, ada_b).reshape(depth, nrow, 6, d)

    tr_state = lambda s: jnp.swapaxes(s, -1, -2)
    s5_state = lambda s: jnp.transpose(s.reshape(bs, depth, 2, ntile, S5_SW), (1, 2, 3, 0, 4))
    lat_state = dict(gla=tr_state(state_gla), ret=tr_state(state_ret),
                     s5_re=s5_state(state_s5_re), s5_im=s5_state(state_s5_im))

    tm_ctx = min(lp, DENSE_TM)
    tm_lat = min(ls, DENSE_TM)
    yp = x_prompt.reshape(bp * lp, d)
    ys = x_sample.reshape(bs * ls, d)
    outs = []
    for l in range(depth):
        lre, lim, bbre, bbim = _s5_prep(s5_a_re[l], s5_a_im[l], s5_log_step[l], s5_b_re[l], s5_b_im[l])
        wb, cre, cim = _s5_block_weights(bbre, bbim, s5_c_re[l], s5_c_im[l])
        lam = jnp.stack([lre.reshape(2, ntile, S5_SW), lim.reshape(2, ntile, S5_SW)], axis=2)
        wg, bg, nw = _gla_gate_params(gla_w_gate[l], gla_b_gate[l], gla_norm_w[l])
        w1p = jnp.pad(hy_f_w1[l], ((0, LANES - hy_f_w1.shape[1]), (0, 0)))
        p = dict(
            w_in=_perm_w_in(w_in[l]),
            gla_wg=wg, gla_bg=bg, gla_nw=nw,
            ret_dexp=jnp.broadcast_to(ret_decay_exp[l][:, None, :, None], (2, SUBLANES, HEADS, LANES)).reshape(2, SUBLANES, GRP),
            s5_wb=wb, s5_lam=lam, s5_cre=cre, s5_cim=cim, s5_d=s5_d[l][None, :],
            s5_glu_w=s5_glu_w[l].astype(bf16), s5_glu_b=s5_glu_b[l][None, :],
            hy_cw=hy_conv_w[l], hy_cb=hy_conv_b[l][None, :],
            hy_w1=w1p, hy_b1=hy_f_b1[l][None, :], hy_w2=hy_f_w2[l], hy_b2=hy_f_b2[l][None, :],
            hy_freq=hy_f_freq[l][None, :], hy_w3=hy_f_w3[l], hy_decay=hy_decay[l][None, :],
            hy_d=hy_d[l][:, None, :],
            w_out=w_out[l].astype(bf16), ln1_g=ln1_g[l][None, :], ln1_b=ln1_b[l][None, :],
            w_up=w_up[l].astype(bf16), w_down=w_down[l].astype(bf16),
            ln2_g=ln2_g[l][None, :], ln2_b=ln2_b[l][None, :],
        )
        yp, st = _layer_group(yp, mod, l, p, None, nbatch=bp, seq=lp, rows=1, tm=tm_ctx,
                              row_of=lambda tile: (lambda i: 0), want_state=True, alpha=alpha)
        outs.append(st)
        ys, _ = _layer_group(ys, mod, l, p, lat_state, nbatch=bs, seq=ls, rows=ls // GRID_W, tm=tm_lat,
                             row_of=lambda tile: (lambda i: 1 + i // (ls // tile)), want_state=False, alpha=alpha)

    new_gla = jnp.stack([jnp.swapaxes(o[0], -1, -2) for o in outs], axis=1)
    new_ret = jnp.stack([jnp.swapaxes(o[1], -1, -2) for o in outs], axis=1)
    unpack = lambda h: jnp.transpose(h, (2, 0, 1, 3)).reshape(bp, 2, ngroup, S5_STATE)
    new_re = jnp.stack([unpack(o[2]) for o in outs], axis=1)
    new_im = jnp.stack([unpack(o[3]) for o in outs], axis=1)
    return (yp.reshape(bp, lp, d), ys.reshape(bs, ls, d), new_gla, new_ret, new_re, new_im)
```

```python
import functools
import math

import numpy as np
import jax
import jax.numpy as jnp
from jax import lax
from jax.experimental import pallas as pl
from jax.experimental.pallas import tpu as pltpu

f32 = jnp.float32
bf16 = jnp.bfloat16

GRID_W = 64
HEADS = 4
DK = 64
DV = 128
GLA_RANK = 16
GLA_TAU = 16.0
S5_GROUP = 16
S5_STATE = 64
HY_BANDS = 16
HY_SHORT = 3
CHUNK = 64
RET_CHUNK = 128
LN_EPS = 1e-5
NORM_EPS = 1e-6

LANES = 128
SUBLANES = 8
VMEM_LIMIT = 52 * 1024 * 1024

GRP = HEADS * DV
OFF_GLA = 0
OFF_RET = 3 * GRP
OFF_S5 = 6 * GRP
OFF_HY = 7 * GRP
OFF_LR = 10 * GRP
NP_IN = 5376
TN_IN = 1792
DENSE_TM = 512
MLP_TF = 1024
OUT_SPLIT = 2


def _cparams(sem, vmem=VMEM_LIMIT):
    return pltpu.CompilerParams(dimension_semantics=sem, vmem_limit_bytes=vmem)


def _ln(x):
    mu = jnp.mean(x, axis=-1, keepdims=True)
    xc = x - mu
    var = jnp.mean(xc * xc, axis=-1, keepdims=True)
    return xc * lax.rsqrt(var + LN_EPS)


def _silu(x):
    return x * jax.nn.sigmoid(x)


def _bdot(a, b):
    return jnp.dot(a.astype(bf16), b.astype(bf16), preferred_element_type=f32)


def _bdot_nt(a, b):
    return lax.dot_general(a.astype(bf16), b.astype(bf16), (((1,), (1,)), ((), ())), preferred_element_type=f32)


def _bdot_tn(a, b):
    return lax.dot_general(a.astype(bf16), b.astype(bf16), (((0,), (0,)), ((), ())), preferred_element_type=f32)


def _dot_exact_lhs(a_bf, x):
    hi = x.astype(bf16)
    lo = (x - hi.astype(f32)).astype(bf16)
    d = lambda b: jnp.dot(a_bf, b, preferred_element_type=f32)
    return d(hi) + d(lo)


def _dot3(a, b):
    ah = a.astype(bf16)
    al = (a - ah.astype(f32)).astype(bf16)
    bh = b.astype(bf16)
    bl = (b - bh.astype(f32)).astype(bf16)
    d = lambda p, q: jnp.dot(p, q, preferred_element_type=f32)
    return d(ah, bh) + (d(al, bh) + d(ah, bl))


def _ada_kernel(c_ref, w_ref, b_ref, o_ref):
    s = _silu(c_ref[...])
    o_ref[...] = _bdot(s, w_ref[...]) + b_ref[...]


def _ada(c_rows, ada_w, ada_b):
    depth, d, n = ada_w.shape
    r = c_rows.shape[0]
    tn = 2048
    return pl.pallas_call(
        _ada_kernel,
        out_shape=jax.ShapeDtypeStruct((depth, r, n), f32),
        grid=(depth, n // tn),
        in_specs=[pl.BlockSpec((r, d), lambda l, j: (0, 0)),
                  pl.BlockSpec((None, d, tn), lambda l, j: (l, 0, j)),
                  pl.BlockSpec((None, 1, tn), lambda l, j: (l, 0, j))],
        out_specs=pl.BlockSpec((None, r, tn), lambda l, j: (l, 0, j)),
        compiler_params=_cparams(("parallel", "parallel")),
        name="ada",
    )(c_rows, ada_w, ada_b.reshape(depth, 1, n))


def _in_kernel(x_ref, mod_ref, w_ref, o_ref, u_ref, hy_ref, h_ref, *, nsub, sub):
    j = pl.program_id(1)

    @pl.when(j == 0)
    def _():
        y = _ln(x_ref[...])
        h_ref[...] = (y * (1.0 + mod_ref[1:2, :]) + mod_ref[0:1, :]).astype(bf16)

    cols = pl.ds(pl.multiple_of(j * TN_IN, LANES), TN_IN)
    o_ref[...] = jnp.dot(h_ref[...], w_ref[:, cols], preferred_element_type=f32)

    @pl.when(j == OFF_S5 // TN_IN)
    def _():
        c0 = OFF_S5 % TN_IN
        for k in range(nsub):
            u_ref[:, k * GRP:(k + 1) * GRP] = o_ref[k * sub:(k + 1) * sub, c0:c0 + GRP]

    @pl.when(j == OFF_HY // TN_IN)
    def _():
        c0 = OFF_HY % TN_IN
        hy_ref[...] = o_ref[:, c0:c0 + 3 * GRP]


def _in_proj(x, mod, layer, row_fn, w_perm, tm, seq):
    t, d = x.shape
    assert OFF_S5 // TN_IN == (OFF_S5 + GRP - 1) // TN_IN
    assert OFF_HY // TN_IN == (OFF_HY + 3 * GRP - 1) // TN_IN
    sub = min(tm, seq)
    nsub = tm // sub
    tps = seq // sub
    return pl.pallas_call(
        functools.partial(_in_kernel, nsub=nsub, sub=sub),
        out_shape=[jax.ShapeDtypeStruct((t, NP_IN), f32), jax.ShapeDtypeStruct((seq, (t // seq) * GRP), f32),
                   jax.ShapeDtypeStruct((t, 3 * GRP), f32)],
        grid=(t // tm, NP_IN // TN_IN),
        in_specs=[pl.BlockSpec((tm, d), lambda i, j: (i, 0)),
                  pl.BlockSpec((None, None, 6, d), lambda i, j: (layer, row_fn(i), 0, 0)),
                  pl.BlockSpec((d, NP_IN), lambda i, j: (0, 0), pipeline_mode=pl.Buffered(1))],
        out_specs=[pl.BlockSpec((tm, TN_IN), lambda i, j: (i, j)),
                   pl.BlockSpec((sub, nsub * GRP), lambda i, j: (i % tps, i // tps)),
                   pl.BlockSpec((tm, 3 * GRP), lambda i, j: (i, 0))],
        scratch_shapes=[pltpu.VMEM((tm, d), bf16)],
        compiler_params=_cparams(("parallel", "arbitrary")),
        name="in_proj",
    )(x, mod, w_perm)


LA_TAIL = 256


def _linattn_kernel(*refs, mode, seq, has_s0, want_state):
    it = iter(refs)
    qk_ref, v_ref, g_ref = next(it), next(it), next(it)
    if mode == "gla":
        lr_ref, wg_ref, bg_ref, nw_ref = next(it), next(it), next(it), next(it)
    else:
        dexp_ref = next(it)
    s0_ref = next(it) if has_s0 else None
    o_ref = next(it)
    sf_ref = next(it) if want_state else None
    ob_ref, st_ref = next(it), next(it)
    if mode == "gla":
        qkd_ref, ku_ref, dec_ref = next(it), next(it), next(it)

    c = CHUNK if mode == "gla" else RET_CHUNK
    nchunk = seq // c
    scale = DK ** -0.5
    row = lax.broadcasted_iota(jnp.int32, (c, c), 0)
    col = lax.broadcasted_iota(jnp.int32, (c, c), 1)
    is_k = (lax.broadcasted_iota(jnp.int32, (1, GRP), 1) & DK) != 0
    hq = lambda a, h: a[:, h * LANES:h * LANES + DK]
    hk = lambda a, h: a[:, h * LANES + DK:(h + 1) * LANES]
    hv = lambda h: slice(h * DV, (h + 1) * DV)

    if has_s0:
        st_ref[...] = s0_ref[...]
    else:
        st_ref[...] = jnp.zeros_like(st_ref)

    per_dir = []
    for d in (0, 1):
        causal = (row >= col) if d == 0 else (row <= col)
        if mode == "gla":
            tri = jnp.where(causal, 1.0, 0.0).astype(bf16)
            sgn = jnp.where(is_k, -1.0, 1.0)
            qsc = jnp.where(is_k, 1.0, scale)
            per_dir.append((causal, tri, sgn, qsc))
        else:
            lgr = jnp.log1p(-jnp.exp2(-dexp_ref[d]))[0:1, :]
            rowg = lax.broadcasted_iota(jnp.int32, (c, GRP), 0)
            pos = rowg if d == 0 else (c - 1 - rowg)
            pw = jnp.where(is_k, c - 1 - pos, pos + 1).astype(f32)
            wqk = jnp.exp(lgr * pw) * jnp.where(is_k, scale, 1.0)
            dist = ((row - col) if d == 0 else (col - row)).astype(f32)
            masks = [jnp.where(causal, jnp.exp(jnp.broadcast_to(lgr[:, h * LANES:h * LANES + c], (c, c)) * dist), 0.0) * scale
                     for h in range(HEADS)]
            dec = jnp.exp(lgr * float(c))
            per_dir.append((wqk, masks, dec))

    if mode == "gla":
        def pre(n, carry):
            rows = pl.ds(pl.multiple_of(n * c, c), c)
            qk = qk_ref[rows, :]
            lrc = lr_ref[rows, :]
            logits = [_bdot(lrc, wg_ref[d]) + bg_ref[d] for d in (0, 1)]
            la = [(jnp.minimum(x, 0.0) - jnp.log(1.0 + jnp.exp(-jnp.abs(x)))) * (1.0 / GLA_TAU) for x in logits]
            b = [_dot_exact_lhs(per_dir[d][1], la[d]) for d in (0, 1)]
            for d in (0, 1):
                _, _, sgn, qsc = per_dir[d]
                last = c - 1 if d == 0 else 0
                btot = b[d][last:last + 1, :]
                qkd_ref[d, rows, :] = (qk * (jnp.exp(b[d] * sgn) * qsc)).astype(bf16)
                ku_ref[d, rows, :] = (qk * jnp.exp(btot - b[d])).astype(bf16)
                dec_ref[d, n] = jnp.exp(btot)
            return carry

        lax.fori_loop(0, nchunk, pre, 0, unroll=2)

    def body(i, carry):
        rows, qka, qko, ku, dec = [], [], [], [], []
        for d in (0, 1):
            n = i if d == 0 else nchunk - 1 - i
            rows.append(pl.ds(pl.multiple_of(n * c, c), c))
            if mode == "gla":
                qka.append(qkd_ref[d, rows[d], :])
                qko.append(qka[d])
                ku.append(ku_ref[d, rows[d], :])
                dec.append(dec_ref[d, n])
            else:
                wqk, _, dcy = per_dir[d]
                qka.append(qk_ref[rows[d], :])
                qko.append(qka[d] * wqk)
                ku.append(qko[d])
                dec.append(dcy)
        chains = [(d, h) for d in (0, 1) for h in range(HEADS)]
        v = {(d, h): v_ref[rows[d], hv(h)] for d, h in chains}
        st = {(d, h): st_ref[d, h] for d, h in chains}
        att = {(d, h): _bdot_nt(hq(qka[d], h), hk(qka[d], h)) for d, h in chains}
        oi = {(d, h): _bdot_nt(hq(qko[d], h), st[d, h]) for d, h in chains}
        up = {(d, h): _bdot_tn(v[d, h], hk(ku[d], h)) for d, h in chains}
        for d, h in chains:
            if mode == "gla":
                a = jnp.where(per_dir[d][0], att[d, h], 0.0)
            else:
                a = att[d, h] * per_dir[d][1][h]
            o = _bdot(a, v[d, h]) + oi[d, h]
            st_ref[d, h] = hq(dec[d], h) * st[d, h] + up[d, h]
            if d == 0:
                o_ref[rows[d], hv(h)] = o
            else:
                ob_ref[rows[d], hv(h)] = o
        return carry

    lax.fori_loop(0, nchunk, body, 0, unroll=4 if mode == "gla" else 2)
    if want_state:
        sf_ref[...] = st_ref[...]

    tl = min(LA_TAIL, seq)

    def tail(i, carry):
        rows = pl.ds(pl.multiple_of(i * tl, tl), tl)
        for h in range(HEADS):
            o = o_ref[rows, hv(h)] + ob_ref[rows, hv(h)]
            if mode == "gla":
                o = o * lax.rsqrt(jnp.mean(o * o, axis=-1, keepdims=True) + NORM_EPS) * nw_ref[:, hv(h)]
            else:
                o = _ln(o)
            o_ref[rows, hv(h)] = o * _silu(g_ref[rows, hv(h)])
        return carry

    lax.fori_loop(0, seq // tl, tail, 0)


def _linattn(proj, col0, extra, s0t, layer, *, mode, nbatch, seq, want_state):
    cb = col0 // GRP
    full = lambda a: pl.BlockSpec(a.shape, lambda b: (0,) * a.ndim)
    in_specs = [pl.BlockSpec((seq, GRP), lambda b: (b, cb)),
                pl.BlockSpec((seq, GRP), lambda b: (b, cb + 1)),
                pl.BlockSpec((seq, GRP), lambda b: (b, cb + 2))]
    args = [proj, proj, proj]
    scratch = [pltpu.VMEM((seq, GRP), f32), pltpu.VMEM((2, HEADS, DV, DK), f32)]
    if mode == "gla":
        scratch += [pltpu.VMEM((2, seq, GRP), bf16), pltpu.VMEM((2, seq, GRP), bf16),
                    pltpu.VMEM((2, seq // CHUNK, 1, GRP), f32)]
    if mode == "gla":
        in_specs += [pl.BlockSpec((seq, LANES), lambda b: (b, OFF_LR // LANES))] + [full(a) for a in extra]
        args += [proj, *extra]
    else:
        in_specs += [full(a) for a in extra]
        args += list(extra)
    has_s0 = s0t is not None
    if has_s0:
        in_specs += [pl.BlockSpec((None, None, 2, HEADS, DV, DK), lambda b: (b, layer, 0, 0, 0, 0))]
        args += [s0t]
    out_shape = [jax.ShapeDtypeStruct((nbatch * seq, GRP), f32)]
    out_specs = [pl.BlockSpec((seq, GRP), lambda b: (b, 0))]
    if want_state:
        out_shape += [jax.ShapeDtypeStruct((nbatch, 2, HEADS, DV, DK), f32)]
        out_specs += [pl.BlockSpec((None, 2, HEADS, DV, DK), lambda b: (b, 0, 0, 0, 0))]
    res = pl.pallas_call(
        functools.partial(_linattn_kernel, mode=mode, seq=seq, has_s0=has_s0, want_state=want_state),
        out_shape=out_shape,
        grid=(nbatch,),
        in_specs=in_specs,
        out_specs=out_specs,
        scratch_shapes=scratch,
        compiler_params=_cparams(("parallel",)),
        name="linattn_" + mode,
    )(*args)
    return (res[0], res[1]) if want_state else (res[0], None)


def _s5_prep_kernel(are_ref, aim_ref, lstep_ref, bre_ref, bim_ref, lre_ref, lim_ref, bbre_ref, bbim_ref):
    ar, ai = are_ref[...], aim_ref[...]
    st = jnp.exp(lstep_ref[...])
    mag = jnp.exp(ar * st)
    lr = mag * jnp.cos(ai * st)
    li = mag * jnp.sin(ai * st)
    nr, ni = lr - 1.0, li
    den = ar * ar + ai * ai
    kr = (nr * ar + ni * ai) / den
    ki = (ni * ar - nr * ai) / den
    br, bi = bre_ref[...], bim_ref[...]
    lre_ref[...] = lr
    lim_ref[...] = li
    bbre_ref[...] = kr * br - ki * bi
    bbim_ref[...] = kr * bi + ki * br


def _s5_prep(a_re, a_im, log_step, b_re, b_im):
    _, g, p = a_re.shape
    hch = b_re.shape[-1]
    rows = 2 * g * hch
    bc = lambda a: jnp.broadcast_to(a[:, :, None, :], (2, g, hch, p)).reshape(rows, p)
    tr = lambda b: jnp.transpose(b, (0, 1, 3, 2)).reshape(rows, p)
    ls = jnp.broadcast_to(log_step[:, :, None, None], (2, g, hch, p)).reshape(rows, p)
    spec = pl.BlockSpec((rows, p), lambda: (0, 0))
    lre, lim, bbre, bbim = pl.pallas_call(
        _s5_prep_kernel,
        out_shape=[jax.ShapeDtypeStruct((rows, p), f32)] * 4,
        in_specs=[spec] * 5,
        out_specs=[spec] * 4,
        name="s5_prep",
    )(bc(a_re), bc(a_im), ls, tr(b_re), tr(b_im))
    r4 = lambda a: a.reshape(2, g, hch, p)
    return r4(lre)[:, :, 0, :], r4(lim)[:, :, 0, :], r4(bbre), r4(bbim)


S5_TC = 64
S5_GPT = LANES // S5_GROUP
S5_SW = S5_GPT * S5_STATE


S5_TB = 512


def _s5_scan_kernel(*refs, tb, nb, has_h0, want_state):
    it = iter(refs)
    uf_ref, ub_ref, wb_ref, lam_ref, cre_ref, cim_ref, dsk_ref = (next(it) for _ in range(7))
    h0re_ref = next(it) if has_h0 else None
    h0im_ref = next(it) if has_h0 else None
    yf_ref, yb_ref = next(it), next(it)
    hfre_ref = next(it) if want_state else None
    hfim_ref = next(it) if want_state else None
    bu_ref, hs_ref, st_ref = next(it), next(it), next(it)

    s = pl.program_id(1)
    tc = S5_TC
    nchunk = tb // tc
    sw = S5_SW
    u_refs, y_refs = (uf_ref, ub_ref), (yf_ref, yb_ref)

    @pl.when(s == 0)
    def _():
        if has_h0:
            for d in (0, 1):
                st_ref[2 * d] = h0re_ref[d]
                st_ref[2 * d + 1] = h0im_ref[d]
        else:
            st_ref[...] = jnp.zeros_like(st_ref)

    lam = [(jnp.broadcast_to(lam_ref[d, 0:1, :], (nb, sw)), jnp.broadcast_to(lam_ref[d, 1:2, :], (nb, sw)))
           for d in (0, 1)]

    def chunk_pair(ip, carry):
        h = [list(carry[0:2]), list(carry[2:4])]
        for par in (0, 1):
            c = 2 * ip + par
            t0 = [pl.multiple_of(c * tc, tc), pl.multiple_of((nchunk - 1 - c) * tc, tc)]
            ub = [u_refs[d][pl.ds(t0[d], tc)] for d in (0, 1)]
            for d in (0, 1):
                bu_ref[par, d] = _bdot(ub[d].reshape(tc * nb, LANES), wb_ref[d])
            for k in range(tc):
                for d in (0, 1):
                    t = k if d == 0 else tc - 1 - k
                    rs = slice(t * nb, (t + 1) * nb)
                    lam_r, lam_i = lam[d]
                    hre, him = h[d]
                    nre = lam_r * hre - lam_i * him + bu_ref[par, d, rs, 0:sw]
                    nim = lam_r * him + lam_i * hre + bu_ref[par, d, rs, sw:2 * sw]
                    h[d] = [nre, nim]
                    hs_ref[par, d, rs, 0:sw] = nre
                    hs_ref[par, d, rs, sw:2 * sw] = nim
            for d in (0, 1):
                y = _bdot(hs_ref[par, d, :, 0:sw], cre_ref[d]) - _bdot(hs_ref[par, d, :, sw:2 * sw], cim_ref[d])
                y = y.reshape(tc, nb, LANES)
                if d == 0:
                    y = y + ub[0] * dsk_ref[...]
                y_refs[d][pl.ds(t0[d], tc)] = y
        return (*h[0], *h[1])

    hfin = lax.fori_loop(0, nchunk // 2, chunk_pair, tuple(st_ref[q] for q in range(4)))
    for q in range(4):
        st_ref[q] = hfin[q]

    if want_state:
        @pl.when(s == pl.num_programs(1) - 1)
        def _():
            for d in (0, 1):
                hfre_ref[d] = hfin[2 * d]
                hfim_ref[d] = hfin[2 * d + 1]


def _s5_scan(u_t, wb, lam, cre, cim, dsk, h0, *, want_state):
    seq, nb, ch = u_t.shape
    ntile = ch // LANES
    sw = S5_SW
    tb = min(S5_TB, seq)
    nblk = seq // tb
    fwd = lambda j, s: (s, 0, j)
    bwd = lambda j, s: (nblk - 1 - s, 0, j)
    par = lambda shape: pl.BlockSpec(shape, lambda j, s: (0, j, 0, 0))
    in_specs = [pl.BlockSpec((tb, nb, LANES), fwd), pl.BlockSpec((tb, nb, LANES), bwd),
                par((2, None, LANES, 2 * sw)), par((2, None, 2, sw)),
                par((2, None, sw, LANES)), par((2, None, sw, LANES)),
                pl.BlockSpec((1, LANES), lambda j, s: (0, j))]
    args = [u_t, u_t, wb, lam, cre, cim, dsk]
    has_h0 = h0 is not None
    st_spec = par((2, None, nb, sw))
    if has_h0:
        in_specs += [st_spec, st_spec]
        args += list(h0)
    out_shape = [jax.ShapeDtypeStruct((seq, nb, ch), f32)] * 2
    out_specs = [pl.BlockSpec((tb, nb, LANES), fwd), pl.BlockSpec((tb, nb, LANES), bwd)]
    if want_state:
        out_shape += [jax.ShapeDtypeStruct((2, ntile, nb, sw), f32)] * 2
        out_specs += [st_spec, st_spec]
    res = pl.pallas_call(
        functools.partial(_s5_scan_kernel, tb=tb, nb=nb, has_h0=has_h0, want_state=want_state),
        out_shape=out_shape,
        grid=(ntile, nblk),
        in_specs=in_specs,
        out_specs=out_specs,
        scratch_shapes=[pltpu.VMEM((2, 2, S5_TC * nb, 2 * sw), f32), pltpu.VMEM((2, 2, S5_TC * nb, 2 * sw), f32),
                        pltpu.VMEM((4, nb, sw), f32)],
        compiler_params=_cparams(("parallel", "arbitrary")),
        name="s5_scan",
    )(*args)
    return (res[0], res[1], res[2], res[3]) if want_state else (res[0], res[1], None, None)


HY_TK = 256
HY_CT = 256


def _pk_mul(xr, xi, a, b, d):
    return xr * a - xi * b, xr * b + xi * d


def _hy_filter_kernel(fte_ref, fto_ref, w1_ref, b1_ref, w2_ref, b2_ref, fq_ref, w3f_ref, w3b_ref, dcf_ref, dcb_ref,
                      fre_ref, fim_ref, cph_ref, sph_ref,
                      fea_ref, feb_ref, foa_ref, fob_ref, ga_ref, gb_ref, ny_ref, xs_ref, he_ref, ho_ref, *, lh, tk):
    j = pl.program_id(1)
    ct = HY_CT

    @pl.when(jnp.logical_and(j == 0, pl.program_id(0) == 0))
    def _():
        fq = fq_ref[...]
        for feat_ref, h_ref in ((fte_ref, he_ref), (fto_ref, ho_ref)):
            h = jnp.sin(fq * (_dot3(feat_ref[...], w1_ref[...]) + b1_ref[...]))
            h_ref[...] = jnp.sin(fq * (_dot3(h, w2_ref[...]) + b2_ref[...]))

    @pl.when(j == 0)
    def _():
        he, te = he_ref[...], fte_ref[:, 0:1]
        ho, to = ho_ref[...], fto_ref[:, 0:1]
        raw = lambda h, t, w3_ref, dc_ref: _dot3(h, w3_ref[...]) * jnp.exp(-t * jnp.abs(dc_ref[...]))
        ffe, ffo = raw(he, te, w3f_ref, dcf_ref), raw(ho, to, w3f_ref, dcf_ref)
        fbe, fbo = raw(he, te, w3b_ref, dcb_ref), raw(ho, to, w3b_ref, dcb_ref)
        sf = jnp.sum(jnp.abs(ffe), axis=0, keepdims=True) + jnp.sum(jnp.abs(ffo), axis=0, keepdims=True)
        sb = jnp.sum(jnp.abs(fbe), axis=0, keepdims=True) + jnp.sum(jnp.abs(fbo), axis=0, keepdims=True)
        rowi = lax.broadcasted_iota(jnp.int32, (lh, ct), 0)
        xs_ref[:, 0 * ct:1 * ct] = (ffe / sf).astype(bf16)
        xs_ref[:, 1 * ct:2 * ct] = jnp.where(rowi == 0, 0.0, fbe / sb).astype(bf16)
        xs_ref[:, 2 * ct:3 * ct] = (ffo / sf).astype(bf16)
        xs_ref[:, 3 * ct:4 * ct] = (fbo / sb).astype(bf16)

    xs = xs_ref[...]
    r = jnp.dot(fre_ref[...], xs, preferred_element_type=f32)
    i = jnp.dot(fim_ref[...], xs, preferred_element_type=f32)
    r1, r2, r3, r4 = (r[:, q * ct:(q + 1) * ct] for q in range(4))
    i1, i2, i3, i4 = (i[:, q * ct:(q + 1) * ct] for q in range(4))
    c, s = cph_ref[...], sph_ref[...]
    fe_re, fe_im = r1 + r2, i1 - i2
    fo_re = r3 + c * r4 + s * i4
    fo_im = i3 + s * r4 - c * i4
    g_re = c * fo_re + s * fo_im
    g_im = c * fo_im - s * fo_re
    scl = 1.0 / lh
    first = jnp.logical_and(lax.broadcasted_iota(jnp.int32, (tk, ct), 0) == 0, j == 0)
    for a_ref, b_ref, re, im in ((fea_ref, feb_ref, fe_re, fe_im), (foa_ref, fob_ref, fo_re, fo_im),
                                 (ga_ref, gb_ref, g_re, g_im)):
        a_ref[...] = jnp.where(first, 0.5 * scl, scl) * re
        b_ref[...] = jnp.where(first, 0.0, scl * im)

    @pl.when(j == 0)
    def _():
        fo_ny = i3[0:1, :] - i4[0:1, :]
        ny_ref[...] = jnp.zeros_like(ny_ref)
        ny_ref[0:1, :] = (0.5 * scl) * (i1[0:1, :] + i2[0:1, :])
        ny_ref[1:2, :] = (0.5 * scl) * fo_ny
        ny_ref[2:3, :] = (-0.5 * scl) * fo_ny


def _hy_filter(feat_e, feat_o, w1p, b1, w2, b2, fq, w3, dc, fre, fim, cph, sph):
    lh = feat_e.shape[0]
    ncol = w3.shape[1] // 2
    nct = ncol // HY_CT
    tk = min(HY_TK, lh)
    full = lambda a: pl.BlockSpec(a.shape, lambda c, j: (0,) * a.ndim)
    out_spec = pl.BlockSpec((tk, HY_CT), lambda c, j: (j, c))
    return pl.pallas_call(
        functools.partial(_hy_filter_kernel, lh=lh, tk=tk),
        out_shape=[jax.ShapeDtypeStruct((lh, ncol), f32)] * 6 + [jax.ShapeDtypeStruct((SUBLANES, ncol), f32)],
        grid=(nct, lh // tk),
        in_specs=[full(feat_e), full(feat_o), full(w1p), full(b1), full(w2), full(b2), full(fq),
                  pl.BlockSpec((w3.shape[0], HY_CT), lambda c, j: (0, c)),
                  pl.BlockSpec((w3.shape[0], HY_CT), lambda c, j: (0, nct + c)),
                  pl.BlockSpec((1, HY_CT), lambda c, j: (0, c)),
                  pl.BlockSpec((1, HY_CT), lambda c, j: (0, nct + c)),
                  pl.BlockSpec((tk, lh), lambda c, j: (j, 0)),
                  pl.BlockSpec((tk, lh), lambda c, j: (j, 0)),
                  pl.BlockSpec((tk, HY_CT), lambda c, j: (j, 0)),
                  pl.BlockSpec((tk, HY_CT), lambda c, j: (j, 0))],
        out_specs=[out_spec] * 6 + [pl.BlockSpec((SUBLANES, HY_CT), lambda c, j: (0, c))],
        scratch_shapes=[pltpu.VMEM((lh, 4 * HY_CT), bf16), pltpu.VMEM((lh, w2.shape[1]), f32),
                        pltpu.VMEM((lh, w2.shape[1]), f32)],
        compiler_params=_cparams(("arbitrary", "arbitrary")),
        name="hy_filter",
    )(feat_e, feat_o, w1p, b1, w2, b2, fq, w3, w3, dc, dc, fre, fim, cph, sph)


def _short_conv(x, w_ref, b_ref, seg):
    rows = x.shape[0]
    r = lax.broadcasted_iota(jnp.int32, x.shape, 0) & (seg - 1)
    xp = jnp.where(r == 0, 0.0, pltpu.roll(x, 1, 0))
    xn = jnp.where(r == seg - 1, 0.0, pltpu.roll(x, rows - 1, 0))
    return xp * w_ref[0:1, :] + x * w_ref[1:2, :] + xn * w_ref[2:3, :] + b_ref[...]


HY_PB = 256


def _longconv_kernel(yin_ref, gate_ref, cwy_ref, cby_ref, cwg_ref, cbg_ref, fre_ref, fim_ref, iim_ref,
                     fea_ref, feb_ref, foa_ref, fob_ref, ga_ref, gb_ref, ny_ref, dsk_ref,
                     o_ref, y32_ref, ybf_ref, acc_ref, *, seg, conv_yin, tk):
    j = pl.program_id(1)
    g = GRP
    seq = yin_ref.shape[0]
    pb = min(HY_PB, seq)
    hb = pb // 2
    pr = lax.broadcasted_iota(jnp.int32, (pb, pb), 0)
    pc = lax.broadcasted_iota(jnp.int32, (pb, pb), 1)

    @pl.when(j == 0)
    def _():
        y = yin_ref[...]
        if conv_yin:
            y = _short_conv(y, cwy_ref, cby_ref, seg)
        y32_ref[...] = y
        split = jnp.where(pc == jnp.where(pr < hb, 2 * pr, 2 * (pr - hb) + 1), 1.0, 0.0).astype(bf16)
        for k in range(seq // pb):
            t = jnp.dot(split, y[k * pb:(k + 1) * pb, :].astype(bf16), preferred_element_type=f32).astype(bf16)
            ybf_ref[k * hb:(k + 1) * hb, 0:g] = t[0:hb, :]
            ybf_ref[k * hb:(k + 1) * hb, g:2 * g] = t[hb:pb, :]
        acc_ref[...] = jnp.zeros_like(acc_ref)

    freq = pl.ds(pl.multiple_of(j * tk, tk), tk)
    yb = ybf_ref[...]
    r = jnp.dot(fre_ref[freq, :], yb, preferred_element_type=f32)
    i = jnp.dot(fim_ref[freq, :], yb, preferred_element_type=f32)
    first = jnp.logical_and(lax.broadcasted_iota(jnp.int32, (tk, g), 0) == 0, j == 0)
    fe_a, fo_a, g_a = fea_ref[...], foa_ref[...], ga_ref[...]
    f_e = (fe_a, feb_ref[...], jnp.where(first, ny_ref[0:1, :], fe_a))
    f_o = (fo_a, fob_ref[...], jnp.where(first, ny_ref[1:2, :], fo_a))
    f_g = (g_a, gb_ref[...], jnp.where(first, ny_ref[2:3, :], g_a))
    er, ei, orr, oi = r[:, 0:g], i[:, 0:g], r[:, g:2 * g], i[:, g:2 * g]
    p1, p2 = _pk_mul(er, ei, *f_e), _pk_mul(orr, oi, *f_g)
    p3, p4 = _pk_mul(er, ei, *f_o), _pk_mul(orr, oi, *f_e)
    zr = jnp.concatenate([p1[0] + p2[0], p3[0] + p4[0]], axis=1).astype(bf16)
    zi = jnp.concatenate([p1[1] + p2[1], p3[1] + p4[1]], axis=1).astype(bf16)
    inv = jnp.concatenate([fre_ref[:, freq], iim_ref[:, freq]], axis=1)
    acc_ref[...] += jnp.dot(inv, jnp.concatenate([zr, zi], axis=0), preferred_element_type=f32)

    @pl.when(j == pl.num_programs(1) - 1)
    def _():
        gate = _short_conv(gate_ref[...], cwg_ref, cbg_ref, seg)
        d = dsk_ref[...]
        merge = jnp.where(pr == jnp.where(pc < hb, 2 * pc, 2 * (pc - hb) + 1), 1.0, 0.0).astype(bf16)
        for k in range(seq // pb):
            stacked = jnp.concatenate([acc_ref[k * hb:(k + 1) * hb, 0:g], acc_ref[k * hb:(k + 1) * hb, g:2 * g]], axis=0)
            conv = _dot_exact_lhs(merge, stacked)
            rows = slice(k * pb, (k + 1) * pb)
            o_ref[rows, :] = gate[rows, :] * (conv + d * y32_ref[rows, :])


def _longconv(yin, yin_cb, gate_arr, gate_cb, cw, cb, part_y, part_g, fre, fim, iim, filt, dsk, order,
              *, nbatch, seq, seg, conv_yin):
    lh = seq // 2
    tk = min(HY_TK, lh)
    once = pl.Buffered(1)
    cw_spec = lambda p: pl.BlockSpec((HY_SHORT, GRP), lambda b, j: (0, p))
    cb_spec = lambda p: pl.BlockSpec((1, GRP), lambda b, j: (0, p))
    dft_spec = pl.BlockSpec((lh, lh), lambda b, j: (0, 0), pipeline_mode=once)
    filt_spec = pl.BlockSpec((tk, GRP), lambda b, j: (j, order))
    return pl.pallas_call(
        functools.partial(_longconv_kernel, seg=seg, conv_yin=conv_yin, tk=tk),
        out_shape=jax.ShapeDtypeStruct((nbatch * seq, GRP), f32),
        grid=(nbatch, lh // tk),
        in_specs=[pl.BlockSpec((seq, GRP), lambda b, j: (b, yin_cb)),
                  pl.BlockSpec((seq, GRP), lambda b, j: (b, gate_cb)),
                  cw_spec(part_y), cb_spec(part_y), cw_spec(part_g), cb_spec(part_g),
                  dft_spec, dft_spec, dft_spec,
                  filt_spec, filt_spec, filt_spec, filt_spec, filt_spec, filt_spec,
                  pl.BlockSpec((SUBLANES, GRP), lambda b, j: (0, order)),
                  pl.BlockSpec((None, 1, GRP), lambda b, j: (order, 0, 0))],
        out_specs=pl.BlockSpec((seq, GRP), lambda b, j: (b, 0)),
        scratch_shapes=[pltpu.VMEM((seq, GRP), f32), pltpu.VMEM((lh, 2 * GRP), bf16), pltpu.VMEM((lh, 2 * GRP), f32)],
        compiler_params=_cparams(("parallel", "arbitrary")),
        name="hy_longconv",
    )(yin, gate_arr, cw, cb, cw, cb, fre, fim, iim, *filt, dsk)


def _out_kernel(m0_ref, m1_ref, y2f_ref, y2b_ref, m3_ref, gw_ref, gb_ref, x_ref, mod_ref, w_ref, g_ref, b_ref, o_ref, *, alpha):
    tm = x_ref.shape[0]
    nh = OUT_SPLIT if tm % (OUT_SPLIT * SUBLANES) == 0 else 1
    for k in range(nh):
        rows = slice(k * (tm // nh), (k + 1) * (tm // nh))
        y = y2f_ref[rows, :] + y2b_ref[rows, :]
        z = y * (0.5 * (1.0 + jnp.tanh(math.sqrt(2.0 / math.pi) * (y + 0.044715 * (y * y * y)))))
        m2 = z * jax.nn.sigmoid(_bdot(z, gw_ref[...]) + gb_ref[...])
        acc = _bdot(m0_ref[rows, :], w_ref[0 * GRP:1 * GRP, :])
        acc += _bdot(m1_ref[rows, :], w_ref[1 * GRP:2 * GRP, :])
        acc += _bdot(m2, w_ref[2 * GRP:3 * GRP, :])
        acc += _bdot(m3_ref[rows, :], w_ref[3 * GRP:4 * GRP, :])
        r = alpha * x_ref[rows, :] + mod_ref[2:3, :] * acc
        o_ref[rows, :] = _ln(r) * g_ref[...] + b_ref[...]


def _out_proj(mixes, s5_spec, glu_w, glu_b, x, mod, layer, row_fn, w_out, g, b, tm, alpha):
    t, d = x.shape
    mspec = pl.BlockSpec((tm, GRP), lambda i: (i, 0))
    const = lambda a: pl.BlockSpec(a.shape, lambda i: (0,) * a.ndim)
    return pl.pallas_call(
        functools.partial(_out_kernel, alpha=alpha),
        out_shape=jax.ShapeDtypeStruct((t, d), f32),
        grid=(t // tm,),
        in_specs=[mspec, mspec, s5_spec, s5_spec, mspec, const(glu_w), const(glu_b),
                  pl.BlockSpec((tm, d), lambda i: (i, 0)),
                  pl.BlockSpec((None, None, 6, d), lambda i: (layer, row_fn(i), 0, 0)),
                  pl.BlockSpec((d, d), lambda i: (0, 0), pipeline_mode=pl.Buffered(1)),
                  pl.BlockSpec((1, d), lambda i: (0, 0)),
                  pl.BlockSpec((1, d), lambda i: (0, 0))],
        out_specs=pl.BlockSpec((tm, d), lambda i: (i, 0)),
        compiler_params=_cparams(("parallel",)),
        name="out_proj",
    )(*mixes, glu_w, glu_b, x, mod, w_out, g, b)


def _mlp_kernel(x_ref, mod_ref, wu_ref, wd_ref, g_ref, b_ref, o_ref, h_ref, acc_ref, *, alpha):
    j = pl.program_id(1)

    @pl.when(j == 0)
    def _():
        y = _ln(x_ref[...])
        h_ref[...] = (y * (1.0 + mod_ref[4:5, :]) + mod_ref[3:4, :]).astype(bf16)
        acc_ref[...] = jnp.zeros_like(acc_ref)

    u = jnp.maximum(jnp.dot(h_ref[...], wu_ref[...], preferred_element_type=f32), 0.0)
    acc_ref[...] += jnp.dot((u * u).astype(bf16), wd_ref[...], preferred_element_type=f32)

    @pl.when(j == pl.num_programs(1) - 1)
    def _():
        r = alpha * x_ref[...] + mod_ref[5:6, :] * acc_ref[...]
        o_ref[...] = _ln(r) * g_ref[...] + b_ref[...]


def _mlp(x, mod, layer, row_fn, w_up, w_down, g, b, tm, tf, alpha):
    t, d = x.shape
    dff = w_up.shape[1]
    return pl.pallas_call(
        functools.partial(_mlp_kernel, alpha=alpha),
        out_shape=jax.ShapeDtypeStruct((t, d), f32),
        grid=(t // tm, dff // tf),
        in_specs=[pl.BlockSpec((tm, d), lambda i, j: (i, 0)),
                  pl.BlockSpec((None, None, 6, d), lambda i, j: (layer, row_fn(i), 0, 0)),
                  pl.BlockSpec((d, tf), lambda i, j: (0, j)),
                  pl.BlockSpec((tf, d), lambda i, j: (j, 0)),
                  pl.BlockSpec((1, d), lambda i, j: (0, 0)),
                  pl.BlockSpec((1, d), lambda i, j: (0, 0))],
        out_specs=pl.BlockSpec((tm, d), lambda i, j: (i, 0)),
        scratch_shapes=[pltpu.VMEM((tm, d), bf16), pltpu.VMEM((tm, d), f32)],
        compiler_params=_cparams(("parallel", "arbitrary")),
        name="mlp",
    )(x, mod, w_up, w_down, g, b)


def _perm_w_in(w):
    d = w.shape[0]
    hk, hv = HEADS * DK, HEADS * DV
    o = 0
    parts = {}
    for name, width in (("gq", hk), ("gk", hk), ("gv", hv), ("gg", GRP), ("glr", 2 * GLA_RANK),
                        ("rq", hk), ("rk", hk), ("rv", hv), ("rg", GRP), ("su", GRP), ("hy", 3 * GRP)):
        parts[name] = w[:, o:o + width]
        o += width
    qk = lambda q, k: jnp.concatenate([q.reshape(d, HEADS, DK), k.reshape(d, HEADS, DK)], axis=-1).reshape(d, 2 * hk)
    cols = [qk(parts["gq"], parts["gk"]), parts["gv"], parts["gg"],
            qk(parts["rq"], parts["rk"]), parts["rv"], parts["rg"],
            parts["su"], parts["hy"], parts["glr"]]
    wp = jnp.concatenate(cols, axis=1)
    return jnp.pad(wp, ((0, 0), (0, NP_IN - wp.shape[1]))).astype(bf16)


def _gla_gate_params(w_gate, b_gate, norm_w):
    r = w_gate.shape[1]
    dup = lambda a: jnp.concatenate([a, a], axis=-1).reshape(*a.shape[:-2], 2 * HEADS * DK)
    wg = dup(w_gate.reshape(2, r, HEADS, DK))
    full = jnp.zeros((2, LANES, 2 * HEADS * DK), f32)
    for d in range(2):
        full = full.at[d, d * r:(d + 1) * r, :].set(wg[d])
    return full.astype(bf16), dup(b_gate.reshape(2, 1, HEADS, DK)), jnp.tile(norm_w[None, :], (1, HEADS))


def _s5_block_weights(bbre, bbim, c_re, c_im):
    _, g, hch, p = bbre.shape
    ntile = g // S5_GPT
    eye = jnp.eye(S5_GPT, dtype=f32)

    def in_map(b):
        b = b.reshape(2, ntile, S5_GPT, hch, p)
        return jnp.einsum("dtghp,gk->dtghkp", b, eye).reshape(2, ntile, S5_GPT * hch, S5_GPT * p)

    def out_map(c):
        c = c.reshape(2, ntile, S5_GPT, hch, p)
        return jnp.einsum("dtghp,gk->dtgpkh", c, eye).reshape(2, ntile, S5_GPT * p, S5_GPT * hch)

    wb = jnp.concatenate([in_map(bbre), in_map(bbim)], axis=-1).astype(bf16)
    return wb, out_map(c_re).astype(bf16), out_map(c_im).astype(bf16)


@functools.lru_cache(maxsize=None)
def _dft_tables(seq):
    n = 2 * seq
    k = np.arange(seq, dtype=np.int64)[:, None]
    s1 = np.arange(seq // 64, dtype=np.int64)[None, :]
    s2 = np.arange(64, dtype=np.int64)[None, :]
    a = 2.0 * np.pi * ((k * s1 * 64) % n) / n
    b = 2.0 * np.pi * ((k * s2) % n) / n
    return tuple(np.asarray(t, np.float32) for t in (np.cos(a), np.sin(a), np.cos(b), np.sin(b)))


def _dft_mats(seq):
    ca, sa, cb, sb = (jnp.asarray(t) for t in _dft_tables(seq))
    cosm = (ca[:, :, None] * cb[:, None, :] - sa[:, :, None] * sb[:, None, :]).reshape(seq, seq)
    sinm = (sa[:, :, None] * cb[:, None, :] + ca[:, :, None] * sb[:, None, :]).reshape(seq, seq)
    sign = jnp.where((jnp.arange(seq) & 1) == 0, 1.0, -1.0).astype(f32)
    ri = jnp.arange(seq)[:, None]
    ci = jnp.arange(seq)[None, :]
    fim = jnp.where(ri == 0, sign[None, :], -sinm)
    iim = jnp.where(ci == 0, sign[:, None], -sinm)
    return cosm.astype(bf16), fim.astype(bf16), iim.astype(bf16)


@functools.lru_cache(maxsize=None)
def _hy_feats(seq):
    t = np.linspace(0.0, 1.0, seq)[:, None]
    w = 2.0 * math.pi * np.arange(seq, dtype=np.float64)[:, None] / seq
    fr = np.linspace(1e-4, HY_BANDS - 1.0, HY_BANDS)[None, :]
    feats = np.concatenate([t, np.cos(fr * w), -np.sin(fr * w)], axis=-1).astype(np.float32)
    return np.pad(feats, ((0, 0), (0, LANES - feats.shape[1])))


@functools.lru_cache(maxsize=None)
def _hy_twiddles(lh):
    ang = np.pi * np.arange(lh, dtype=np.float64)[:, None] / lh
    return np.cos(ang).astype(np.float32), np.sin(ang).astype(np.float32)


def _layer_group(x, mod, layer, p, state, *, nbatch, seq, rows, tm, row_of, want_state, alpha):
    tm_dense = min(DENSE_TM, nbatch * seq)
    proj, u_t, hy = _in_proj(x, mod, layer, row_of(tm_dense), p["w_in"], tm_dense, seq)

    o_gla, s_gla = _linattn(proj, OFF_GLA, (p["gla_wg"], p["gla_bg"], p["gla_nw"]),
                            None if state is None else state["gla"], layer,
                            mode="gla", nbatch=nbatch, seq=seq, want_state=want_state)
    o_ret, s_ret = _linattn(proj, OFF_RET, (p["ret_dexp"],),
                            None if state is None else state["ret"], layer,
                            mode="ret", nbatch=nbatch, seq=seq, want_state=want_state)

    u_t = u_t.reshape(seq, nbatch, GRP)
    h0 =None if state is None else (state["s5_re"][layer], state["s5_im"][layer])
    y_f, y_b, hf_re, hf_im = _s5_scan(u_t, p["s5_wb"], p["s5_lam"], p["s5_cre"], p["s5_cim"], p["s5_d"], h0,
                                      want_state=want_state)
    y_f, y_b = y_f.reshape(seq, nbatch * GRP), y_b.reshape(seq, nbatch * GRP)
    tpb = seq // tm
    s5_spec = pl.BlockSpec((tm, GRP), lambda i: (i % tpb, i // tpb))

    lh = seq // 2
    fre, fim, iim = _dft_mats(lh)
    feats = _hy_feats(seq)
    cph, sph = (jnp.broadcast_to(jnp.asarray(t), (lh, HY_CT)) for t in _hy_twiddles(lh))
    filt = _hy_filter(jnp.asarray(feats[0::2]), jnp.asarray(feats[1::2]), p["hy_w1"], p["hy_b1"], p["hy_w2"],
                      p["hy_b2"], p["hy_freq"], p["hy_w3"], p["hy_decay"], fre, fim, cph, sph)
    seg = seq // rows
    y1 = _longconv(hy, 0, hy, 1, p["hy_cw"], p["hy_cb"], 0, 1, fre, fim, iim, filt,
                   p["hy_d"], 0, nbatch=nbatch, seq=seq, seg=seg, conv_yin=True)
    o_hy = _longconv(y1, 0, hy, 2, p["hy_cw"], p["hy_cb"], 0, 2, fre, fim, iim, filt,
                     p["hy_d"], 1, nbatch=nbatch, seq=seq, seg=seg, conv_yin=False)

    x = _out_proj((o_gla, o_ret, y_f, y_b, o_hy), s5_spec, p["s5_glu_w"], p["s5_glu_b"], x, mod, layer, row_of(tm), p["w_out"],
                  p["ln1_g"], p["ln1_b"], tm, alpha)
    x = _mlp(x, mod, layer, row_of(tm_dense), p["w_up"], p["w_down"], p["ln2_g"], p["ln2_b"], tm_dense, MLP_TF, alpha)
    return x, (s_gla, s_ret, hf_re, hf_im)


def kernel(x_prompt, x_sample, state_gla, state_ret, state_s5_re, state_s5_im, c, c_ctx, ada_w, ada_b, w_in, gla_w_gate, gla_b_gate, gla_norm_w, ret_decay_exp, s5_a_re, s5_a_im, s5_log_step, s5_b_re, s5_b_im, s5_c_re, s5_c_im, s5_d, s5_glu_w, s5_glu_b, hy_conv_w, hy_conv_b, hy_f_w1, hy_f_b1, hy_f_w2, hy_f_b2, hy_f_freq, hy_f_w3, hy_decay, hy_d, w_out, ln1_g, ln1_b, w_up, w_down, ln2_g, ln2_b):
    bp, lp, d = x_prompt.shape
    bs, ls, _ = x_sample.shape
    depth = w_in.shape[0]
    alpha = (2 * depth) ** 0.25
    ngroup = s5_a_re.shape[2]
    ntile = ngroup // S5_GPT

    nrow = -(-(1 + bs) // SUBLANES) * SUBLANES
    c_rows = jnp.concatenate([c_ctx[None, :], c, jnp.zeros((nrow - 1 - bs, d), f32)], axis=0)
    mod = _ada(c_rows, ada_w, ada_b).reshape(depth, nrow, 6, d)

    tr_state = lambda s: jnp.swapaxes(s, -1, -2)
    s5_state = lambda s: jnp.transpose(s.reshape(bs, depth, 2, ntile, S5_SW), (1, 2, 3, 0, 4))
    lat_state = dict(gla=tr_state(state_gla), ret=tr_state(state_ret),
                     s5_re=s5_state(state_s5_re), s5_im=s5_state(state_s5_im))

    tm_ctx = min(lp, DENSE_TM)
    tm_lat = min(ls, DENSE_TM)
    yp = x_prompt.reshape(bp * lp, d)
    ys = x_sample.reshape(bs * ls, d)
    outs = []
    for l in range(depth):
        lre, lim, bbre, bbim = _s5_prep(s5_a_re[l], s5_a_im[l], s5_log_step[l], s5_b_re[l], s5_b_im[l])
        wb, cre, cim = _s5_block_weights(bbre, bbim, s5_c_re[l], s5_c_im[l])
        lam = jnp.stack([lre.reshape(2, ntile, S5_SW), lim.reshape(2, ntile, S5_SW)], axis=2)
        wg, bg, nw = _gla_gate_params(gla_w_gate[l], gla_b_gate[l], gla_norm_w[l])
        w1p = jnp.pad(hy_f_w1[l], ((0, LANES - hy_f_w1.shape[1]), (0, 0)))
        p = dict(
            w_in=_perm_w_in(w_in[l]),
            gla_wg=wg, gla_bg=bg, gla_nw=nw,
            ret_dexp=jnp.broadcast_to(ret_decay_exp[l][:, None, :, None], (2, SUBLANES, HEADS, LANES)).reshape(2, SUBLANES, GRP),
            s5_wb=wb, s5_lam=lam, s5_cre=cre, s5_cim=cim, s5_d=s5_d[l][None, :],
            s5_glu_w=s5_glu_w[l].astype(bf16), s5_glu_b=s5_glu_b[l][None, :],
            hy_cw=hy_conv_w[l], hy_cb=hy_conv_b[l][None, :],
            hy_w1=w1p, hy_b1=hy_f_b1[l][None, :], hy_w2=hy_f_w2[l], hy_b2=hy_f_b2[l][None, :],
            hy_freq=hy_f_freq[l][None, :], hy_w3=hy_f_w3[l], hy_decay=hy_decay[l][None, :],
            hy_d=hy_d[l][:, None, :],
            w_out=w_out[l].astype(bf16), ln1_g=ln1_g[l][None, :], ln1_b=ln1_b[l][None, :],
            w_up=w_up[l].astype(bf16), w_down=w_down[l].astype(bf16),
            ln2_g=ln2_g[l][None, :], ln2_b=ln2_b[l][None, :],
        )
        yp, st = _layer_group(yp, mod, l, p, None, nbatch=bp, seq=lp, rows=1, tm=tm_ctx,
                              row_of=lambda tile: (lambda i: 0), want_state=True, alpha=alpha)
        outs.append(st)
        ys, _ = _layer_group(ys, mod, l, p, lat_state, nbatch=bs, seq=ls, rows=ls // GRID_W, tm=tm_lat,
                             row_of=lambda tile: (lambda i: 1 + i // (ls // tile)), want_state=False, alpha=alpha)

    new_gla = jnp.stack([jnp.swapaxes(o[0], -1, -2) for o in outs], axis=1)
    new_ret = jnp.stack([jnp.swapaxes(o[1], -1, -2) for o in outs], axis=1)
    unpack = lambda h: jnp.transpose(h, (2, 0, 1, 3)).reshape(bp, 2, ngroup, S5_STATE)
    new_re = jnp.stack([unpack(o[2]) for o in outs], axis=1)
    new_im = jnp.stack([unpack(o[3]) for o in outs], axis=1)
    return (yp.reshape(bp, lp, d), ys.reshape(bs, ls, d), new_gla, new_ret, new_re, new_im)
```
